```python
import math
import jax, jax.numpy as jnp
from jax import lax
import numpy as np

D_MODEL = 2048
BATCH = 4
SEQ = 4096
DEPTH = 1

CHUNK = 64
Q_BLOCK = 128
D_MIX = D_MODEL
D_ATTN = D_MIX // 2
D_CONV = D_MIX - D_ATTN
N_DIFF_HEADS = 8
DIFF_QK_DIM = D_ATTN // (2 * N_DIFF_HEADS)
DIFF_V_DIM = 2 * DIFF_QK_DIM
CONV_WIDTH = 3
CONV_GROUPS = 8
CONV_GROUP_DIM = D_CONV // CONV_GROUPS
NUM_BUCKETS = 32
MAX_DISTANCE = 128
N_GROUPS = 4
EXPERTS_PER_GROUP = 8
N_EXPERTS = N_GROUPS * EXPERTS_PER_GROUP
TOP_K = 2
D_FF_EXPERT = D_MODEL // 4
PLE_DIM = 256
EPS = 1e-6
MASK_VALUE = -1e30

kernel_name = "hybrid_diffattn_shortconv_hmoe_block"


def rms_norm(x, g):
    xf = x.astype(jnp.float32)
    y = xf * lax.rsqrt(jnp.mean(xf * xf, axis=-1, keepdims=True) + EPS)
    return (y * g.astype(jnp.float32)).astype(x.dtype)


def rel_bucket(rel):
    nb = NUM_BUCKETS // 2
    max_exact = nb // 2
    ret = jnp.where(rel > 0, nb, 0).astype(jnp.int32)
    n = jnp.abs(rel)
    nf = jnp.maximum(n, 1).astype(jnp.float32)
    large = max_exact + (jnp.log(nf / max_exact) / math.log(MAX_DISTANCE / max_exact)
                         * (nb - max_exact)).astype(jnp.int32)
    large = jnp.minimum(large, nb - 1)
    return ret + jnp.where(n < max_exact, n, large)


def diff_attention(q, k, v, g_q, g_k, lq1, lk1, lq2, lk2, g_subln, rel_bias, lam_init):
    b, s, _ = q.shape
    q = rms_norm(q.reshape(b, s, N_DIFF_HEADS, 2, DIFF_QK_DIM), g_q)
    k = rms_norm(k.reshape(b, s, N_DIFF_HEADS, 2, DIFF_QK_DIM), g_k)
    v = v.reshape(b, s, N_DIFF_HEADS, DIFF_V_DIM)
    lam = (jnp.exp(jnp.sum(lq1.astype(jnp.float32) * lk1.astype(jnp.float32)))
           - jnp.exp(jnp.sum(lq2.astype(jnp.float32) * lk2.astype(jnp.float32))) + lam_init)
    scale = DIFF_QK_DIM ** -0.5
    k_pos = jnp.arange(s, dtype=jnp.int32)

    def block(i):
        q_start = i * Q_BLOCK
        qs = lax.dynamic_slice_in_dim(q, q_start, Q_BLOCK, axis=1)
        logits = jnp.einsum('bqhcd,bkhcd->bhcqk', qs, k).astype(jnp.float32) * scale
        q_pos = q_start + jnp.arange(Q_BLOCK, dtype=jnp.int32)
        bias = rel_bias[rel_bucket(k_pos[None, :] - q_pos[:, None])]
        bias = jnp.transpose(bias, (2, 0, 1)).astype(jnp.float32)
        mask = (k_pos[None, :] // CHUNK) <= (q_pos[:, None] // CHUNK)
        logits = jnp.where(mask, logits + bias[None, :, None], MASK_VALUE)
        probs = jax.nn.softmax(logits, axis=-1)
        attn = probs[:, :, 0] - lam * probs[:, :, 1]
        return jnp.einsum('bhqk,bkhe->bqhe', attn.astype(v.dtype), v)

    out = lax.map(block, jnp.arange(s // Q_BLOCK))
    out = jnp.transpose(out, (1, 0, 2, 3, 4)).reshape(b, s, N_DIFF_HEADS, DIFF_V_DIM)
    out = rms_norm(out, g_subln) * (1.0 - lam_init)
    return out.reshape(b, s, D_ATTN)


def short_gated_conv(gate_b, gate_c, u, conv_w, conv_b, g_conv):
    b, s, _ = u.shape
    cu = gate_c * u
    kern = conv_w[:, None, :].astype(cu.dtype)
    conv = lax.conv_general_dilated(cu, kern, window_strides=(1,),
                                    padding=[(CONV_WIDTH - 1, 0)],
                                    dimension_numbers=('NWC', 'WIO', 'NWC'),
                                    feature_group_count=D_CONV)
    z = gate_b * (conv + conv_b)
    z = rms_norm(z.reshape(b, s, CONV_GROUPS, CONV_GROUP_DIM),
                 g_conv.reshape(CONV_GROUPS, CONV_GROUP_DIM))
    return z.reshape(b, s, D_CONV)


def hierarchical_moe(xn, w_group, b_group, w_expert, b_expert, w1, w3, w2):
    shape = xn.shape
    t = xn.reshape(-1, shape[-1])
    n_tok = t.shape[0]
    xf = t.astype(jnp.float32)
    group_logits = xf @ w_group.astype(jnp.float32) + b_group.astype(jnp.float32)
    group_probs = jax.nn.softmax(group_logits, axis=-1)
    g = jnp.argmax(group_logits, axis=-1)
    p_g = jnp.take_along_axis(group_probs, g[:, None], axis=1)[:, 0]
    expert_logits = (xf @ w_expert.astype(jnp.float32) + b_expert.astype(jnp.float32)
                     ).reshape(n_tok, N_GROUPS, EXPERTS_PER_GROUP)
    in_group = jnp.take_along_axis(expert_logits, g[:, None, None], axis=1)[:, 0]
    top_v, top_i = lax.top_k(in_group, TOP_K)
    w_top = jax.nn.softmax(top_v, axis=-1) * p_g[:, None]
    expert_id = g[:, None] * EXPERTS_PER_GROUP + top_i
    combine = jnp.sum(jax.nn.one_hot(expert_id, N_EXPERTS, dtype=jnp.float32)
                      * w_top[..., None], axis=1).astype(t.dtype)
    out = jnp.zeros_like(t)
    for e in range(N_EXPERTS):
        hid = jax.nn.silu(t @ w1[e]) * (t @ w3[e])
        out = out + combine[:, e:e + 1] * (hid @ w2[e])
    return out.reshape(shape)


def setup_inputs(seed: int = 0) -> dict:
    key = jax.random.key(seed)
    ks = jax.random.split(key, 32)
    f32 = jnp.float32

    def nrm(k, shape, scale):
        return jax.random.normal(k, shape, f32) * scale

    def gain(k, shape):
        return 1.0 + nrm(k, shape, 0.05)

    d_in = 3 * D_ATTN + 3 * D_CONV
    return {
        "x": nrm(ks[0], (BATCH, SEQ, D_MODEL), 1.0),
        "p": nrm(ks[1], (DEPTH, BATCH, SEQ, PLE_DIM), 1.0),
        "rel_bias": nrm(ks[2], (NUM_BUCKETS, N_DIFF_HEADS), 0.5),
        "g_mix": gain(ks[3], (DEPTH, D_MODEL)),
        "w_in": nrm(ks[4], (DEPTH, D_MODEL, d_in), D_MODEL ** -0.5),
        "g_q": gain(ks[5], (DEPTH, DIFF_QK_DIM)),
        "g_k": gain(ks[6], (DEPTH, DIFF_QK_DIM)),
        "lam_q1": nrm(ks[7], (DEPTH, DIFF_QK_DIM), 0.1),
        "lam_k1": nrm(ks[8], (DEPTH, DIFF_QK_DIM), 0.1),
        "lam_q2": nrm(ks[9], (DEPTH, DIFF_QK_DIM), 0.1),
        "lam_k2": nrm(ks[10], (DEPTH, DIFF_QK_DIM), 0.1),
        "g_subln": gain(ks[11], (DEPTH, DIFF_V_DIM)),
        "conv_w": nrm(ks[12], (DEPTH, CONV_WIDTH, D_CONV), CONV_WIDTH ** -0.5),
        "conv_b": nrm(ks[13], (DEPTH, D_CONV), 0.02),
        "g_conv": gain(ks[14], (DEPTH, D_CONV)),
        "w_o": nrm(ks[15], (DEPTH, D_MIX, D_MODEL), D_MIX ** -0.5),
        "g_ffn": gain(ks[16], (DEPTH, D_MODEL)),
        "w_group": nrm(ks[17], (DEPTH, D_MODEL, N_GROUPS), D_MODEL ** -0.5),
        "b_group": nrm(ks[18], (DEPTH, N_GROUPS), 0.01),
        "w_expert": nrm(ks[19], (DEPTH, D_MODEL, N_EXPERTS), D_MODEL ** -0.5),
        "b_expert": nrm(ks[20], (DEPTH, N_EXPERTS), 0.01),
        "w1": nrm(ks[21], (DEPTH, N_EXPERTS, D_MODEL, D_FF_EXPERT), D_MODEL ** -0.5),
        "w3": nrm(ks[22], (DEPTH, N_EXPERTS, D_MODEL, D_FF_EXPERT), D_MODEL ** -0.5),
        "w2": nrm(ks[23], (DEPTH, N_EXPERTS, D_FF_EXPERT, D_MODEL), D_FF_EXPERT ** -0.5),
        "g_ple": gain(ks[24], (DEPTH, D_MODEL)),
        "w_ple_gate": nrm(ks[25], (DEPTH, D_MODEL, D_MODEL), D_MODEL ** -0.5),
        "w_ple_proj": nrm(ks[26], (DEPTH, PLE_DIM, D_MODEL), PLE_DIM ** -0.5),
    }


def reference(x, p, rel_bias, g_mix, w_in, g_q, g_k, lam_q1, lam_k1, lam_q2, lam_k2,
              g_subln, conv_w, conv_b, g_conv, w_o, g_ffn, w_group, b_group, w_expert,
              b_expert, w1, w3, w2, g_ple, w_ple_gate, w_ple_proj):
    h = x
    splits = [D_ATTN, 2 * D_ATTN, 3 * D_ATTN, 3 * D_ATTN + D_CONV, 3 * D_ATTN + 2 * D_CONV]
    for i in range(DEPTH):
        lam_init = 0.8 - 0.6 * math.exp(-0.3 * i)
        xn = rms_norm(h, g_mix[i])
        proj = xn @ w_in[i]
        q, k, v, gate_b, gate_c, u = jnp.split(proj, splits, axis=-1)
        attn_out = diff_attention(q, k, v, g_q[i], g_k[i], lam_q1[i], lam_k1[i],
                                  lam_q2[i], lam_k2[i], g_subln[i], rel_bias, lam_init)
        conv_out = short_gated_conv(gate_b, gate_c, u, conv_w[i], conv_b[i], g_conv[i])
        h = h + jnp.concatenate([attn_out, conv_out], axis=-1) @ w_o[i]
        hn = rms_norm(h, g_ffn[i])
        h = h + hierarchical_moe(hn, w_group[i], b_group[i], w_expert[i], b_expert[i],
                                 w1[i], w3[i], w2[i])
        gate = jax.nn.sigmoid(rms_norm(h, g_ple[i]) @ w_ple_gate[i])
        h = h + gate * (p[i] @ w_ple_proj[i])
    return h
```

```python
import functools
import math

import jax
import jax.numpy as jnp
from jax import lax
from jax.experimental import pallas as pl
from jax.experimental.pallas import tpu as pltpu

F32 = jnp.float32
BF16 = jnp.bfloat16

N_DIFF_HEADS = 8
DIFF_QK_DIM = 64
DIFF_V_DIM = 128
CHUNK = 64
NUM_BUCKETS = 32
CONV_WIDTH = 3
CONV_GROUP_DIM = 128
N_GROUPS = 4
EXPERTS_PER_GROUP = 8
N_EXPERTS = N_GROUPS * EXPERTS_PER_GROUP
TOP_K = 2
EPS = 1e-6
MASK_VALUE = -1e30
LAM_INIT = 0.8 - 0.6 * math.exp(-0.3 * 0)

LANES = 128
SUBLANES = 8
VMEM_LIMIT_BYTES = 56 * 1024 * 1024

IN_PROJ_TM = 1024
IN_PROJ_TN = 512
ATTN_TQ = 256
CONV_TS = 512
OUT_PROJ_TM = 512
MOE_TM = 256
PLE_TM = 512
ROUTER_COLS = 128


def _cparams(semantics):
    return pltpu.CompilerParams(dimension_semantics=semantics,
                                vmem_limit_bytes=VMEM_LIMIT_BYTES)


def _in_proj_kernel(x_ref, g_ref, w_ref, gqk_ref, gsum_ref, o_ref, xn_ref, *, n_qk_tiles):
    j = pl.program_id(1)

    @pl.when(j == 0)
    def _():
        x = x_ref[...]
        ms = jnp.mean(x * x, axis=-1, keepdims=True)
        xn_ref[...] = (x * lax.rsqrt(ms + EPS) * g_ref[...]).astype(BF16)

    acc = jnp.dot(xn_ref[...], w_ref[...], preferred_element_type=F32)

    @pl.when(j < n_qk_tiles)
    def _():
        for c in range(acc.shape[1] // LANES):
            a = acc[:, c * LANES:(c + 1) * LANES]
            sq = a * a
            hi = sq.astype(BF16)
            lo = (sq - hi.astype(F32)).astype(BF16)
            ss = (jnp.dot(hi, gsum_ref[...], preferred_element_type=F32)
                  + jnp.dot(lo, gsum_ref[...], preferred_element_type=F32))
            y = a * lax.rsqrt(ss * (1.0 / DIFF_QK_DIM) + EPS)
            o_ref[:, c * LANES:(c + 1) * LANES] = (
                y * gqk_ref[:, c * LANES:(c + 1) * LANES]).astype(o_ref.dtype)

    @pl.when(j >= n_qk_tiles)
    def _():
        o_ref[...] = acc.astype(o_ref.dtype)


def _in_proj(x2, g_mix, w_in_bf, gqk, gsum):
    t, d = x2.shape
    n_out = w_in_bf.shape[1]
    tm, tn = min(IN_PROJ_TM, t), IN_PROJ_TN
    n_qk_tiles = gqk.shape[1] // tn
    return pl.pallas_call(
        functools.partial(_in_proj_kernel, n_qk_tiles=n_qk_tiles),
        grid=(t // tm, n_out // tn),
        in_specs=[
            pl.BlockSpec((tm, d), lambda i, j: (i, 0)),
            pl.BlockSpec((1, d), lambda i, j: (0, 0)),
            pl.BlockSpec((d, tn), lambda i, j: (0, j)),
            pl.BlockSpec((1, tn), lambda i, j: (0, jnp.minimum(j, n_qk_tiles - 1))),
            pl.BlockSpec((LANES, LANES), lambda i, j: (0, 0)),
        ],
        out_specs=pl.BlockSpec((tm, tn), lambda i, j: (i, j)),
        out_shape=jax.ShapeDtypeStruct((t, n_out), BF16),
        scratch_shapes=[pltpu.VMEM((tm, d), BF16)],
        compiler_params=_cparams(("parallel", "arbitrary")),
        name="in_proj",
    )(x2, g_mix, w_in_bf, gqk, gsum)


def _attn_kernel(lam_ref, q_ref, k_ref, v_ref, bias_ref, gsub_ref, o_ref,
                 m_ref, l_ref, acc_ref, *, tq):
    qi = pl.program_id(2)
    q = q_ref[0]
    lane = lax.broadcasted_iota(jnp.int32, q.shape, 1)
    zero = jnp.zeros_like(q)
    qs = jnp.concatenate([jnp.where(lane < DIFF_QK_DIM, q, zero),
                          jnp.where(lane < DIFF_QK_DIM, zero, q)], axis=0)

    def logits(kblk):
        return lax.dot_general(qs, kblk, (((1,), (1,)), ((), ())),
                               preferred_element_type=F32)

    def update(s, vblk):
        m_prev = m_ref[...]
        m_new = jnp.maximum(m_prev, jnp.max(s, axis=-1, keepdims=True))
        alpha = jnp.exp(m_prev - m_new)
        p = jnp.exp(s - m_new)
        l_ref[...] = alpha * l_ref[...] + jnp.sum(p, axis=-1, keepdims=True)
        acc_ref[...] = alpha * acc_ref[...] + jnp.dot(
            p.astype(BF16), vblk, preferred_element_type=F32)
        m_ref[...] = m_new

    start = pl.multiple_of(qi * tq, tq)
    s = logits(k_ref[0, pl.ds(start, tq), :]) + bias_ref[0, 0]
    m0 = jnp.max(s, axis=-1, keepdims=True)
    p = jnp.exp(s - m0)
    m_ref[...] = m0
    l_ref[...] = jnp.sum(p, axis=-1, keepdims=True)
    acc_ref[...] = jnp.dot(p.astype(BF16), v_ref[0, pl.ds(start, tq), :],
                           preferred_element_type=F32)

    @pl.when(qi > 0)
    def _():
        st = pl.multiple_of((qi - 1) * tq, tq)
        update(logits(k_ref[0, pl.ds(st, tq), :]) + bias_ref[0, 1],
               v_ref[0, pl.ds(st, tq), :])

    def far(j, carry):
        st = pl.multiple_of(j * tq, tq)
        update(logits(k_ref[0, pl.ds(st, tq), :]), v_ref[0, pl.ds(st, tq), :])
        return carry

    lax.fori_loop(0, jnp.maximum(qi - 1, 0), far, 0)

    acc = acc_ref[...]
    inv_l = 1.0 / l_ref[...]
    o = acc[:tq] * inv_l[:tq] - lam_ref[0] * (acc[tq:] * inv_l[tq:])
    ms = jnp.mean(o * o, axis=-1, keepdims=True)
    o_ref[0] = (o * lax.rsqrt(ms + EPS) * gsub_ref[...]).astype(o_ref.dtype)


def _attention(proj3, bias_tiles, gsub, lam):
    b, s, _ = proj3.shape
    h = N_DIFF_HEADS
    tq = min(ATTN_TQ, s)
    kern = functools.partial(_attn_kernel, tq=tq)
    return pl.pallas_call(
        kern,
        grid=(b, h, s // tq),
        in_specs=[
            pl.BlockSpec(memory_space=pltpu.SMEM),
            pl.BlockSpec((1, tq, LANES), lambda bi, hi, qi: (bi, qi, hi)),
            pl.BlockSpec((1, s, LANES), lambda bi, hi, qi: (bi, 0, h + hi)),
            pl.BlockSpec((1, s, LANES), lambda bi, hi, qi: (bi, 0, 2 * h + hi)),
            pl.BlockSpec((1, 2, 2 * tq, tq), lambda bi, hi, qi: (hi, 0, 0, 0)),
            pl.BlockSpec((1, LANES), lambda bi, hi, qi: (0, 0)),
        ],
        out_specs=pl.BlockSpec((1, tq, LANES), lambda bi, hi, qi: (bi, qi, hi)),
        out_shape=jax.ShapeDtypeStruct((b, s, h * DIFF_V_DIM), BF16),
        scratch_shapes=[pltpu.VMEM((2 * tq, 1), F32),
                        pltpu.VMEM((2 * tq, 1), F32),
                        pltpu.VMEM((2 * tq, DIFF_V_DIM), F32)],
        compiler_params=_cparams(("parallel", "parallel", "arbitrary")),
        name="diff_attention",
    )(lam, proj3, proj3, proj3, bias_tiles, gsub)


def _rel_bucket(rel):
    nb = NUM_BUCKETS // 2
    max_exact = nb // 2
    n = jnp.abs(rel)
    n2 = n * n
    large = max_exact + sum((n2 >= (max_exact * max_exact) * (2 ** k)).astype(jnp.int32)
                            for k in range(1, nb - max_exact))
    return jnp.where(rel > 0, nb, 0) + jnp.where(n < max_exact, n, large)


def _bias_tiles(rel_bias, tq):
    qpos = jnp.arange(tq, dtype=jnp.int32)[:, None]
    kpos = jnp.arange(tq, dtype=jnp.int32)[None, :]
    far_bias = rel_bias[NUM_BUCKETS // 2 - 1]
    b0 = rel_bias[_rel_bucket(kpos - qpos)] - far_bias
    b1 = rel_bias[_rel_bucket(kpos - tq - qpos)] - far_bias
    mask = (kpos // CHUNK) <= (qpos // CHUNK)
    b0 = jnp.where(mask[..., None], b0, MASK_VALUE)
    tiles = jnp.transpose(jnp.stack([b0, b1]), (3, 0, 1, 2)).astype(F32)
    return jnp.concatenate([tiles, tiles], axis=2)


def _conv_kernel(b_ref, c_ref, u_ref, cp_ref, up_ref, w_ref, cb_ref, g_ref, o_ref, buf_ref):
    si = pl.program_id(1)
    ts = o_ref.shape[1]
    cu = c_ref[0].astype(F32) * u_ref[0].astype(F32)
    prev = cp_ref[0].astype(F32) * up_ref[0].astype(F32)
    buf_ref[0:SUBLANES, :] = jnp.where(si > 0, prev, 0.0)
    buf_ref[SUBLANES:, :] = cu
    conv = (w_ref[0:1, :] * buf_ref[pl.ds(SUBLANES - 2, ts), :]
            + w_ref[1:2, :] * buf_ref[pl.ds(SUBLANES - 1, ts), :]
            + w_ref[2:3, :] * cu)
    z = b_ref[0].astype(F32) * (conv + cb_ref[...])
    for c in range(z.shape[1] // CONV_GROUP_DIM):
        sl = slice(c * CONV_GROUP_DIM, (c + 1) * CONV_GROUP_DIM)
        zc = z[:, sl]
        ms = jnp.mean(zc * zc, axis=-1, keepdims=True)
        o_ref[0, :, sl] = (zc * lax.rsqrt(ms + EPS) * g_ref[:, sl]).astype(o_ref.dtype)


def _short_conv(proj3, conv_w, conv_b, g_conv):
    b, s, n = proj3.shape
    dc = conv_w.shape[1]
    ts = min(CONV_TS, s)
    col0 = (n - 3 * dc) // dc
    halo = ts // SUBLANES

    def main(col):
        return pl.BlockSpec((1, ts, dc), lambda bi, si: (bi, si, col))

    def prev(col):
        return pl.BlockSpec((1, SUBLANES, dc),
                            lambda bi, si: (bi, jnp.maximum(si * halo - 1, 0), col))

    return pl.pallas_call(
        _conv_kernel,
        grid=(b, s // ts),
        in_specs=[main(col0), main(col0 + 1), main(col0 + 2), prev(col0 + 1), prev(col0 + 2),
                  pl.BlockSpec((CONV_WIDTH, dc), lambda bi, si: (0, 0)),
                  pl.BlockSpec((1, dc), lambda bi, si: (0, 0)),
                  pl.BlockSpec((1, dc), lambda bi, si: (0, 0))],
        out_specs=pl.BlockSpec((1, ts, dc), lambda bi, si: (bi, si, 0)),
        out_shape=jax.ShapeDtypeStruct((b, s, dc), BF16),
        scratch_shapes=[pltpu.VMEM((ts + SUBLANES, dc), F32)],
        compiler_params=_cparams(("parallel", "parallel")),
        name="short_conv",
    )(proj3, proj3, proj3, proj3, proj3, conv_w, conv_b, g_conv)


def _out_proj_kernel(x_ref, a_ref, c_ref, wa_ref, wc_ref, g_ref, wr_ref, br_ref, tri_ref,
                     h_ref, hn_ref, mi_ref, wcol_ref, cnt_ref, carry_ref):
    i = pl.program_id(0)
    tm = x_ref.shape[0]

    @pl.when(i == 0)
    def _():
        carry_ref[...] = jnp.zeros_like(carry_ref)

    h = (x_ref[...]
         + jnp.dot(a_ref[...], wa_ref[...], preferred_element_type=F32)
         + jnp.dot(c_ref[...], wc_ref[...], preferred_element_type=F32))
    h_ref[...] = h
    ms = jnp.mean(h * h, axis=-1, keepdims=True)
    hn = h * lax.rsqrt(ms + EPS) * g_ref[...]
    hn_hi = hn.astype(BF16)
    hn_ref[...] = hn_hi
    hn_lo = (hn - hn_hi.astype(F32)).astype(BF16)

    r_hi = jnp.dot(hn_hi, wr_ref[...], preferred_element_type=F32)
    r_lo = jnp.dot(hn_lo, wr_ref[...], preferred_element_type=F32)
    logits = (r_hi[:, :ROUTER_COLS] + r_hi[:, ROUTER_COLS:] + r_lo[:, :ROUTER_COLS]
              + br_ref[...])
    lt = logits.T

    e = EXPERTS_PER_GROUP
    row = lax.broadcasted_iota(jnp.int32, (e, tm), 0)

    def first_argmax(v):
        vmax = jnp.max(v, axis=0, keepdims=True)
        idx = jnp.min(jnp.where(v == vmax, row, e), axis=0, keepdims=True)
        return vmax, idx

    gl = lt[0:e]
    gmax, g = first_argmax(gl)
    p_g = 1.0 / jnp.sum(jnp.exp(gl - gmax), axis=0, keepdims=True)
    ing = lt[e:2 * e]
    for gi in range(1, N_GROUPS):
        ing = jnp.where(g == gi, lt[(gi + 1) * e:(gi + 2) * e], ing)
    v1, i1 = first_argmax(ing)
    v2, i2 = first_argmax(jnp.where(row == i1, -jnp.inf, ing))
    ex = jnp.exp(v2 - v1)
    w1 = p_g / (1.0 + ex)
    w2 = w1 * ex
    e1 = g * e + i1
    e2 = g * e + i2

    erow = lax.broadcasted_iota(jnp.int32, (N_EXPERTS, tm), 0)
    oh1 = (erow == e1).astype(F32)
    oh2 = (erow == e2).astype(F32)
    oh = jnp.concatenate([oh1, oh2], axis=0).astype(BF16)
    pre = jnp.dot(oh, tri_ref[...], preferred_element_type=F32)
    cnt1 = jnp.sum(oh1, axis=1, keepdims=True)
    cnt2 = jnp.sum(oh2, axis=1, keepdims=True)
    carry = carry_ref[:, 0:1]
    r1 = jnp.sum(oh1 * (pre[:N_EXPERTS] + carry), axis=0, keepdims=True)
    r2 = jnp.sum(oh2 * (pre[N_EXPERTS:] + carry + cnt1), axis=0, keepdims=True)
    new_carry = carry + cnt1 + cnt2
    carry_ref[...] = jnp.broadcast_to(new_carry, carry_ref.shape)
    cnt_ref[...] = jnp.broadcast_to(new_carry, cnt_ref.shape)

    mi_ref[0] = jnp.concatenate(
        [e1, e2, r1.astype(jnp.int32), r2.astype(jnp.int32),
         jnp.zeros((SUBLANES - 4, tm), jnp.int32)], axis=0)
    wrow = jnp.concatenate([w1, w2, jnp.zeros((ROUTER_COLS - 2, tm), F32)], axis=0)
    wcol_ref[...] = wrow.T


def _out_proj(x2, attn2, conv2, wo_a, wo_c, g_ffn, wr, br, tri):
    t, d = x2.shape
    tm = min(OUT_PROJ_TM, t)
    nt = t // tm
    da, dc = attn2.shape[1], conv2.shape[1]
    const = lambda i: (0, 0)
    return pl.pallas_call(
        _out_proj_kernel,
        grid=(nt,),
        in_specs=[
            pl.BlockSpec((tm, d), lambda i: (i, 0)),
            pl.BlockSpec((tm, da), lambda i: (i, 0)),
            pl.BlockSpec((tm, dc), lambda i: (i, 0)),
            pl.BlockSpec((da, d), const),
            pl.BlockSpec((dc, d), const),
            pl.BlockSpec((1, d), const),
            pl.BlockSpec((d, 2 * ROUTER_COLS), const),
            pl.BlockSpec((1, ROUTER_COLS), const),
            pl.BlockSpec((tm, tm), const),
        ],
        out_specs=[
            pl.BlockSpec((tm, d), lambda i: (i, 0)),
            pl.BlockSpec((tm, d), lambda i: (i, 0)),
            pl.BlockSpec((1, SUBLANES, tm), lambda i: (i, 0, 0)),
            pl.BlockSpec((tm, ROUTER_COLS), lambda i: (i, 0)),
            pl.BlockSpec((N_EXPERTS, LANES), const),
        ],
        out_shape=[
            jax.ShapeDtypeStruct((t, d), F32),
            jax.ShapeDtypeStruct((t, d), BF16),
            jax.ShapeDtypeStruct((nt, SUBLANES, tm), jnp.int32),
            jax.ShapeDtypeStruct((t, ROUTER_COLS), F32),
            jax.ShapeDtypeStruct((N_EXPERTS, LANES), F32),
        ],
        scratch_shapes=[pltpu.VMEM((N_EXPERTS, LANES), F32)],
        compiler_params=_cparams(("arbitrary",)),
        name="out_proj_router",
    )(x2, attn2, conv2, wo_a, wo_c, g_ffn, wr, br, tri)


def _moe_kernel(te_ref, nv_ref, x_ref, w13_ref, w2_ref, o_ref):
    i = pl.program_id(0)

    @pl.when(i < nv_ref[0])
    def _():
        dff = w2_ref.shape[1]
        ab = jnp.dot(x_ref[...], w13_ref[0], preferred_element_type=F32)
        a, b = ab[:, :dff], ab[:, dff:]
        hid = (a * jax.nn.sigmoid(a) * b).astype(BF16)
        o_ref[...] = jnp.dot(hid, w2_ref[0], preferred_element_type=F32).astype(o_ref.dtype)

    @pl.when(i >= nv_ref[0])
    def _():
        o_ref[...] = jnp.zeros_like(o_ref)


def _moe(xs, w13, w2, tile_expert, n_valid):
    n, d = xs.shape
    tm = MOE_TM
    nt = n // tm
    dff = w2.shape[1]
    grid_spec = pltpu.PrefetchScalarGridSpec(
        num_scalar_prefetch=2,
        grid=(nt,),
        in_specs=[
            pl.BlockSpec((tm, d), lambda i, te, nv: (jnp.minimum(i, nv[0] - 1), 0)),
            pl.BlockSpec((1, d, 2 * dff), lambda i, te, nv: (te[i], 0, 0)),
            pl.BlockSpec((1, dff, d), lambda i, te, nv: (te[i], 0, 0)),
        ],
        out_specs=pl.BlockSpec((tm, d), lambda i, te, nv: (i, 0)),
    )
    return pl.pallas_call(
        _moe_kernel,
        grid_spec=grid_spec,
        out_shape=jax.ShapeDtypeStruct((n, d), BF16),
        compiler_params=_cparams(("arbitrary",)),
        name="moe_grouped",
    )(tile_expert, n_valid, xs, w13, w2)


def _ple_kernel(h_ref, y1_ref, y2_ref, wcol_ref, p_ref, g_ref, wg_ref, wp_ref, o_ref):
    wcol = wcol_ref[...]
    h = (h_ref[...]
         + wcol[:, 0:1] * y1_ref[...].astype(F32)
         + wcol[:, 1:2] * y2_ref[...].astype(F32))
    ms = jnp.mean(h * h, axis=-1, keepdims=True)
    hn = (h * lax.rsqrt(ms + EPS) * g_ref[...]).astype(BF16)
    gate = jax.nn.sigmoid(jnp.dot(hn, wg_ref[...], preferred_element_type=F32))
    emb = jnp.dot(p_ref[...].astype(BF16), wp_ref[...], preferred_element_type=F32)
    o_ref[...] = h + gate * emb


def _ple(h1, y1, y2, wcol, p2, g_ple, wg, wp):
    t, d = h1.shape
    tm = min(PLE_TM, t)
    dp = p2.shape[1]
    const = lambda i: (0, 0)
    row = lambda i: (i, 0)
    return pl.pallas_call(
        _ple_kernel,
        grid=(t // tm,),
        in_specs=[
            pl.BlockSpec((tm, d), row),
            pl.BlockSpec((tm, d), row),
            pl.BlockSpec((tm, d), row),
            pl.BlockSpec((tm, ROUTER_COLS), row),
            pl.BlockSpec((tm, dp), row),
            pl.BlockSpec((1, d), const),
            pl.BlockSpec((d, d), const),
            pl.BlockSpec((dp, d), const),
        ],
        out_specs=pl.BlockSpec((tm, d), row),
        out_shape=jax.ShapeDtypeStruct((t, d), F32),
        compiler_params=_cparams(("parallel",)),
        name="combine_ple",
    )(h1, y1, y2, wcol, p2, g_ple, wg, wp)


def kernel(x, p, rel_bias, g_mix, w_in, g_q, g_k, lam_q1, lam_k1, lam_q2, lam_k2, g_subln,
           conv_w, conv_b, g_conv, w_o, g_ffn, w_group, b_group, w_expert, b_expert,
           w1, w3, w2, g_ple, w_ple_gate, w_ple_proj):
    depth = g_mix.shape[0]
    assert depth == 1
    li = 0
    b, s, d = x.shape
    t = b * s
    d_attn = N_DIFF_HEADS * DIFF_V_DIM
    x2 = x.reshape(t, d)

    n_groups_qk = d_attn // DIFF_QK_DIM
    gqk = jnp.concatenate([jnp.tile(g_q[li] * (DIFF_QK_DIM ** -0.5), n_groups_qk),
                           jnp.tile(g_k[li], n_groups_qk)])[None, :].astype(F32)
    blk = jnp.arange(LANES) // DIFF_QK_DIM
    gsum = (blk[:, None] == blk[None, :]).astype(BF16)
    lam = (jnp.exp(jnp.sum(lam_q1[li] * lam_k1[li])) - jnp.exp(jnp.sum(lam_q2[li] * lam_k2[li]))
           + LAM_INIT).reshape(1).astype(F32)
    gsub = (g_subln[li] * (1.0 - LAM_INIT))[None, :].astype(F32)
    tq = min(ATTN_TQ, s)
    bias_tiles = _bias_tiles(rel_bias, tq)

    wr_f32 = jnp.zeros((d, ROUTER_COLS), F32)
    wr_f32 = wr_f32.at[:, :N_GROUPS].set(w_group[li])
    wr_f32 = wr_f32.at[:, EXPERTS_PER_GROUP:EXPERTS_PER_GROUP + N_EXPERTS].set(w_expert[li])
    wr_hi = wr_f32.astype(BF16)
    wr_lo = (wr_f32 - wr_hi.astype(F32)).astype(BF16)
    wr = jnp.concatenate([wr_hi, wr_lo], axis=1)
    br = jnp.zeros((ROUTER_COLS,), F32)
    br = br.at[:EXPERTS_PER_GROUP].set(MASK_VALUE)
    br = br.at[:N_GROUPS].set(b_group[li])
    br = br.at[EXPERTS_PER_GROUP:EXPERTS_PER_GROUP + N_EXPERTS].set(b_expert[li])[None, :]
    tm_r = min(OUT_PROJ_TM, t)
    ar = jnp.arange(tm_r)
    tri = (ar[:, None] < ar[None, :]).astype(BF16)

    w_in_bf = w_in[li].astype(BF16)
    wo_bf = w_o[li].astype(BF16)
    w13 = jnp.concatenate([w1[li], w3[li]], axis=-1).astype(BF16)
    w2_bf = w2[li].astype(BF16)
    wg_bf = w_ple_gate[li].astype(BF16)
    wp_bf = w_ple_proj[li].astype(BF16)

    proj = _in_proj(x2, g_mix[li][None, :], w_in_bf, gqk, gsum)
    proj3 = proj.reshape(b, s, -1)
    attn = _attention(proj3, bias_tiles, gsub, lam)
    conv = _short_conv(proj3, conv_w[li], conv_b[li][None, :], g_conv[li][None, :])
    h1, hn, meta_i, wcol, counts = _out_proj(
        x2, attn.reshape(t, -1), conv.reshape(t, -1), wo_bf[:d_attn], wo_bf[d_attn:],
        g_ffn[li][None, :], wr, br, tri)

    nt_r = t // tm_r
    eid = jnp.transpose(meta_i[:, 0:2, :], (1, 0, 2)).reshape(2, t)
    rank = jnp.transpose(meta_i[:, 2:4, :], (1, 0, 2)).reshape(2, t)
    cnt = counts[:, 0].astype(jnp.int32)
    tiles_per = (cnt + MOE_TM - 1) // MOE_TM
    tile_end = jnp.cumsum(tiles_per)
    tile_start = tile_end - tiles_per
    n_tiles = (TOP_K * t) // MOE_TM + N_EXPERTS
    pos = (tile_start * MOE_TM)[eid] + rank
    tile_expert = jnp.minimum(
        jnp.searchsorted(tile_end, jnp.arange(n_tiles, dtype=jnp.int32), side="right"),
        N_EXPERTS - 1).astype(jnp.int32)
    n_valid = tile_end[-1:].astype(jnp.int32)

    xs = jnp.zeros((n_tiles * MOE_TM, d), BF16)
    xs = xs.at[pos[0]].set(hn).at[pos[1]].set(hn)
    ys = _moe(xs, w13, w2_bf, tile_expert, n_valid)
    y1 = jnp.take(ys, pos[0], axis=0)
    y2 = jnp.take(ys, pos[1], axis=0)

    out = _ple(h1, y1, y2, wcol, p[li].reshape(t, -1), g_ple[li][None, :], wg_bf, wp_bf)
    return out.reshape(b, s, d)
```

```python
import functools
import math

import jax
import jax.numpy as jnp
from jax import lax
from jax.experimental import pallas as pl
from jax.experimental.pallas import tpu as pltpu
from jax.experimental.pallas import tpu_sc as plsc

F32 = jnp.float32
BF16 = jnp.bfloat16

N_DIFF_HEADS = 8
DIFF_QK_DIM = 64
DIFF_V_DIM = 128
CHUNK = 64
NUM_BUCKETS = 32
MAX_DISTANCE = 128
CONV_WIDTH = 3
CONV_GROUP_DIM = 128
N_GROUPS = 4
EXPERTS_PER_GROUP = 8
N_EXPERTS = N_GROUPS * EXPERTS_PER_GROUP
TOP_K = 2
EPS = 1e-6
MASK_VALUE = -1e30
LAM_INIT = 0.8 - 0.6 * math.exp(-0.3 * 0)
LOG2E = math.log2(math.e)

LANES = 128
SUBLANES = 8
VMEM_LIMIT_BYTES = 56 * 1024 * 1024

IN_PROJ_TM = 1024
IN_PROJ_TN = 512
ATTN_TQ = 512
ATTN_TK = 256
CONV_TS = 512
OUT_PROJ_TM = 512
MOE_TM = 256
PLE_TM = 512
ROUTER_COLS = 128
SC_WINDOW = 128
SC_NUM_CORES = 2
SC_NUM_SUBCORES = 16


def _cparams(semantics):
    return pltpu.CompilerParams(dimension_semantics=semantics,
                                vmem_limit_bytes=VMEM_LIMIT_BYTES)


def _store_routed_rows(o_ref, x):
    tm, d = x.shape
    half = d // 2
    slabs = half // LANES
    xr = x.astype(BF16).astype(F32)
    lo = lax.bitcast_convert_type(xr[:, :half], jnp.uint32)
    hi = lax.bitcast_convert_type(xr[:, half:], jnp.uint32)
    packed = (lo >> 16) | (hi & jnp.uint32(0xFFFF0000))
    for r in range(slabs):
        o_ref[pl.ds(r, tm, stride=slabs), :] = packed[:, r * LANES:(r + 1) * LANES]


def _load_routed_rows(x_ref, tm, dtype):
    slabs = x_ref.shape[0] // tm
    parts = [x_ref[pl.ds(r, tm, stride=slabs), :] for r in range(slabs)]
    lo = [lax.bitcast_convert_type(w << 16, F32).astype(dtype) for w in parts]
    hi = [lax.bitcast_convert_type(w & jnp.uint32(0xFFFF0000), F32).astype(dtype) for w in parts]
    return jnp.concatenate(lo + hi, axis=1)


def _sc_mesh():
    return plsc.VectorSubcoreMesh(core_axis_name="core", subcore_axis_name="subcore",
                                  num_cores=SC_NUM_CORES, num_subcores=SC_NUM_SUBCORES)


def _sc_scatter_rows(rows, idx, n_out_rows):
    n_src, n_idx = rows.shape[0], idx.shape[1]
    src_windows = n_src // SC_WINDOW

    @functools.partial(pl.kernel, mesh=_sc_mesh(), scratch_types=[],
                       out_type=jax.ShapeDtypeStruct((n_out_rows, LANES), rows.dtype))
    def scatter(x_hbm, i_hbm, o_hbm):
        def body(x_vmem, i_vmem):
            pltpu.sync_copy(x_vmem, o_hbm.at[i_vmem.at[0]])

        pltpu.emit_pipeline(
            body,
            grid=(n_idx // SC_WINDOW,),
            in_specs=[pl.BlockSpec((SC_WINDOW, LANES), lambda i: (lax.rem(i, src_windows), 0)),
                      pl.BlockSpec((1, SC_WINDOW), lambda i: (0, i))],
            out_specs=[],
            core_axis_name=("core", "subcore"),
            dimension_semantics=(pltpu.PARALLEL,),
        )(x_hbm, i_hbm)

    return scatter(rows, idx)


def _sc_gather_rows(table, idx):
    n_idx = idx.shape[1]

    @functools.partial(pl.kernel, mesh=_sc_mesh(), scratch_types=[],
                       out_type=jax.ShapeDtypeStruct((n_idx, LANES), table.dtype))
    def gather(t_hbm, i_hbm, o_hbm):
        def body(i_vmem, o_vmem):
            pltpu.sync_copy(t_hbm.at[i_vmem.at[0]], o_vmem)

        pltpu.emit_pipeline(
            body,
            grid=(n_idx // SC_WINDOW,),
            in_specs=[pl.BlockSpec((1, SC_WINDOW), lambda i: (0, i))],
            out_specs=[pl.BlockSpec((SC_WINDOW, LANES), lambda i: (i, 0))],
            core_axis_name=("core", "subcore"),
            dimension_semantics=(pltpu.PARALLEL,),
        )(i_hbm, o_hbm)

    return gather(table, idx)


def _in_proj_kernel(x_ref, g_ref, w_ref, gqk_ref, gsum_ref, o_ref, xn_ref, *, n_qk_tiles):
    j = pl.program_id(1)

    @pl.when(j == 0)
    def _():
        x = x_ref[...]
        ms = jnp.mean(x * x, axis=-1, keepdims=True)
        xn_ref[...] = (x * lax.rsqrt(ms + EPS) * g_ref[...]).astype(BF16)

    acc = jnp.dot(xn_ref[...], w_ref[...], preferred_element_type=F32)

    @pl.when(j < n_qk_tiles)
    def _():
        for c in range(acc.shape[1] // LANES):
            a = acc[:, c * LANES:(c + 1) * LANES]
            sq = a * a
            hi = sq.astype(BF16)
            lo = (sq - hi.astype(F32)).astype(BF16)
            ss = (jnp.dot(hi, gsum_ref[...], preferred_element_type=F32)
                  + jnp.dot(lo, gsum_ref[...], preferred_element_type=F32))
            y = a * lax.rsqrt(ss * (1.0 / DIFF_QK_DIM) + EPS)
            o_ref[:, c * LANES:(c + 1) * LANES] = (
                y * gqk_ref[:, c * LANES:(c + 1) * LANES]).astype(o_ref.dtype)

    @pl.when(j >= n_qk_tiles)
    def _():
        o_ref[...] = acc.astype(o_ref.dtype)


def _in_proj(x2, g_mix, w_in_bf, gqk, gsum):
    t, d = x2.shape
    n_out = w_in_bf.shape[1]
    tm, tn = min(IN_PROJ_TM, t), IN_PROJ_TN
    n_qk_tiles = gqk.shape[1] // tn
    return pl.pallas_call(
        functools.partial(_in_proj_kernel, n_qk_tiles=n_qk_tiles),
        grid=(t // tm, n_out // tn),
        in_specs=[
            pl.BlockSpec((tm, d), lambda i, j: (i, 0)),
            pl.BlockSpec((1, d), lambda i, j: (0, 0)),
            pl.BlockSpec((d, tn), lambda i, j: (0, j)),
            pl.BlockSpec((1, tn), lambda i, j: (0, jnp.minimum(j, n_qk_tiles - 1))),
            pl.BlockSpec((LANES, LANES), lambda i, j: (0, 0)),
        ],
        out_specs=pl.BlockSpec((tm, tn), lambda i, j: (i, j)),
        out_shape=jax.ShapeDtypeStruct((t, n_out), BF16),
        scratch_shapes=[pltpu.VMEM((tm, d), BF16)],
        compiler_params=_cparams(("parallel", "arbitrary")),
        name="in_proj",
    )(x2, g_mix, w_in_bf, gqk, gsum)


def _attn_kernel(lam_ref, q_ref, k_ref, v_ref, bias_ref, gsub_ref, o_ref,
                 qs_ref, m_ref, acc_ref, s0_ref, s1_ref, p0_ref, p1_ref, a0_ref, a1_ref, *, tq, tk):
    qi = pl.program_id(2)
    dv = DIFF_V_DIM
    s_bufs, p_bufs, a_bufs = (s0_ref, s1_ref), (p0_ref, p1_ref), (a0_ref, a1_ref)

    q = q_ref[0]
    lane = lax.broadcasted_iota(jnp.int32, q.shape, 1)
    zero = jnp.zeros_like(q)
    qs_ref[0:tq, :] = jnp.where(lane < DIFF_QK_DIM, q, zero)
    qs_ref[tq:, :] = jnp.where(lane < DIFF_QK_DIM, zero, q)
    acc_ref[...] = jnp.zeros_like(acc_ref)
    m_ref[...] = jnp.full_like(m_ref, MASK_VALUE)
    p1_ref[...] = jnp.zeros_like(p1_ref)
    a1_ref[...] = jnp.zeros_like(a1_ref)
    ones = jnp.ones((tk, dv), BF16)

    def block_start(step):
        return pl.multiple_of(jnp.maximum(step, 0) * tk, tk)

    def logits_stage(step, dst):
        dst[...] = lax.dot_general(qs_ref[...], k_ref[0, pl.ds(block_start(step), tk), :],
                                   (((1,), (1,)), ((), ())), preferred_element_type=F32)

    def pv_stage(step, buf):
        vx = jnp.concatenate([v_ref[0, pl.ds(block_start(step), tk), :], ones], axis=1)
        alpha = a_bufs[buf][...]
        acc_ref[...] = (jnp.concatenate([alpha, alpha], axis=1) * acc_ref[...]
                        + jnp.dot(p_bufs[buf][...], vx, preferred_element_type=F32))

    def softmax_stage(buf, bias_tile):
        s = s_bufs[buf][...]
        if bias_tile is not None:
            bias = bias_ref[0, bias_tile]
            s = jnp.concatenate([s[:tq] + bias, s[tq:] + bias], axis=0)
        m_prev = m_ref[...]
        m_new = jnp.maximum(m_prev, jnp.max(s, axis=-1, keepdims=True))
        a_bufs[buf][...] = jnp.exp2(m_prev - m_new)
        m_ref[...] = m_new
        p = jnp.exp2(s - jnp.concatenate([m_new] * (tk // LANES), axis=1))
        p_bufs[buf][...] = p.astype(BF16)

    def pair(step0, bias0, bias1, lookahead=True):
        logits_stage(step0 + 1, s_bufs[1])
        pv_stage(step0 - 1, 1)
        softmax_stage(0, bias0)
        if lookahead:
            logits_stage(step0 + 2, s_bufs[0])
        pv_stage(step0, 0)
        softmax_stage(1, bias1)

    logits_stage(0, s_bufs[0])

    def far_pair(jj, carry):
        pair(2 * jj, None, None)
        return carry

    lax.fori_loop(0, jnp.maximum(qi - 1, 0), far_pair, 0)

    @pl.when(qi > 0)
    def _():
        pair(2 * qi - 2, None, 2)

    pair(2 * qi, 0, 1, lookahead=False)
    pv_stage(2 * qi + 1, 1)

    acc = acc_ref[...]
    o = (acc[:tq, :dv] / acc[:tq, dv:]) - lam_ref[0] * (acc[tq:, :dv] / acc[tq:, dv:])
    ms = jnp.mean(o * o, axis=-1, keepdims=True)
    o_ref[0] = (o * lax.rsqrt(ms + EPS) * gsub_ref[...]).astype(o_ref.dtype)


def _attention(proj3, bias_tiles, gsub, lam):
    b, s, _ = proj3.shape
    h = N_DIFF_HEADS
    tq, tk = ATTN_TQ, ATTN_TK
    assert tq == 2 * tk and s % tq == 0
    kern = functools.partial(_attn_kernel, tq=tq, tk=tk)
    return pl.pallas_call(
        kern,
        grid=(b, h, s // tq),
        in_specs=[
            pl.BlockSpec(memory_space=pltpu.SMEM),
            pl.BlockSpec((1, tq, LANES), lambda bi, hi, qi: (bi, qi, hi)),
            pl.BlockSpec((1, s, LANES), lambda bi, hi, qi: (bi, 0, h + hi)),
            pl.BlockSpec((1, s, LANES), lambda bi, hi, qi: (bi, 0, 2 * h + hi)),
            pl.BlockSpec((1, 3, tq, tk), lambda bi, hi, qi: (hi, 0, 0, 0)),
            pl.BlockSpec((1, LANES), lambda bi, hi, qi: (0, 0)),
        ],
        out_specs=pl.BlockSpec((1, tq, LANES), lambda bi, hi, qi: (bi, qi, hi)),
        out_shape=jax.ShapeDtypeStruct((b, s, h * DIFF_V_DIM), BF16),
        scratch_shapes=[pltpu.VMEM((2 * tq, LANES), BF16),
                        pltpu.VMEM((2 * tq, LANES), F32),
                        pltpu.VMEM((2 * tq, 2 * DIFF_V_DIM), F32),
                        pltpu.VMEM((2 * tq, tk), F32),
                        pltpu.VMEM((2 * tq, tk), F32),
                        pltpu.VMEM((2 * tq, tk), BF16),
                        pltpu.VMEM((2 * tq, tk), BF16),
                        pltpu.VMEM((2 * tq, LANES), F32),
                        pltpu.VMEM((2 * tq, LANES), F32)],
        compiler_params=_cparams(("parallel", "parallel", "arbitrary")),
        name="diff_attention",
    )(lam, proj3, proj3, proj3, bias_tiles, gsub)


def _rel_bucket(rel):
    nb = NUM_BUCKETS // 2
    max_exact = nb // 2
    n = jnp.abs(rel)
    n2 = n * n
    large = max_exact + sum((n2 >= (max_exact * max_exact) * (2 ** k)).astype(jnp.int32)
                            for k in range(1, nb - max_exact))
    return jnp.where(rel > 0, nb, 0) + jnp.where(n < max_exact, n, large)


def _bias_tiles(rel_bias, tq, tk):
    assert tk >= MAX_DISTANCE
    qpos = jnp.arange(tq, dtype=jnp.int32)[:, None]
    far_bias = rel_bias[NUM_BUCKETS // 2 - 1]
    tiles = []
    for offset in (0, tk, -tk):
        kpos = offset + jnp.arange(tk, dtype=jnp.int32)[None, :]
        bias = (rel_bias[_rel_bucket(kpos - qpos)] - far_bias) * LOG2E
        mask = (kpos // CHUNK) <= (qpos // CHUNK)
        tiles.append(jnp.where(mask[..., None], bias, MASK_VALUE))
    return jnp.transpose(jnp.stack(tiles), (3, 0, 1, 2)).astype(F32)


def _conv_kernel(b_ref, c_ref, u_ref, cp_ref, up_ref, w_ref, cb_ref, g_ref, o_ref, buf_ref):
    si = pl.program_id(1)
    ts = o_ref.shape[1]
    cu = c_ref[0].astype(F32) * u_ref[0].astype(F32)
    prev = cp_ref[0].astype(F32) * up_ref[0].astype(F32)
    buf_ref[0:SUBLANES, :] = jnp.where(si > 0, prev, 0.0)
    buf_ref[SUBLANES:, :] = cu
    conv = (w_ref[0:1, :] * buf_ref[pl.ds(SUBLANES - 2, ts), :]
            + w_ref[1:2, :] * buf_ref[pl.ds(SUBLANES - 1, ts), :]
            + w_ref[2:3, :] * cu)
    z = b_ref[0].astype(F32) * (conv + cb_ref[...])
    for c in range(z.shape[1] // CONV_GROUP_DIM):
        sl = slice(c * CONV_GROUP_DIM, (c + 1) * CONV_GROUP_DIM)
        zc = z[:, sl]
        ms = jnp.mean(zc * zc, axis=-1, keepdims=True)
        o_ref[0, :, sl] = (zc * lax.rsqrt(ms + EPS) * g_ref[:, sl]).astype(o_ref.dtype)


def _short_conv(proj3, conv_w, conv_b, g_conv):
    b, s, n = proj3.shape
    dc = conv_w.shape[1]
    ts = min(CONV_TS, s)
    col0 = (n - 3 * dc) // dc
    halo = ts // SUBLANES

    def main(col):
        return pl.BlockSpec((1, ts, dc), lambda bi, si: (bi, si, col))

    def prev(col):
        return pl.BlockSpec((1, SUBLANES, dc),
                            lambda bi, si: (bi, jnp.maximum(si * halo - 1, 0), col))

    return pl.pallas_call(
        _conv_kernel,
        grid=(b, s // ts),
        in_specs=[main(col0), main(col0 + 1), main(col0 + 2), prev(col0 + 1), prev(col0 + 2),
                  pl.BlockSpec((CONV_WIDTH, dc), lambda bi, si: (0, 0)),
                  pl.BlockSpec((1, dc), lambda bi, si: (0, 0)),
                  pl.BlockSpec((1, dc), lambda bi, si: (0, 0))],
        out_specs=pl.BlockSpec((1, ts, dc), lambda bi, si: (bi, si, 0)),
        out_shape=jax.ShapeDtypeStruct((b, s, dc), BF16),
        scratch_shapes=[pltpu.VMEM((ts + SUBLANES, dc), F32)],
        compiler_params=_cparams(("parallel", "parallel")),
        name="short_conv",
    )(proj3, proj3, proj3, proj3, proj3, conv_w, conv_b, g_conv)


def _out_proj_kernel(x_ref, a_ref, c_ref, wa_ref, wc_ref, g_ref, wr_ref, br_ref, tri_ref,
                     h_ref, hn_ref, mi_ref, wcol_ref, cnt_ref, carry_ref):
    i = pl.program_id(0)
    tm = x_ref.shape[0]

    @pl.when(i == 0)
    def _():
        carry_ref[...] = jnp.zeros_like(carry_ref)

    h = (x_ref[...]
         + jnp.dot(a_ref[...], wa_ref[...], preferred_element_type=F32)
         + jnp.dot(c_ref[...], wc_ref[...], preferred_element_type=F32))
    h_ref[...] = h
    ms = jnp.mean(h * h, axis=-1, keepdims=True)
    hn = h * lax.rsqrt(ms + EPS) * g_ref[...]
    hn_hi = hn.astype(BF16)
    _store_routed_rows(hn_ref, hn)
    hn_lo = (hn - hn_hi.astype(F32)).astype(BF16)

    r_hi = jnp.dot(hn_hi, wr_ref[...], preferred_element_type=F32)
    r_lo = jnp.dot(hn_lo, wr_ref[...], preferred_element_type=F32)
    logits = (r_hi[:, :ROUTER_COLS] + r_hi[:, ROUTER_COLS:] + r_lo[:, :ROUTER_COLS]
              + br_ref[...])
    lt = logits.T

    e = EXPERTS_PER_GROUP
    row = lax.broadcasted_iota(jnp.int32, (e, tm), 0)

    def first_argmax(v):
        vmax = jnp.max(v, axis=0, keepdims=True)
        idx = jnp.min(jnp.where(v == vmax, row, e), axis=0, keepdims=True)
        return vmax, idx

    gl = lt[0:e]
    gmax, g = first_argmax(gl)
    p_g = 1.0 / jnp.sum(jnp.exp(gl - gmax), axis=0, keepdims=True)
    ing = lt[e:2 * e]
    for gi in range(1, N_GROUPS):
        ing = jnp.where(g == gi, lt[(gi + 1) * e:(gi + 2) * e], ing)
    v1, i1 = first_argmax(ing)
    v2, i2 = first_argmax(jnp.where(row == i1, -jnp.inf, ing))
    ex = jnp.exp(v2 - v1)
    w1 = p_g / (1.0 + ex)
    w2 = w1 * ex
    e1 = g * e + i1
    e2 = g * e + i2

    erow = lax.broadcasted_iota(jnp.int32, (N_EXPERTS, tm), 0)
    oh1 = (erow == e1).astype(F32)
    oh2 = (erow == e2).astype(F32)
    oh = jnp.concatenate([oh1, oh2], axis=0).astype(BF16)
    pre = jnp.dot(oh, tri_ref[...], preferred_element_type=F32)
    cnt1 = jnp.sum(oh1, axis=1, keepdims=True)
    cnt2 = jnp.sum(oh2, axis=1, keepdims=True)
    carry = carry_ref[:, 0:1]
    r1 = jnp.sum(oh1 * (pre[:N_EXPERTS] + carry), axis=0, keepdims=True)
    r2 = jnp.sum(oh2 * (pre[N_EXPERTS:] + carry + cnt1), axis=0, keepdims=True)
    new_carry = carry + cnt1 + cnt2
    carry_ref[...] = jnp.broadcast_to(new_carry, carry_ref.shape)
    cnt_ref[...] = jnp.broadcast_to(new_carry, cnt_ref.shape)

    mi_ref[0] = jnp.concatenate(
        [e1, e2, r1.astype(jnp.int32), r2.astype(jnp.int32),
         jnp.zeros((SUBLANES - 4, tm), jnp.int32)], axis=0)
    wrow = jnp.concatenate([w1, w2, jnp.zeros((ROUTER_COLS - 2, tm), F32)], axis=0)
    wcol_ref[...] = wrow.T


def _out_proj(x2, attn2, conv2, wo_a, wo_c, g_ffn, wr, br, tri):
    t, d = x2.shape
    tm = min(OUT_PROJ_TM, t)
    nt = t // tm
    da, dc = attn2.shape[1], conv2.shape[1]
    slabs = d // (2 * LANES)
    const = lambda i: (0, 0)
    return pl.pallas_call(
        _out_proj_kernel,
        grid=(nt,),
        in_specs=[
            pl.BlockSpec((tm, d), lambda i: (i, 0)),
            pl.BlockSpec((tm, da), lambda i: (i, 0)),
            pl.BlockSpec((tm, dc), lambda i: (i, 0)),
            pl.BlockSpec((da, d), const),
            pl.BlockSpec((dc, d), const),
            pl.BlockSpec((1, d), const),
            pl.BlockSpec((d, 2 * ROUTER_COLS), const),
            pl.BlockSpec((1, ROUTER_COLS), const),
            pl.BlockSpec((tm, tm), const),
        ],
        out_specs=[
            pl.BlockSpec((tm, d), lambda i: (i, 0)),
            pl.BlockSpec((tm * slabs, LANES), lambda i: (i, 0)),
            pl.BlockSpec((1, SUBLANES, tm), lambda i: (i, 0, 0)),
            pl.BlockSpec((tm, ROUTER_COLS), lambda i: (i, 0)),
            pl.BlockSpec((N_EXPERTS, LANES), const),
        ],
        out_shape=[
            jax.ShapeDtypeStruct((t, d), F32),
            jax.ShapeDtypeStruct((t * slabs, LANES), jnp.uint32),
            jax.ShapeDtypeStruct((nt, SUBLANES, tm), jnp.int32),
            jax.ShapeDtypeStruct((t, ROUTER_COLS), F32),
            jax.ShapeDtypeStruct((N_EXPERTS, LANES), F32),
        ],
        scratch_shapes=[pltpu.VMEM((N_EXPERTS, LANES), F32)],
        compiler_params=_cparams(("arbitrary",)),
        name="out_proj_router",
    )(x2, attn2, conv2, wo_a, wo_c, g_ffn, wr, br, tri)


def _moe_kernel(te_ref, nv_ref, x_ref, w13_ref, w2_ref, o_ref):
    i = pl.program_id(0)

    @pl.when(i < nv_ref[0])
    def _():
        dff = w2_ref.shape[1]
        x = _load_routed_rows(x_ref, MOE_TM, BF16)
        ab = jnp.dot(x, w13_ref[0], preferred_element_type=F32)
        a, b = ab[:, :dff], ab[:, dff:]
        hid = (a * jax.nn.sigmoid(a) * b).astype(BF16)
        _store_routed_rows(o_ref, jnp.dot(hid, w2_ref[0], preferred_element_type=F32))

    @pl.when(i >= nv_ref[0])
    def _():
        o_ref[...] = jnp.zeros_like(o_ref)


def _moe(xs, w13, w2, tile_expert, n_valid):
    d, dff = w2.shape[2], w2.shape[1]
    slabs = d // (2 * LANES)
    rows = MOE_TM * slabs
    nt = xs.shape[0] // rows
    grid_spec = pltpu.PrefetchScalarGridSpec(
        num_scalar_prefetch=2,
        grid=(nt,),
        in_specs=[
            pl.BlockSpec((rows, LANES), lambda i, te, nv: (jnp.minimum(i, nv[0] - 1), 0)),
            pl.BlockSpec((1, d, 2 * dff), lambda i, te, nv: (te[i], 0, 0)),
            pl.BlockSpec((1, dff, d), lambda i, te, nv: (te[i], 0, 0)),
        ],
        out_specs=pl.BlockSpec((rows, LANES), lambda i, te, nv: (i, 0)),
    )
    return pl.pallas_call(
        _moe_kernel,
        grid_spec=grid_spec,
        out_shape=jax.ShapeDtypeStruct(xs.shape, jnp.uint32),
        compiler_params=_cparams(("arbitrary",)),
        name="moe_grouped",
    )(tile_expert, n_valid, xs, w13, w2)


def _ple_kernel(h_ref, y1_ref, y2_ref, wcol_ref, p_ref, g_ref, wg_ref, wp_ref, o_ref):
    wcol = wcol_ref[...]
    tm = h_ref.shape[0]
    h = (h_ref[...]
         + wcol[:, 0:1] * _load_routed_rows(y1_ref.at[0], tm, F32)
         + wcol[:, 1:2] * _load_routed_rows(y2_ref.at[0], tm, F32))
    ms = jnp.mean(h * h, axis=-1, keepdims=True)
    hn = (h * lax.rsqrt(ms + EPS) * g_ref[...]).astype(BF16)
    gate = jax.nn.sigmoid(jnp.dot(hn, wg_ref[...], preferred_element_type=F32))
    emb = jnp.dot(p_ref[...].astype(BF16), wp_ref[...], preferred_element_type=F32)
    o_ref[...] = h + gate * emb


def _ple(h1, yg, wcol, p2, g_ple, wg, wp):
    t, d = h1.shape
    tm = min(PLE_TM, t)
    dp = p2.shape[1]
    slabs = d // (2 * LANES)
    const = lambda i: (0, 0)
    row = lambda i: (i, 0)
    return pl.pallas_call(
        _ple_kernel,
        grid=(t // tm,),
        in_specs=[
            pl.BlockSpec((tm, d), row),
            pl.BlockSpec((1, tm * slabs, LANES), lambda i: (0, i, 0)),
            pl.BlockSpec((1, tm * slabs, LANES), lambda i: (1, i, 0)),
            pl.BlockSpec((tm, ROUTER_COLS), row),
            pl.BlockSpec((tm, dp), row),
            pl.BlockSpec((1, d), const),
            pl.BlockSpec((d, d), const),
            pl.BlockSpec((dp, d), const),
        ],
        out_specs=pl.BlockSpec((tm, d), row),
        out_shape=jax.ShapeDtypeStruct((t, d), F32),
        compiler_params=_cparams(("parallel",)),
        name="combine_ple",
    )(h1, yg, yg, wcol, p2, g_ple, wg, wp)


def kernel(x, p, rel_bias, g_mix, w_in, g_q, g_k, lam_q1, lam_k1, lam_q2, lam_k2, g_subln,
           conv_w, conv_b, g_conv, w_o, g_ffn, w_group, b_group, w_expert, b_expert,
           w1, w3, w2, g_ple, w_ple_gate, w_ple_proj):
    depth = g_mix.shape[0]
    assert depth == 1
    li = 0
    b, s, d = x.shape
    t = b * s
    d_attn = N_DIFF_HEADS * DIFF_V_DIM
    x2 = x.reshape(t, d)

    n_groups_qk = d_attn // DIFF_QK_DIM
    gqk = jnp.concatenate([jnp.tile(g_q[li] * (DIFF_QK_DIM ** -0.5 * LOG2E), n_groups_qk),
                           jnp.tile(g_k[li], n_groups_qk)])[None, :].astype(F32)
    blk = jnp.arange(LANES) // DIFF_QK_DIM
    gsum = (blk[:, None] == blk[None, :]).astype(BF16)
    lam = (jnp.exp(jnp.sum(lam_q1[li] * lam_k1[li])) - jnp.exp(jnp.sum(lam_q2[li] * lam_k2[li]))
           + LAM_INIT).reshape(1).astype(F32)
    gsub = (g_subln[li] * (1.0 - LAM_INIT))[None, :].astype(F32)
    bias_tiles = _bias_tiles(rel_bias, ATTN_TQ, ATTN_TK)

    wr_f32 = jnp.zeros((d, ROUTER_COLS), F32)
    wr_f32 = wr_f32.at[:, :N_GROUPS].set(w_group[li])
    wr_f32 = wr_f32.at[:, EXPERTS_PER_GROUP:EXPERTS_PER_GROUP + N_EXPERTS].set(w_expert[li])
    wr_hi = wr_f32.astype(BF16)
    wr_lo = (wr_f32 - wr_hi.astype(F32)).astype(BF16)
    wr = jnp.concatenate([wr_hi, wr_lo], axis=1)
    br = jnp.zeros((ROUTER_COLS,), F32)
    br = br.at[:EXPERTS_PER_GROUP].set(MASK_VALUE)
    br = br.at[:N_GROUPS].set(b_group[li])
    br = br.at[EXPERTS_PER_GROUP:EXPERTS_PER_GROUP + N_EXPERTS].set(b_expert[li])[None, :]
    tm_r = min(OUT_PROJ_TM, t)
    ar = jnp.arange(tm_r)
    tri = (ar[:, None] < ar[None, :]).astype(BF16)

    w_in_bf = w_in[li].astype(BF16)
    wo_bf = w_o[li].astype(BF16)
    w13 = jnp.concatenate([w1[li], w3[li]], axis=-1).astype(BF16)
    w2_bf = w2[li].astype(BF16)
    wg_bf = w_ple_gate[li].astype(BF16)
    wp_bf = w_ple_proj[li].astype(BF16)

    proj = _in_proj(x2, g_mix[li][None, :], w_in_bf, gqk, gsum)
    proj3 = proj.reshape(b, s, -1)
    attn = _attention(proj3, bias_tiles, gsub, lam)
    conv = _short_conv(proj3, conv_w[li], conv_b[li][None, :], g_conv[li][None, :])
    h1, hn, meta_i, wcol, counts = _out_proj(
        x2, attn.reshape(t, -1), conv.reshape(t, -1), wo_bf[:d_attn], wo_bf[d_attn:],
        g_ffn[li][None, :], wr, br, tri)

    nt_r = t // tm_r
    eid = jnp.transpose(meta_i[:, 0:2, :], (1, 0, 2)).reshape(2, t)
    rank = jnp.transpose(meta_i[:, 2:4, :], (1, 0, 2)).reshape(2, t)
    cnt = counts[:, 0].astype(jnp.int32)
    tiles_per = (cnt + MOE_TM - 1) // MOE_TM
    tile_end = jnp.cumsum(tiles_per)
    tile_start = tile_end - tiles_per
    n_tiles = (TOP_K * t) // MOE_TM + N_EXPERTS
    pos = (tile_start * MOE_TM)[eid] + rank
    tile_expert = jnp.minimum(
        jnp.searchsorted(tile_end, jnp.arange(n_tiles, dtype=jnp.int32), side="right"),
        N_EXPERTS - 1).astype(jnp.int32)
    n_valid = tile_end[-1:].astype(jnp.int32)

    slabs = d // (2 * LANES)
    row_idx = (pos[:, :, None] * slabs + jnp.arange(slabs, dtype=jnp.int32)).reshape(1, -1)

    xs = _sc_scatter_rows(hn, row_idx, n_tiles * MOE_TM * slabs)
    ys = _moe(xs, w13, w2_bf, tile_expert, n_valid)
    yg = _sc_gather_rows(ys, row_idx).reshape(TOP_K, t * slabs, LANES)

    out = _ple(h1, yg, wcol, p[li].reshape(t, -1), g_ple[li][None, :], wg_bf, wp_bf)
    return out.reshape(b, s, d)
```

```python
import functools
import math

import jax
import jax.numpy as jnp
from jax import lax
from jax.experimental import pallas as pl
from jax.experimental.pallas import tpu as pltpu
from jax.experimental.pallas import tpu_sc as plsc

F32 = jnp.float32
BF16 = jnp.bfloat16

N_DIFF_HEADS = 8
DIFF_QK_DIM = 64
DIFF_V_DIM = 128
CHUNK = 64
NUM_BUCKETS = 32
MAX_DISTANCE = 128
CONV_WIDTH = 3
CONV_GROUP_DIM = 128
N_GROUPS = 4
EXPERTS_PER_GROUP = 8
N_EXPERTS = N_GROUPS * EXPERTS_PER_GROUP
TOP_K = 2
EPS = 1e-6
MASK_VALUE = -1e30
LAM_INIT = 0.8 - 0.6 * math.exp(-0.3 * 0)
LOG2E = math.log2(math.e)

LANES = 128
SUBLANES = 8
VMEM_LIMIT_BYTES = 56 * 1024 * 1024

IN_PROJ_TM = 1024
IN_PROJ_TN = 512
ATTN_TQ = 512
ATTN_TK = 256
CONV_TS = 512
OUT_PROJ_TM = 512
MOE_TM = 256
PLE_TM = 512
ROUTER_COLS = 128
SC_WINDOW = 128
SC_NUM_CORES = 2
SC_NUM_SUBCORES = 16


def _cparams(semantics):
    return pltpu.CompilerParams(dimension_semantics=semantics,
                                vmem_limit_bytes=VMEM_LIMIT_BYTES)


def _store_routed_rows(o_ref, x):
    tm, d = x.shape
    half = d // 2
    slabs = half // LANES
    xr = x.astype(BF16).astype(F32)
    lo = lax.bitcast_convert_type(xr[:, :half], jnp.uint32)
    hi = lax.bitcast_convert_type(xr[:, half:], jnp.uint32)
    packed = (lo >> 16) | (hi & jnp.uint32(0xFFFF0000))
    for r in range(slabs):
        o_ref[pl.ds(r, tm, stride=slabs), :] = packed[:, r * LANES:(r + 1) * LANES]


def _load_routed_rows(x_ref, tm, dtype):
    slabs = x_ref.shape[0] // tm
    parts = [x_ref[pl.ds(r, tm, stride=slabs), :] for r in range(slabs)]
    lo = [lax.bitcast_convert_type(w << 16, F32).astype(dtype) for w in parts]
    hi = [lax.bitcast_convert_type(w & jnp.uint32(0xFFFF0000), F32).astype(dtype) for w in parts]
    return jnp.concatenate(lo + hi, axis=1)


def _sc_mesh():
    return plsc.VectorSubcoreMesh(core_axis_name="core", subcore_axis_name="subcore",
                                  num_cores=SC_NUM_CORES, num_subcores=SC_NUM_SUBCORES)


def _sc_scatter_rows(rows, idx, n_out_rows):
    n_src, n_idx = rows.shape[0], idx.shape[1]
    src_windows = n_src // SC_WINDOW

    @functools.partial(pl.kernel, mesh=_sc_mesh(), scratch_types=[],
                       out_type=jax.ShapeDtypeStruct((n_out_rows, LANES), rows.dtype))
    def scatter(x_hbm, i_hbm, o_hbm):
        def body(x_vmem, i_vmem):
            pltpu.sync_copy(x_vmem, o_hbm.at[i_vmem.at[0]])

        pltpu.emit_pipeline(
            body,
            grid=(n_idx // SC_WINDOW,),
            in_specs=[pl.BlockSpec((SC_WINDOW, LANES), lambda i: (lax.rem(i, src_windows), 0)),
                      pl.BlockSpec((1, SC_WINDOW), lambda i: (0, i))],
            out_specs=[],
            core_axis_name=("core", "subcore"),
            dimension_semantics=(pltpu.PARALLEL,),
        )(x_hbm, i_hbm)

    return scatter(rows, idx)


def _sc_gather_rows(table, idx):
    n_idx = idx.shape[1]

    @functools.partial(pl.kernel, mesh=_sc_mesh(), scratch_types=[],
                       out_type=jax.ShapeDtypeStruct((n_idx, LANES), table.dtype))
    def gather(t_hbm, i_hbm, o_hbm):
        def body(i_vmem, o_vmem):
            pltpu.sync_copy(t_hbm.at[i_vmem.at[0]], o_vmem)

        pltpu.emit_pipeline(
            body,
            grid=(n_idx // SC_WINDOW,),
            in_specs=[pl.BlockSpec((1, SC_WINDOW), lambda i: (0, i))],
            out_specs=[pl.BlockSpec((SC_WINDOW, LANES), lambda i: (i, 0))],
            core_axis_name=("core", "subcore"),
            dimension_semantics=(pltpu.PARALLEL,),
        )(i_hbm, o_hbm)

    return gather(table, idx)


def _in_proj_kernel(x_ref, g_ref, w_ref, gqk_ref, gsum_ref, o_ref, xn_ref, *, n_qk_tiles):
    j = pl.program_id(1)

    @pl.when(j == 0)
    def _():
        x = x_ref[...]
        ms = jnp.mean(x * x, axis=-1, keepdims=True)
        xn_ref[...] = (x * lax.rsqrt(ms + EPS) * g_ref[...]).astype(BF16)

    acc = jnp.dot(xn_ref[...], w_ref[...], preferred_element_type=F32)

    @pl.when(j < n_qk_tiles)
    def _():
        for c in range(acc.shape[1] // LANES):
            a = acc[:, c * LANES:(c + 1) * LANES]
            sq = a * a
            hi = sq.astype(BF16)
            lo = (sq - hi.astype(F32)).astype(BF16)
            ss = (jnp.dot(hi, gsum_ref[...], preferred_element_type=F32)
                  + jnp.dot(lo, gsum_ref[...], preferred_element_type=F32))
            y = a * lax.rsqrt(ss * (1.0 / DIFF_QK_DIM) + EPS)
            o_ref[:, c * LANES:(c + 1) * LANES] = (
                y * gqk_ref[:, c * LANES:(c + 1) * LANES]).astype(o_ref.dtype)

    @pl.when(j >= n_qk_tiles)
    def _():
        o_ref[...] = acc.astype(o_ref.dtype)


def _in_proj(x2, g_mix, w_in_bf, gqk, gsum):
    t, d = x2.shape
    n_out = w_in_bf.shape[1]
    tm, tn = min(IN_PROJ_TM, t), IN_PROJ_TN
    n_qk_tiles = gqk.shape[1] // tn
    return pl.pallas_call(
        functools.partial(_in_proj_kernel, n_qk_tiles=n_qk_tiles),
        grid=(t // tm, n_out // tn),
        in_specs=[
            pl.BlockSpec((tm, d), lambda i, j: (i, 0)),
            pl.BlockSpec((1, d), lambda i, j: (0, 0)),
            pl.BlockSpec((d, tn), lambda i, j: (0, j)),
            pl.BlockSpec((1, tn), lambda i, j: (0, jnp.minimum(j, n_qk_tiles - 1))),
            pl.BlockSpec((LANES, LANES), lambda i, j: (0, 0)),
        ],
        out_specs=pl.BlockSpec((tm, tn), lambda i, j: (i, j)),
        out_shape=jax.ShapeDtypeStruct((t, n_out), BF16),
        scratch_shapes=[pltpu.VMEM((tm, d), BF16)],
        compiler_params=_cparams(("parallel", "arbitrary")),
        name="in_proj",
    )(x2, g_mix, w_in_bf, gqk, gsum)


def _attn_kernel(lam_ref, q_ref, k_ref, v_ref, bias_ref, gsub_ref, o_ref,
                 qs_ref, m_ref, acc_ref, s0_ref, s1_ref, p0_ref, p1_ref, a0_ref, a1_ref, *, tq, tk):
    qi = pl.program_id(2)
    dv = DIFF_V_DIM
    s_bufs, p_bufs, a_bufs = (s0_ref, s1_ref), (p0_ref, p1_ref), (a0_ref, a1_ref)

    q = q_ref[0]
    lane = lax.broadcasted_iota(jnp.int32, q.shape, 1)
    zero = jnp.zeros_like(q)
    qs_ref[0:tq, :] = jnp.where(lane < DIFF_QK_DIM, q, zero)
    qs_ref[tq:, :] = jnp.where(lane < DIFF_QK_DIM, zero, q)
    acc_ref[...] = jnp.zeros_like(acc_ref)
    m_ref[...] = jnp.full_like(m_ref, MASK_VALUE)
    p1_ref[...] = jnp.zeros_like(p1_ref)
    a1_ref[...] = jnp.zeros_like(a1_ref)
    ones = jnp.ones((tk, dv), BF16)

    def block_start(step):
        return pl.multiple_of(jnp.maximum(step, 0) * tk, tk)

    def logits_stage(step, dst):
        dst[...] = lax.dot_general(qs_ref[...], k_ref[0, pl.ds(block_start(step), tk), :],
                                   (((1,), (1,)), ((), ())), preferred_element_type=F32)

    def pv_stage(step, buf):
        vx = jnp.concatenate([v_ref[0, pl.ds(block_start(step), tk), :], ones], axis=1)
        alpha = a_bufs[buf][...]
        acc_ref[...] = (jnp.concatenate([alpha, alpha], axis=1) * acc_ref[...]
                        + jnp.dot(p_bufs[buf][...], vx, preferred_element_type=F32))

    def softmax_stage(buf, bias_tile):
        s = s_bufs[buf][...]
        if bias_tile is not None:
            bias = bias_ref[0, bias_tile]
            s = jnp.concatenate([s[:tq] + bias, s[tq:] + bias], axis=0)
        m_prev = m_ref[...]
        m_new = jnp.maximum(m_prev, jnp.max(s, axis=-1, keepdims=True))
        a_bufs[buf][...] = jnp.exp2(m_prev - m_new)
        m_ref[...] = m_new
        p = jnp.exp2(s - jnp.concatenate([m_new] * (tk // LANES), axis=1))
        p_bufs[buf][...] = p.astype(BF16)

    def pair(step0, bias0, bias1, lookahead=True):
        logits_stage(step0 + 1, s_bufs[1])
        pv_stage(step0 - 1, 1)
        softmax_stage(0, bias0)
        if lookahead:
            logits_stage(step0 + 2, s_bufs[0])
        pv_stage(step0, 0)
        softmax_stage(1, bias1)

    logits_stage(0, s_bufs[0])

    def far_pair(jj, carry):
        pair(2 * jj, None, None)
        return carry

    lax.fori_loop(0, jnp.maximum(qi - 1, 0), far_pair, 0)

    @pl.when(qi > 0)
    def _():
        pair(2 * qi - 2, None, 2)

    pair(2 * qi, 0, 1, lookahead=False)
    pv_stage(2 * qi + 1, 1)

    acc = acc_ref[...]
    o = (acc[:tq, :dv] / acc[:tq, dv:]) - lam_ref[0] * (acc[tq:, :dv] / acc[tq:, dv:])
    ms = jnp.mean(o * o, axis=-1, keepdims=True)
    o_ref[0] = (o * lax.rsqrt(ms + EPS) * gsub_ref[...]).astype(o_ref.dtype)


def _attention(proj3, bias_tiles, gsub, lam):
    b, s, _ = proj3.shape
    h = N_DIFF_HEADS
    tq, tk = ATTN_TQ, ATTN_TK
    assert tq == 2 * tk and s % tq == 0
    kern = functools.partial(_attn_kernel, tq=tq, tk=tk)
    return pl.pallas_call(
        kern,
        grid=(b, h, s // tq),
        in_specs=[
            pl.BlockSpec(memory_space=pltpu.SMEM),
            pl.BlockSpec((1, tq, LANES), lambda bi, hi, qi: (bi, qi, hi)),
            pl.BlockSpec((1, s, LANES), lambda bi, hi, qi: (bi, 0, h + hi)),
            pl.BlockSpec((1, s, LANES), lambda bi, hi, qi: (bi, 0, 2 * h + hi)),
            pl.BlockSpec((1, 3, tq, tk), lambda bi, hi, qi: (hi, 0, 0, 0)),
            pl.BlockSpec((1, LANES), lambda bi, hi, qi: (0, 0)),
        ],
        out_specs=pl.BlockSpec((1, tq, LANES), lambda bi, hi, qi: (bi, qi, hi)),
        out_shape=jax.ShapeDtypeStruct((b, s, h * DIFF_V_DIM), BF16),
        scratch_shapes=[pltpu.VMEM((2 * tq, LANES), BF16),
                        pltpu.VMEM((2 * tq, LANES), F32),
                        pltpu.VMEM((2 * tq, 2 * DIFF_V_DIM), F32),
                        pltpu.VMEM((2 * tq, tk), F32),
                        pltpu.VMEM((2 * tq, tk), F32),
                        pltpu.VMEM((2 * tq, tk), BF16),
                        pltpu.VMEM((2 * tq, tk), BF16),
                        pltpu.VMEM((2 * tq, LANES), F32),
                        pltpu.VMEM((2 * tq, LANES), F32)],
        compiler_params=_cparams(("parallel", "parallel", "arbitrary")),
        name="diff_attention",
    )(lam, proj3, proj3, proj3, bias_tiles, gsub)


def _rel_bucket(rel):
    nb = NUM_BUCKETS // 2
    max_exact = nb // 2
    n = jnp.abs(rel)
    n2 = n * n
    large = max_exact + sum((n2 >= (max_exact * max_exact) * (2 ** k)).astype(jnp.int32)
                            for k in range(1, nb - max_exact))
    return jnp.where(rel > 0, nb, 0) + jnp.where(n < max_exact, n, large)


def _bias_kernel(rb_ref, o_ref, *, tk):
    h = pl.program_id(0)
    tq = o_ref.shape[2]
    qpos = lax.broadcasted_iota(jnp.int32, (tq, tk), 0)
    kcol = lax.broadcasted_iota(jnp.int32, (tq, tk), 1)
    chunk_shift = CHUNK.bit_length() - 1
    far_bias = rb_ref[NUM_BUCKETS // 2 - 1, h]
    for tile, offset in enumerate((0, tk, -tk)):
        kpos = kcol + offset
        bucket = _rel_bucket(kpos - qpos)
        bias = jnp.zeros((tq, tk), F32)
        for b in range(NUM_BUCKETS):
            bias = jnp.where(bucket == b, rb_ref[b, h], bias)
        mask = (kpos >> chunk_shift) <= (qpos >> chunk_shift)
        o_ref[0, tile] = jnp.where(mask, (bias - far_bias) * LOG2E, MASK_VALUE)


def _bias_tiles(rel_bias, tq, tk):
    assert tk >= MAX_DISTANCE
    assert CHUNK & (CHUNK - 1) == 0
    n_heads = rel_bias.shape[1]
    return pl.pallas_call(
        functools.partial(_bias_kernel, tk=tk),
        grid=(n_heads,),
        in_specs=[pl.BlockSpec(memory_space=pltpu.SMEM)],
        out_specs=pl.BlockSpec((1, 3, tq, tk), lambda h: (h, 0, 0, 0)),
        out_shape=jax.ShapeDtypeStruct((n_heads, 3, tq, tk), F32),
        compiler_params=_cparams(("parallel",)),
        name="bias_tiles",
    )(rel_bias)


def _conv_kernel(b_ref, c_ref, u_ref, cp_ref, up_ref, w_ref, cb_ref, g_ref, o_ref, buf_ref):
    si = pl.program_id(1)
    ts = o_ref.shape[1]
    cu = c_ref[0].astype(F32) * u_ref[0].astype(F32)
    prev = cp_ref[0].astype(F32) * up_ref[0].astype(F32)
    buf_ref[0:SUBLANES, :] = jnp.where(si > 0, prev, 0.0)
    buf_ref[SUBLANES:, :] = cu
    conv = (w_ref[0:1, :] * buf_ref[pl.ds(SUBLANES - 2, ts), :]
            + w_ref[1:2, :] * buf_ref[pl.ds(SUBLANES - 1, ts), :]
            + w_ref[2:3, :] * cu)
    z = b_ref[0].astype(F32) * (conv + cb_ref[...])
    for c in range(z.shape[1] // CONV_GROUP_DIM):
        sl = slice(c * CONV_GROUP_DIM, (c + 1) * CONV_GROUP_DIM)
        zc = z[:, sl]
        ms = jnp.mean(zc * zc, axis=-1, keepdims=True)
        o_ref[0, :, sl] = (zc * lax.rsqrt(ms + EPS) * g_ref[:, sl]).astype(o_ref.dtype)


def _short_conv(proj3, conv_w, conv_b, g_conv):
    b, s, n = proj3.shape
    dc = conv_w.shape[1]
    ts = min(CONV_TS, s)
    col0 = (n - 3 * dc) // dc
    halo = ts // SUBLANES

    def main(col):
        return pl.BlockSpec((1, ts, dc), lambda bi, si: (bi, si, col))

    def prev(col):
        return pl.BlockSpec((1, SUBLANES, dc),
                            lambda bi, si: (bi, jnp.maximum(si * halo - 1, 0), col))

    return pl.pallas_call(
        _conv_kernel,
        grid=(b, s // ts),
        in_specs=[main(col0), main(col0 + 1), main(col0 + 2), prev(col0 + 1), prev(col0 + 2),
                  pl.BlockSpec((CONV_WIDTH, dc), lambda bi, si: (0, 0)),
                  pl.BlockSpec((1, dc), lambda bi, si: (0, 0)),
                  pl.BlockSpec((1, dc), lambda bi, si: (0, 0))],
        out_specs=pl.BlockSpec((1, ts, dc), lambda bi, si: (bi, si, 0)),
        out_shape=jax.ShapeDtypeStruct((b, s, dc), BF16),
        scratch_shapes=[pltpu.VMEM((ts + SUBLANES, dc), F32)],
        compiler_params=_cparams(("parallel", "parallel")),
        name="short_conv",
    )(proj3, proj3, proj3, proj3, proj3, conv_w, conv_b, g_conv)


def _out_proj_kernel(x_ref, a_ref, c_ref, wa_ref, wc_ref, g_ref, wr_ref, br_ref, tri_ref,
                     h_ref, hn_ref, mi_ref, wcol_ref, cnt_ref, carry_ref):
    i = pl.program_id(0)
    tm = x_ref.shape[0]

    @pl.when(i == 0)
    def _():
        carry_ref[...] = jnp.zeros_like(carry_ref)

    h = (x_ref[...]
         + jnp.dot(a_ref[...], wa_ref[...], preferred_element_type=F32)
         + jnp.dot(c_ref[...], wc_ref[...], preferred_element_type=F32))
    h_ref[...] = h
    ms = jnp.mean(h * h, axis=-1, keepdims=True)
    hn = h * lax.rsqrt(ms + EPS) * g_ref[...]
    hn_hi = hn.astype(BF16)
    _store_routed_rows(hn_ref, hn)
    hn_lo = (hn - hn_hi.astype(F32)).astype(BF16)

    r_hi = jnp.dot(hn_hi, wr_ref[...], preferred_element_type=F32)
    r_lo = jnp.dot(hn_lo, wr_ref[...], preferred_element_type=F32)
    logits = (r_hi[:, :ROUTER_COLS] + r_hi[:, ROUTER_COLS:] + r_lo[:, :ROUTER_COLS]
              + br_ref[...])
    lt = logits.T

    e = EXPERTS_PER_GROUP
    row = lax.broadcasted_iota(jnp.int32, (e, tm), 0)

    def first_argmax(v):
        vmax = jnp.max(v, axis=0, keepdims=True)
        idx = jnp.min(jnp.where(v == vmax, row, e), axis=0, keepdims=True)
        return vmax, idx

    gl = lt[0:e]
    gmax, g = first_argmax(gl)
    p_g = 1.0 / jnp.sum(jnp.exp(gl - gmax), axis=0, keepdims=True)
    ing = lt[e:2 * e]
    for gi in range(1, N_GROUPS):
        ing = jnp.where(g == gi, lt[(gi + 1) * e:(gi + 2) * e], ing)
    v1, i1 = first_argmax(ing)
    v2, i2 = first_argmax(jnp.where(row == i1, -jnp.inf, ing))
    ex = jnp.exp(v2 - v1)
    w1 = p_g / (1.0 + ex)
    w2 = w1 * ex
    e1 = g * e + i1
    e2 = g * e + i2

    erow = lax.broadcasted_iota(jnp.int32, (N_EXPERTS, tm), 0)
    oh1 = (erow == e1).astype(F32)
    oh2 = (erow == e2).astype(F32)
    oh = jnp.concatenate([oh1, oh2], axis=0).astype(BF16)
    pre = jnp.dot(oh, tri_ref[...], preferred_element_type=F32)
    cnt1 = jnp.sum(oh1, axis=1, keepdims=True)
    cnt2 = jnp.sum(oh2, axis=1, keepdims=True)
    carry = carry_ref[:, 0:1]
    r1 = jnp.sum(oh1 * (pre[:N_EXPERTS] + carry), axis=0, keepdims=True)
    r2 = jnp.sum(oh2 * (pre[N_EXPERTS:] + carry + cnt1), axis=0, keepdims=True)
    new_carry = carry + cnt1 + cnt2
    carry_ref[...] = jnp.broadcast_to(new_carry, carry_ref.shape)
    cnt_ref[...] = jnp.broadcast_to(new_carry, cnt_ref.shape)

    mi_ref[0] = jnp.concatenate(
        [e1, e2, r1.astype(jnp.int32), r2.astype(jnp.int32),
         jnp.zeros((SUBLANES - 4, tm), jnp.int32)], axis=0)
    wrow = jnp.concatenate([w1, w2, jnp.zeros((ROUTER_COLS - 2, tm), F32)], axis=0)
    wcol_ref[...] = wrow.T


def _out_proj(x2, attn2, conv2, wo_a, wo_c, g_ffn, wr, br, tri):
    t, d = x2.shape
    tm = min(OUT_PROJ_TM, t)
    nt = t // tm
    da, dc = attn2.shape[1], conv2.shape[1]
    slabs = d // (2 * LANES)
    const = lambda i: (0, 0)
    return pl.pallas_call(
        _out_proj_kernel,
        grid=(nt,),
        in_specs=[
            pl.BlockSpec((tm, d), lambda i: (i, 0)),
            pl.BlockSpec((tm, da), lambda i: (i, 0)),
            pl.BlockSpec((tm, dc), lambda i: (i, 0)),
            pl.BlockSpec((da, d), const),
            pl.BlockSpec((dc, d), const),
            pl.BlockSpec((1, d), const),
            pl.BlockSpec((d, 2 * ROUTER_COLS), const),
            pl.BlockSpec((1, ROUTER_COLS), const),
            pl.BlockSpec((tm, tm), const),
        ],
        out_specs=[
            pl.BlockSpec((tm, d), lambda i: (i, 0)),
            pl.BlockSpec((tm * slabs, LANES), lambda i: (i, 0)),
            pl.BlockSpec((1, SUBLANES, tm), lambda i: (i, 0, 0)),
            pl.BlockSpec((tm, ROUTER_COLS), lambda i: (i, 0)),
            pl.BlockSpec((N_EXPERTS, LANES), const),
        ],
        out_shape=[
            jax.ShapeDtypeStruct((t, d), F32),
            jax.ShapeDtypeStruct((t * slabs, LANES), jnp.uint32),
            jax.ShapeDtypeStruct((nt, SUBLANES, tm), jnp.int32),
            jax.ShapeDtypeStruct((t, ROUTER_COLS), F32),
            jax.ShapeDtypeStruct((N_EXPERTS, LANES), F32),
        ],
        scratch_shapes=[pltpu.VMEM((N_EXPERTS, LANES), F32)],
        compiler_params=_cparams(("arbitrary",)),
        name="out_proj_router",
    )(x2, attn2, conv2, wo_a, wo_c, g_ffn, wr, br, tri)


def _moe_kernel(te_ref, nv_ref, x_ref, w13_ref, w2_ref, o_ref):
    i = pl.program_id(0)

    @pl.when(i < nv_ref[0])
    def _():
        dff = w2_ref.shape[1]
        x = _load_routed_rows(x_ref, MOE_TM, BF16)
        ab = jnp.dot(x, w13_ref[0], preferred_element_type=F32)
        a, b = ab[:, :dff], ab[:, dff:]
        hid = (a * jax.nn.sigmoid(a) * b).astype(BF16)
        _store_routed_rows(o_ref, jnp.dot(hid, w2_ref[0], preferred_element_type=F32))

    @pl.when(i >= nv_ref[0])
    def _():
        o_ref[...] = jnp.zeros_like(o_ref)


def _moe(xs, w13, w2, tile_expert, n_valid):
    d, dff = w2.shape[2], w2.shape[1]
    slabs = d // (2 * LANES)
    rows = MOE_TM * slabs
    nt = xs.shape[0] // rows
    grid_spec = pltpu.PrefetchScalarGridSpec(
        num_scalar_prefetch=2,
        grid=(nt,),
        in_specs=[
            pl.BlockSpec((rows, LANES), lambda i, te, nv: (jnp.minimum(i, nv[0] - 1), 0)),
            pl.BlockSpec((1, d, 2 * dff), lambda i, te, nv: (te[i], 0, 0)),
            pl.BlockSpec((1, dff, d), lambda i, te, nv: (te[i], 0, 0)),
        ],
        out_specs=pl.BlockSpec((rows, LANES), lambda i, te, nv: (i, 0)),
    )
    return pl.pallas_call(
        _moe_kernel,
        grid_spec=grid_spec,
        out_shape=jax.ShapeDtypeStruct(xs.shape, jnp.uint32),
        compiler_params=_cparams(("arbitrary",)),
        name="moe_grouped",
    )(tile_expert, n_valid, xs, w13, w2)


def _ple_kernel(h_ref, y1_ref, y2_ref, wcol_ref, p_ref, g_ref, wg_ref, wp_ref, o_ref):
    wcol = wcol_ref[...]
    tm = h_ref.shape[0]
    h = (h_ref[...]
         + wcol[:, 0:1] * _load_routed_rows(y1_ref.at[0], tm, F32)
         + wcol[:, 1:2] * _load_routed_rows(y2_ref.at[0], tm, F32))
    ms = jnp.mean(h * h, axis=-1, keepdims=True)
    hn = (h * lax.rsqrt(ms + EPS) * g_ref[...]).astype(BF16)
    gate = jax.nn.sigmoid(jnp.dot(hn, wg_ref[...], preferred_element_type=F32))
    emb = jnp.dot(p_ref[...].astype(BF16), wp_ref[...], preferred_element_type=F32)
    o_ref[...] = h + gate * emb


def _ple(h1, yg, wcol, p2, g_ple, wg, wp):
    t, d = h1.shape
    tm = min(PLE_TM, t)
    dp = p2.shape[1]
    slabs = d // (2 * LANES)
    const = lambda i: (0, 0)
    row = lambda i: (i, 0)
    return pl.pallas_call(
        _ple_kernel,
        grid=(t // tm,),
        in_specs=[
            pl.BlockSpec((tm, d), row),
            pl.BlockSpec((1, tm * slabs, LANES), lambda i: (0, i, 0)),
            pl.BlockSpec((1, tm * slabs, LANES), lambda i: (1, i, 0)),
            pl.BlockSpec((tm, ROUTER_COLS), row),
            pl.BlockSpec((tm, dp), row),
            pl.BlockSpec((1, d), const),
            pl.BlockSpec((d, d), const),
            pl.BlockSpec((dp, d), const),
        ],
        out_specs=pl.BlockSpec((tm, d), row),
        out_shape=jax.ShapeDtypeStruct((t, d), F32),
        compiler_params=_cparams(("parallel",)),
        name="combine_ple",
    )(h1, yg, yg, wcol, p2, g_ple, wg, wp)


def kernel(x, p, rel_bias, g_mix, w_in, g_q, g_k, lam_q1, lam_k1, lam_q2, lam_k2, g_subln,
           conv_w, conv_b, g_conv, w_o, g_ffn, w_group, b_group, w_expert, b_expert,
           w1, w3, w2, g_ple, w_ple_gate, w_ple_proj):
    depth = g_mix.shape[0]
    assert depth == 1
    li = 0
    b, s, d = x.shape
    t = b * s
    d_attn = N_DIFF_HEADS * DIFF_V_DIM
    x2 = x.reshape(t, d)

    n_groups_qk = d_attn // DIFF_QK_DIM
    gqk = jnp.concatenate([jnp.tile(g_q[li] * (DIFF_QK_DIM ** -0.5 * LOG2E), n_groups_qk),
                           jnp.tile(g_k[li], n_groups_qk)])[None, :].astype(F32)
    blk = jnp.arange(LANES) // DIFF_QK_DIM
    gsum = (blk[:, None] == blk[None, :]).astype(BF16)
    lam = (jnp.exp(jnp.sum(lam_q1[li] * lam_k1[li])) - jnp.exp(jnp.sum(lam_q2[li] * lam_k2[li]))
           + LAM_INIT).reshape(1).astype(F32)
    gsub = (g_subln[li] * (1.0 - LAM_INIT))[None, :].astype(F32)
    bias_tiles = _bias_tiles(rel_bias, ATTN_TQ, ATTN_TK)

    pad_g = EXPERTS_PER_GROUP - N_GROUPS
    pad_e = ROUTER_COLS - EXPERTS_PER_GROUP - N_EXPERTS
    wr_f32 = jnp.concatenate([w_group[li], jnp.zeros((d, pad_g), F32),
                              w_expert[li], jnp.zeros((d, pad_e), F32)], axis=1)
    wr_hi = wr_f32.astype(BF16)
    wr_lo = (wr_f32 - wr_hi.astype(F32)).astype(BF16)
    wr = jnp.concatenate([wr_hi, wr_lo], axis=1)
    br = jnp.concatenate([b_group[li], jnp.full((pad_g,), MASK_VALUE, F32),
                          b_expert[li], jnp.zeros((pad_e,), F32)])[None, :]
    tm_r = min(OUT_PROJ_TM, t)
    ar = jnp.arange(tm_r)
    tri = (ar[:, None] < ar[None, :]).astype(BF16)

    w_in_bf = w_in[li].astype(BF16)
    wo_bf = w_o[li].astype(BF16)
    w13 = jnp.concatenate([w1[li], w3[li]], axis=-1).astype(BF16)
    w2_bf = w2[li].astype(BF16)
    wg_bf = w_ple_gate[li].astype(BF16)
    wp_bf = w_ple_proj[li].astype(BF16)

    proj = _in_proj(x2, g_mix[li][None, :], w_in_bf, gqk, gsum)
    proj3 = proj.reshape(b, s, -1)
    attn = _attention(proj3, bias_tiles, gsub, lam)
    conv = _short_conv(proj3, conv_w[li], conv_b[li][None, :], g_conv[li][None, :])
    h1, hn, meta_i, wcol, counts = _out_proj(
        x2, attn.reshape(t, -1), conv.reshape(t, -1), wo_bf[:d_attn], wo_bf[d_attn:],
        g_ffn[li][None, :], wr, br, tri)

    eid = jnp.transpose(meta_i[:, 0:2, :], (1, 0, 2)).reshape(2, t)
    rank = jnp.transpose(meta_i[:, 2:4, :], (1, 0, 2)).reshape(2, t)
    cnt = counts[:, 0].astype(jnp.int32)
    tiles_per = (cnt + MOE_TM - 1) // MOE_TM
    experts = jnp.arange(N_EXPERTS, dtype=jnp.int32)
    tile_end = jnp.sum(jnp.where(experts[:, None] <= experts[None, :], tiles_per[:, None], 0),
                       axis=0)
    row_start = (tile_end - tiles_per) * MOE_TM
    n_tiles = (TOP_K * t) // MOE_TM + N_EXPERTS
    pos = rank + jnp.sum(jnp.where(eid[..., None] == experts, row_start, 0), axis=-1)
    tile_expert = jnp.minimum(
        jnp.sum((jnp.arange(n_tiles, dtype=jnp.int32)[:, None] >= tile_end[None, :])
                .astype(jnp.int32), axis=1),
        N_EXPERTS - 1)
    n_valid = tile_end[-1:]

    slabs = d // (2 * LANES)
    row_idx = (pos[:, :, None] * slabs + jnp.arange(slabs, dtype=jnp.int32)).reshape(1, -1)

    xs = _sc_scatter_rows(hn, row_idx, n_tiles * MOE_TM * slabs)
    ys = _moe(xs, w13, w2_bf, tile_expert, n_valid)
    yg = _sc_gather_rows(ys, row_idx).reshape(TOP_K, t * slabs, LANES)

    out = _ple(h1, yg, wcol, p[li].reshape(t, -1), g_ple[li][None, :], wg_bf, wp_bf)
    return out.reshape(b, s, d)
```

```python
import functools
import math

import jax
import jax.numpy as jnp
from jax import lax
from jax.experimental import pallas as pl
from jax.experimental.pallas import tpu as pltpu
from jax.experimental.pallas import tpu_sc as plsc

F32 = jnp.float32
BF16 = jnp.bfloat16

N_DIFF_HEADS = 8
DIFF_QK_DIM = 64
DIFF_V_DIM = 128
CHUNK = 64
NUM_BUCKETS = 32
MAX_DISTANCE = 128
CONV_WIDTH = 3
CONV_GROUP_DIM = 128
N_GROUPS = 4
EXPERTS_PER_GROUP = 8
N_EXPERTS = N_GROUPS * EXPERTS_PER_GROUP
TOP_K = 2
EPS = 1e-6
MASK_VALUE = -1e30
LAM_INIT = 0.8 - 0.6 * math.exp(-0.3 * 0)
LOG2E = math.log2(math.e)

LANES = 128
SUBLANES = 8
MXU_WIDTH = 256
VMEM_LIMIT_BYTES = 56 * 1024 * 1024

IN_PROJ_TM = 1024
IN_PROJ_TN = 512
ATTN_TQ = 512
ATTN_TK = 256
CONV_TS = 512
OUT_PROJ_TM = 512
MOE_TM = 256
PLE_TM = 512
ROUTER_COLS = 128
SC_WINDOW = 128
SC_NUM_CORES = 2
SC_NUM_SUBCORES = 16


def _cparams(semantics):
    return pltpu.CompilerParams(dimension_semantics=semantics,
                                vmem_limit_bytes=VMEM_LIMIT_BYTES)


def _store_routed_rows(o_ref, x):
    tm, d = x.shape
    half = d // 2
    slabs = half // LANES
    xr = x.astype(BF16).astype(F32)
    lo = lax.bitcast_convert_type(xr[:, :half], jnp.uint32)
    hi = lax.bitcast_convert_type(xr[:, half:], jnp.uint32)
    packed = (lo >> 16) | (hi & jnp.uint32(0xFFFF0000))
    for r in range(slabs):
        o_ref[pl.ds(r, tm, stride=slabs), :] = packed[:, r * LANES:(r + 1) * LANES]


def _load_routed_rows(x_ref, tm, dtype):
    slabs = x_ref.shape[0] // tm
    parts = [x_ref[pl.ds(r, tm, stride=slabs), :] for r in range(slabs)]
    lo = [lax.bitcast_convert_type(w << 16, F32).astype(dtype) for w in parts]
    hi = [lax.bitcast_convert_type(w & jnp.uint32(0xFFFF0000), F32).astype(dtype) for w in parts]
    return jnp.concatenate(lo + hi, axis=1)


def _sc_mesh():
    return plsc.VectorSubcoreMesh(core_axis_name="core", subcore_axis_name="subcore",
                                  num_cores=SC_NUM_CORES, num_subcores=SC_NUM_SUBCORES)


def _sc_scatter_rows(rows, idx, n_out_rows):
    n_src, n_idx = rows.shape[0], idx.shape[1]
    src_windows = n_src // SC_WINDOW

    @functools.partial(pl.kernel, mesh=_sc_mesh(), scratch_types=[],
                       out_type=jax.ShapeDtypeStruct((n_out_rows, LANES), rows.dtype))
    def scatter(x_hbm, i_hbm, o_hbm):
        def body(x_vmem, i_vmem):
            pltpu.sync_copy(x_vmem, o_hbm.at[i_vmem.at[0]])

        pltpu.emit_pipeline(
            body,
            grid=(n_idx // SC_WINDOW,),
            in_specs=[pl.BlockSpec((SC_WINDOW, LANES), lambda i: (lax.rem(i, src_windows), 0)),
                      pl.BlockSpec((1, SC_WINDOW), lambda i: (0, i))],
            out_specs=[],
            core_axis_name=("core", "subcore"),
            dimension_semantics=(pltpu.PARALLEL,),
        )(x_hbm, i_hbm)

    return scatter(rows, idx)


def _sc_gather_rows(table, idx):
    n_idx = idx.shape[1]

    @functools.partial(pl.kernel, mesh=_sc_mesh(), scratch_types=[],
                       out_type=jax.ShapeDtypeStruct((n_idx, LANES), table.dtype))
    def gather(t_hbm, i_hbm, o_hbm):
        def body(i_vmem, o_vmem):
            pltpu.sync_copy(t_hbm.at[i_vmem.at[0]], o_vmem)

        pltpu.emit_pipeline(
            body,
            grid=(n_idx // SC_WINDOW,),
            in_specs=[pl.BlockSpec((1, SC_WINDOW), lambda i: (0, i))],
            out_specs=[pl.BlockSpec((SC_WINDOW, LANES), lambda i: (i, 0))],
            core_axis_name=("core", "subcore"),
            dimension_semantics=(pltpu.PARALLEL,),
        )(i_hbm, o_hbm)

    return gather(table, idx)


def _in_proj_kernel(x_ref, g_ref, w_ref, *rest, qk_norm):
    if qk_norm:
        gqk_ref, gsum_ref, o_ref, xn_ref = rest
    else:
        o_ref, xn_ref = rest

    @pl.when(pl.program_id(1) == 0)
    def _():
        x = x_ref[...]
        ms = jnp.mean(x * x, axis=-1, keepdims=True)
        xn_ref[...] = (x * lax.rsqrt(ms + EPS) * g_ref[...]).astype(BF16)

    acc = jnp.dot(xn_ref[...], w_ref[...], preferred_element_type=F32)
    if not qk_norm:
        o_ref[...] = acc.astype(o_ref.dtype)
        return
    for c in range(acc.shape[1] // MXU_WIDTH):
        sl = slice(c * MXU_WIDTH, (c + 1) * MXU_WIDTH)
        a = acc[:, sl]
        sq = a * a
        hi = sq.astype(BF16)
        lo = (sq - hi.astype(F32)).astype(BF16)
        ss = (jnp.dot(hi, gsum_ref[...], preferred_element_type=F32)
              + jnp.dot(lo, gsum_ref[...], preferred_element_type=F32))
        y = a * lax.rsqrt(ss * (1.0 / DIFF_QK_DIM) + EPS)
        o_ref[:, sl] = (y * gqk_ref[:, sl]).astype(o_ref.dtype)


def _in_proj(x2, g_mix, w_in_bf, col0, n_out, gqk=None, gsum=None):
    t, d = x2.shape
    tm, tn = min(IN_PROJ_TM, t), IN_PROJ_TN
    qk_norm = gqk is not None
    jb0 = col0 // tn
    in_specs = [
        pl.BlockSpec((tm, d), lambda i, j: (i, 0)),
        pl.BlockSpec((1, d), lambda i, j: (0, 0)),
        pl.BlockSpec((d, tn), lambda i, j: (0, jb0 + j)),
    ]
    args = [x2, g_mix, w_in_bf]
    if qk_norm:
        in_specs += [pl.BlockSpec((1, tn), lambda i, j: (0, j)),
                     pl.BlockSpec((MXU_WIDTH, MXU_WIDTH), lambda i, j: (0, 0))]
        args += [gqk, gsum]
    return pl.pallas_call(
        functools.partial(_in_proj_kernel, qk_norm=qk_norm),
        grid=(t // tm, n_out // tn),
        in_specs=in_specs,
        out_specs=pl.BlockSpec((tm, tn), lambda i, j: (i, j)),
        out_shape=jax.ShapeDtypeStruct((t, n_out), BF16),
        scratch_shapes=[pltpu.VMEM((tm, d), BF16)],
        compiler_params=_cparams(("parallel", "arbitrary")),
        name="in_proj_qk" if qk_norm else "in_proj_vbcu",
    )(*args)


def _attn_kernel(lam_ref, q_ref, k_ref, v_ref, bias_ref, gsub_ref, o_ref,
                 qs_ref, m_ref, acc_ref, s0_ref, s1_ref, p0_ref, p1_ref, a0_ref, a1_ref, *, tq, tk):
    qi = pl.program_id(2)
    dv = DIFF_V_DIM
    s_bufs, p_bufs, a_bufs = (s0_ref, s1_ref), (p0_ref, p1_ref), (a0_ref, a1_ref)

    q = q_ref[0]
    lane = lax.broadcasted_iota(jnp.int32, q.shape, 1)
    zero = jnp.zeros_like(q)
    qs_ref[0:tq, :] = jnp.where(lane < DIFF_QK_DIM, q, zero)
    qs_ref[tq:, :] = jnp.where(lane < DIFF_QK_DIM, zero, q)
    acc_ref[...] = jnp.zeros_like(acc_ref)
    m_ref[...] = jnp.full_like(m_ref, MASK_VALUE)
    p1_ref[...] = jnp.zeros_like(p1_ref)
    a1_ref[...] = jnp.zeros_like(a1_ref)
    ones = jnp.ones((tk, dv), BF16)

    def block_start(step):
        return pl.multiple_of(jnp.maximum(step, 0) * tk, tk)

    def logits_stage(step, dst):
        dst[...] = lax.dot_general(qs_ref[...], k_ref[0, pl.ds(block_start(step), tk), :],
                                   (((1,), (1,)), ((), ())), preferred_element_type=F32)

    def pv_stage(step, buf):
        vx = jnp.concatenate([v_ref[0, pl.ds(block_start(step), tk), :], ones], axis=1)
        alpha = a_bufs[buf][...]
        acc_ref[...] = (jnp.concatenate([alpha, alpha], axis=1) * acc_ref[...]
                        + jnp.dot(p_bufs[buf][...], vx, preferred_element_type=F32))

    def softmax_stage(buf, bias_tile):
        s = s_bufs[buf][...]
        if bias_tile is not None:
            bias = bias_ref[0, bias_tile]
            s = jnp.concatenate([s[:tq] + bias, s[tq:] + bias], axis=0)
        m_prev = m_ref[...]
        m_new = jnp.maximum(m_prev, jnp.max(s, axis=-1, keepdims=True))
        a_bufs[buf][...] = jnp.exp2(m_prev - m_new)
        m_ref[...] = m_new
        p = jnp.exp2(s - jnp.concatenate([m_new] * (tk // LANES), axis=1))
        p_bufs[buf][...] = p.astype(BF16)

    def pair(step0, bias0, bias1, lookahead=True):
        logits_stage(step0 + 1, s_bufs[1])
        pv_stage(step0 - 1, 1)
        softmax_stage(0, bias0)
        if lookahead:
            logits_stage(step0 + 2, s_bufs[0])
        pv_stage(step0, 0)
        softmax_stage(1, bias1)

    logits_stage(0, s_bufs[0])

    def far_pair(jj, carry):
        pair(2 * jj, None, None)
        return carry

    lax.fori_loop(0, jnp.maximum(qi - 1, 0), far_pair, 0)

    @pl.when(qi > 0)
    def _():
        pair(2 * qi - 2, None, 2)

    pair(2 * qi, 0, 1, lookahead=False)
    pv_stage(2 * qi + 1, 1)

    acc = acc_ref[...]
    o = (acc[:tq, :dv] / acc[:tq, dv:]) - lam_ref[0] * (acc[tq:, :dv] / acc[tq:, dv:])
    ms = jnp.mean(o * o, axis=-1, keepdims=True)
    o_ref[0] = (o * lax.rsqrt(ms + EPS) * gsub_ref[...]).astype(o_ref.dtype)


def _attention(qk3, vbcu3, bias_tiles, gsub, lam):
    b, s, _ = qk3.shape
    h = N_DIFF_HEADS
    tq, tk = ATTN_TQ, ATTN_TK
    assert tq == 2 * tk and s % tq == 0
    kern = functools.partial(_attn_kernel, tq=tq, tk=tk)
    return pl.pallas_call(
        kern,
        grid=(b, h, s // tq),
        in_specs=[
            pl.BlockSpec(memory_space=pltpu.SMEM),
            pl.BlockSpec((1, tq, LANES), lambda bi, hi, qi: (bi, qi, hi)),
            pl.BlockSpec((1, s, LANES), lambda bi, hi, qi: (bi, 0, h + hi)),
            pl.BlockSpec((1, s, LANES), lambda bi, hi, qi: (bi, 0, hi)),
            pl.BlockSpec((1, 3, tq, tk), lambda bi, hi, qi: (hi, 0, 0, 0)),
            pl.BlockSpec((1, LANES), lambda bi, hi, qi: (0, 0)),
        ],
        out_specs=pl.BlockSpec((1, tq, LANES), lambda bi, hi, qi: (bi, qi, hi)),
        out_shape=jax.ShapeDtypeStruct((b, s, h * DIFF_V_DIM), BF16),
        scratch_shapes=[pltpu.VMEM((2 * tq, LANES), BF16),
                        pltpu.VMEM((2 * tq, LANES), F32),
                        pltpu.VMEM((2 * tq, 2 * DIFF_V_DIM), F32),
                        pltpu.VMEM((2 * tq, tk), F32),
                        pltpu.VMEM((2 * tq, tk), F32),
                        pltpu.VMEM((2 * tq, tk), BF16),
                        pltpu.VMEM((2 * tq, tk), BF16),
                        pltpu.VMEM((2 * tq, LANES), F32),
                        pltpu.VMEM((2 * tq, LANES), F32)],
        compiler_params=_cparams(("parallel", "parallel", "arbitrary")),
        name="diff_attention",
    )(lam, qk3, qk3, vbcu3, bias_tiles, gsub)


def _rel_bucket(rel):
    nb = NUM_BUCKETS // 2
    max_exact = nb // 2
    n = jnp.abs(rel)
    n2 = n * n
    large = max_exact + sum((n2 >= (max_exact * max_exact) * (2 ** k)).astype(jnp.int32)
                            for k in range(1, nb - max_exact))
    return jnp.where(rel > 0, nb, 0) + jnp.where(n < max_exact, n, large)


def _bias_kernel(rb_ref, o_ref, *, tk):
    h = pl.program_id(0)
    tq = o_ref.shape[2]
    qpos = lax.broadcasted_iota(jnp.int32, (tq, tk), 0)
    kcol = lax.broadcasted_iota(jnp.int32, (tq, tk), 1)
    chunk_shift = CHUNK.bit_length() - 1
    far_bias = rb_ref[NUM_BUCKETS // 2 - 1, h]
    for tile, offset in enumerate((0, tk, -tk)):
        kpos = kcol + offset
        bucket = _rel_bucket(kpos - qpos)
        bias = jnp.zeros((tq, tk), F32)
        for b in range(NUM_BUCKETS):
            bias = jnp.where(bucket == b, rb_ref[b, h], bias)
        mask = (kpos >> chunk_shift) <= (qpos >> chunk_shift)
        o_ref[0, tile] = jnp.where(mask, (bias - far_bias) * LOG2E, MASK_VALUE)


def _bias_tiles(rel_bias, tq, tk):
    assert tk >= MAX_DISTANCE
    assert CHUNK & (CHUNK - 1) == 0
    n_heads = rel_bias.shape[1]
    return pl.pallas_call(
        functools.partial(_bias_kernel, tk=tk),
        grid=(n_heads,),
        in_specs=[pl.BlockSpec(memory_space=pltpu.SMEM)],
        out_specs=pl.BlockSpec((1, 3, tq, tk), lambda h: (h, 0, 0, 0)),
        out_shape=jax.ShapeDtypeStruct((n_heads, 3, tq, tk), F32),
        compiler_params=_cparams(("parallel",)),
        name="bias_tiles",
    )(rel_bias)


def _conv_kernel(b_ref, c_ref, u_ref, cp_ref, up_ref, w_ref, cb_ref, g_ref, o_ref, buf_ref):
    si = pl.program_id(1)
    ts = o_ref.shape[1]
    cu = c_ref[0].astype(F32) * u_ref[0].astype(F32)
    prev = cp_ref[0].astype(F32) * up_ref[0].astype(F32)
    buf_ref[0:SUBLANES, :] = jnp.where(si > 0, prev, 0.0)
    buf_ref[SUBLANES:, :] = cu
    conv = (w_ref[0:1, :] * buf_ref[pl.ds(SUBLANES - 2, ts), :]
            + w_ref[1:2, :] * buf_ref[pl.ds(SUBLANES - 1, ts), :]
            + w_ref[2:3, :] * cu)
    z = b_ref[0].astype(F32) * (conv + cb_ref[...])
    for c in range(z.shape[1] // CONV_GROUP_DIM):
        sl = slice(c * CONV_GROUP_DIM, (c + 1) * CONV_GROUP_DIM)
        zc = z[:, sl]
        ms = jnp.mean(zc * zc, axis=-1, keepdims=True)
        o_ref[0, :, sl] = (zc * lax.rsqrt(ms + EPS) * g_ref[:, sl]).astype(o_ref.dtype)


def _short_conv(proj3, conv_w, conv_b, g_conv):
    b, s, n = proj3.shape
    dc = conv_w.shape[1]
    ts = min(CONV_TS, s)
    col0 = (n - 3 * dc) // dc
    halo = ts // SUBLANES

    def main(col):
        return pl.BlockSpec((1, ts, dc), lambda bi, si: (bi, si, col))

    def prev(col):
        return pl.BlockSpec((1, SUBLANES, dc),
                            lambda bi, si: (bi, jnp.maximum(si * halo - 1, 0), col))

    return pl.pallas_call(
        _conv_kernel,
        grid=(b, s // ts),
        in_specs=[main(col0), main(col0 + 1), main(col0 + 2), prev(col0 + 1), prev(col0 + 2),
                  pl.BlockSpec((CONV_WIDTH, dc), lambda bi, si: (0, 0)),
                  pl.BlockSpec((1, dc), lambda bi, si: (0, 0)),
                  pl.BlockSpec((1, dc), lambda bi, si: (0, 0))],
        out_specs=pl.BlockSpec((1, ts, dc), lambda bi, si: (bi, si, 0)),
        out_shape=jax.ShapeDtypeStruct((b, s, dc), BF16),
        scratch_shapes=[pltpu.VMEM((ts + SUBLANES, dc), F32)],
        compiler_params=_cparams(("parallel", "parallel")),
        name="short_conv",
    )(proj3, proj3, proj3, proj3, proj3, conv_w, conv_b, g_conv)


def _out_proj_kernel(x_ref, a_ref, c_ref, wa_ref, wc_ref, g_ref, wr_ref, br_ref, tri_ref,
                     h_ref, hn_ref, mi_ref, wcol_ref, cnt_ref, carry_ref):
    i = pl.program_id(0)
    tm = x_ref.shape[0]

    @pl.when(i == 0)
    def _():
        carry_ref[...] = jnp.zeros_like(carry_ref)

    h = (x_ref[...]
         + jnp.dot(a_ref[...], wa_ref[...], preferred_element_type=F32)
         + jnp.dot(c_ref[...], wc_ref[...], preferred_element_type=F32))
    h_ref[...] = h
    ms = jnp.mean(h * h, axis=-1, keepdims=True)
    hn = h * lax.rsqrt(ms + EPS) * g_ref[...]
    hn_hi = hn.astype(BF16)
    _store_routed_rows(hn_ref, hn)
    hn_lo = (hn - hn_hi.astype(F32)).astype(BF16)

    r_hi = jnp.dot(hn_hi, wr_ref[...], preferred_element_type=F32)
    r_lo = jnp.dot(hn_lo, wr_ref[...], preferred_element_type=F32)
    logits = (r_hi[:, :ROUTER_COLS] + r_hi[:, ROUTER_COLS:] + r_lo[:, :ROUTER_COLS]
              + br_ref[...])
    lt = logits.T

    e = EXPERTS_PER_GROUP
    row = lax.broadcasted_iota(jnp.int32, (e, tm), 0)

    def first_argmax(v):
        vmax = jnp.max(v, axis=0, keepdims=True)
        idx = jnp.min(jnp.where(v == vmax, row, e), axis=0, keepdims=True)
        return vmax, idx

    gl = lt[0:e]
    gmax, g = first_argmax(gl)
    p_g = 1.0 / jnp.sum(jnp.exp(gl - gmax), axis=0, keepdims=True)
    ing = lt[e:2 * e]
    for gi in range(1, N_GROUPS):
        ing = jnp.where(g == gi, lt[(gi + 1) * e:(gi + 2) * e], ing)
    v1, i1 = first_argmax(ing)
    v2, i2 = first_argmax(jnp.where(row == i1, -jnp.inf, ing))
    ex = jnp.exp(v2 - v1)
    w1 = p_g / (1.0 + ex)
    w2 = w1 * ex
    e1 = g * e + i1
    e2 = g * e + i2

    erow = lax.broadcasted_iota(jnp.int32, (N_EXPERTS, tm), 0)
    oh1 = (erow == e1).astype(F32)
    oh2 = (erow == e2).astype(F32)
    oh = jnp.concatenate([oh1, oh2], axis=0).astype(BF16)
    pre = jnp.dot(oh, tri_ref[...], preferred_element_type=F32)
    cnt1 = jnp.sum(oh1, axis=1, keepdims=True)
    cnt2 = jnp.sum(oh2, axis=1, keepdims=True)
    carry = carry_ref[:, 0:1]
    r1 = jnp.sum(oh1 * (pre[:N_EXPERTS] + carry), axis=0, keepdims=True)
    r2 = jnp.sum(oh2 * (pre[N_EXPERTS:] + carry + cnt1), axis=0, keepdims=True)
    new_carry = carry + cnt1 + cnt2
    carry_ref[...] = jnp.broadcast_to(new_carry, carry_ref.shape)
    cnt_ref[...] = jnp.broadcast_to(new_carry, cnt_ref.shape)

    mi_ref[0] = jnp.concatenate(
        [e1, e2, r1.astype(jnp.int32), r2.astype(jnp.int32),
         jnp.zeros((SUBLANES - 4, tm), jnp.int32)], axis=0)
    wrow = jnp.concatenate([w1, w2, jnp.zeros((ROUTER_COLS - 2, tm), F32)], axis=0)
    wcol_ref[...] = wrow.T


def _out_proj(x2, attn2, conv2, wo_a, wo_c, g_ffn, wr, br, tri):
    t, d = x2.shape
    tm = min(OUT_PROJ_TM, t)
    nt = t // tm
    da, dc = attn2.shape[1], conv2.shape[1]
    slabs = d // (2 * LANES)
    const = lambda i: (0, 0)
    return pl.pallas_call(
        _out_proj_kernel,
        grid=(nt,),
        in_specs=[
            pl.BlockSpec((tm, d), lambda i: (i, 0)),
            pl.BlockSpec((tm, da), lambda i: (i, 0)),
            pl.BlockSpec((tm, dc), lambda i: (i, 0)),
            pl.BlockSpec((da, d), const),
            pl.BlockSpec((dc, d), const),
            pl.BlockSpec((1, d), const),
            pl.BlockSpec((d, 2 * ROUTER_COLS), const),
            pl.BlockSpec((1, ROUTER_COLS), const),
            pl.BlockSpec((tm, tm), const),
        ],
        out_specs=[
            pl.BlockSpec((tm, d), lambda i: (i, 0)),
            pl.BlockSpec((tm * slabs, LANES), lambda i: (i, 0)),
            pl.BlockSpec((1, SUBLANES, tm), lambda i: (i, 0, 0)),
            pl.BlockSpec((tm, ROUTER_COLS), lambda i: (i, 0)),
            pl.BlockSpec((N_EXPERTS, LANES), const),
        ],
        out_shape=[
            jax.ShapeDtypeStruct((t, d), F32),
            jax.ShapeDtypeStruct((t * slabs, LANES), jnp.uint32),
            jax.ShapeDtypeStruct((nt, SUBLANES, tm), jnp.int32),
            jax.ShapeDtypeStruct((t, ROUTER_COLS), F32),
            jax.ShapeDtypeStruct((N_EXPERTS, LANES), F32),
        ],
        scratch_shapes=[pltpu.VMEM((N_EXPERTS, LANES), F32)],
        compiler_params=_cparams(("arbitrary",)),
        name="out_proj_router",
    )(x2, attn2, conv2, wo_a, wo_c, g_ffn, wr, br, tri)


def _moe_kernel(te_ref, nv_ref, x_ref, w1_ref, w3_ref, w2_ref, o_ref, w13_ref, w2b_ref):
    i = pl.program_id(0)
    dff = w2_ref.shape[2]

    @pl.when(i < nv_ref[0])
    def _():
        @pl.when(jnp.logical_or(i == 0, te_ref[i] != te_ref[jnp.maximum(i - 1, 0)]))
        def _():
            w13_ref[:, :dff] = w1_ref[0, 0].astype(BF16)
            w13_ref[:, dff:] = w3_ref[0, 0].astype(BF16)
            w2b_ref[...] = w2_ref[0, 0].astype(BF16)

        x = _load_routed_rows(x_ref, MOE_TM, BF16)
        ab = jnp.dot(x, w13_ref[...], preferred_element_type=F32)
        a, b = ab[:, :dff], ab[:, dff:]
        hid = (a * jax.nn.sigmoid(a) * b).astype(BF16)
        _store_routed_rows(o_ref, jnp.dot(hid, w2b_ref[...], preferred_element_type=F32))

    @pl.when(i >= nv_ref[0])
    def _():
        o_ref[...] = jnp.zeros_like(o_ref)


def _moe(xs, w1, w3, w2, layer, tile_expert, n_valid):
    d, dff = w2.shape[3], w2.shape[2]
    slabs = d // (2 * LANES)
    rows = MOE_TM * slabs
    nt = xs.shape[0] // rows
    grid_spec = pltpu.PrefetchScalarGridSpec(
        num_scalar_prefetch=2,
        grid=(nt,),
        in_specs=[
            pl.BlockSpec((rows, LANES), lambda i, te, nv: (jnp.minimum(i, nv[0] - 1), 0)),
            pl.BlockSpec((1, 1, d, dff), lambda i, te, nv: (layer, te[i], 0, 0)),
            pl.BlockSpec((1, 1, d, dff), lambda i, te, nv: (layer, te[i], 0, 0)),
            pl.BlockSpec((1, 1, dff, d), lambda i, te, nv: (layer, te[i], 0, 0)),
        ],
        out_specs=pl.BlockSpec((rows, LANES), lambda i, te, nv: (i, 0)),
        scratch_shapes=[pltpu.VMEM((d, 2 * dff), BF16), pltpu.VMEM((dff, d), BF16)],
    )
    return pl.pallas_call(
        _moe_kernel,
        grid_spec=grid_spec,
        out_shape=jax.ShapeDtypeStruct(xs.shape, jnp.uint32),
        compiler_params=_cparams(("arbitrary",)),
        name="moe_grouped",
    )(tile_expert, n_valid, xs, w1, w3, w2)


def _ple_kernel(h_ref, y1_ref, y2_ref, wcol_ref, p_ref, g_ref, wg_ref, wp_ref, o_ref):
    wcol = wcol_ref[...]
    tm = h_ref.shape[0]
    h = (h_ref[...]
         + wcol[:, 0:1] * _load_routed_rows(y1_ref.at[0], tm, F32)
         + wcol[:, 1:2] * _load_routed_rows(y2_ref.at[0], tm, F32))
    ms = jnp.mean(h * h, axis=-1, keepdims=True)
    hn = (h * lax.rsqrt(ms + EPS) * g_ref[...]).astype(BF16)
    gate = jax.nn.sigmoid(jnp.dot(hn, wg_ref[...], preferred_element_type=F32))
    emb = jnp.dot(p_ref[...].astype(BF16), wp_ref[...], preferred_element_type=F32)
    o_ref[...] = h + gate * emb


def _ple(h1, yg, wcol, p2, g_ple, wg, wp):
    t, d = h1.shape
    tm = min(PLE_TM, t)
    dp = p2.shape[1]
    slabs = d // (2 * LANES)
    const = lambda i: (0, 0)
    row = lambda i: (i, 0)
    return pl.pallas_call(
        _ple_kernel,
        grid=(t // tm,),
        in_specs=[
            pl.BlockSpec((tm, d), row),
            pl.BlockSpec((1, tm * slabs, LANES), lambda i: (0, i, 0)),
            pl.BlockSpec((1, tm * slabs, LANES), lambda i: (1, i, 0)),
            pl.BlockSpec((tm, ROUTER_COLS), row),
            pl.BlockSpec((tm, dp), row),
            pl.BlockSpec((1, d), const),
            pl.BlockSpec((d, d), const),
            pl.BlockSpec((dp, d), const),
        ],
        out_specs=pl.BlockSpec((tm, d), row),
        out_shape=jax.ShapeDtypeStruct((t, d), F32),
        compiler_params=_cparams(("parallel",)),
        name="combine_ple",
    )(h1, yg, yg, wcol, p2, g_ple, wg, wp)


def kernel(x, p, rel_bias, g_mix, w_in, g_q, g_k, lam_q1, lam_k1, lam_q2, lam_k2, g_subln,
           conv_w, conv_b, g_conv, w_o, g_ffn, w_group, b_group, w_expert, b_expert,
           w1, w3, w2, g_ple, w_ple_gate, w_ple_proj):
    depth = g_mix.shape[0]
    assert depth == 1
    li = 0
    b, s, d = x.shape
    t = b * s
    d_attn = N_DIFF_HEADS * DIFF_V_DIM
    x2 = x.reshape(t, d)

    n_groups_qk = d_attn // DIFF_QK_DIM
    gqk = jnp.concatenate([jnp.tile(g_q[li] * (DIFF_QK_DIM ** -0.5 * LOG2E), n_groups_qk),
                           jnp.tile(g_k[li], n_groups_qk)])[None, :].astype(F32)
    blk = jnp.arange(MXU_WIDTH) // DIFF_QK_DIM
    gsum = (blk[:, None] == blk[None, :]).astype(BF16)
    lam = (jnp.exp(jnp.sum(lam_q1[li] * lam_k1[li])) - jnp.exp(jnp.sum(lam_q2[li] * lam_k2[li]))
           + LAM_INIT).reshape(1).astype(F32)
    gsub = (g_subln[li] * (1.0 - LAM_INIT))[None, :].astype(F32)
    bias_tiles = _bias_tiles(rel_bias, ATTN_TQ, ATTN_TK)

    pad_g = EXPERTS_PER_GROUP - N_GROUPS
    pad_e = ROUTER_COLS - EXPERTS_PER_GROUP - N_EXPERTS
    wr_f32 = jnp.concatenate([w_group[li], jnp.zeros((d, pad_g), F32),
                              w_expert[li], jnp.zeros((d, pad_e), F32)], axis=1)
    wr_hi = wr_f32.astype(BF16)
    wr_lo = (wr_f32 - wr_hi.astype(F32)).astype(BF16)
    wr = jnp.concatenate([wr_hi, wr_lo], axis=1)
    br = jnp.concatenate([b_group[li], jnp.full((pad_g,), MASK_VALUE, F32),
                          b_expert[li], jnp.zeros((pad_e,), F32)])[None, :]
    tm_r = min(OUT_PROJ_TM, t)
    ar = jnp.arange(tm_r)
    tri = (ar[:, None] < ar[None, :]).astype(BF16)

    w_in_bf = w_in[li].astype(BF16)
    wo_bf = w_o[li].astype(BF16)
    wg_bf = w_ple_gate[li].astype(BF16)
    wp_bf = w_ple_proj[li].astype(BF16)

    n_qk = 2 * d_attn
    qk = _in_proj(x2, g_mix[li][None, :], w_in_bf, 0, n_qk, gqk, gsum)
    vbcu = _in_proj(x2, g_mix[li][None, :], w_in_bf, n_qk, w_in_bf.shape[1] - n_qk)
    vbcu3 = vbcu.reshape(b, s, -1)
    attn = _attention(qk.reshape(b, s, -1), vbcu3, bias_tiles, gsub, lam)
    conv = _short_conv(vbcu3, conv_w[li], conv_b[li][None, :], g_conv[li][None, :])
    h1, hn, meta_i, wcol, counts = _out_proj(
        x2, attn.reshape(t, -1), conv.reshape(t, -1), wo_bf[:d_attn], wo_bf[d_attn:],
        g_ffn[li][None, :], wr, br, tri)

    eid = jnp.transpose(meta_i[:, 0:2, :], (1, 0, 2)).reshape(2, t)
    rank = jnp.transpose(meta_i[:, 2:4, :], (1, 0, 2)).reshape(2, t)
    cnt = counts[:, 0].astype(jnp.int32)
    tiles_per = (cnt + MOE_TM - 1) // MOE_TM
    experts = jnp.arange(N_EXPERTS, dtype=jnp.int32)
    tile_end = jnp.sum(jnp.where(experts[:, None] <= experts[None, :], tiles_per[:, None], 0),
                       axis=0)
    row_start = (tile_end - tiles_per) * MOE_TM
    n_tiles = (TOP_K * t) // MOE_TM + N_EXPERTS
    pos = rank + jnp.sum(jnp.where(eid[..., None] == experts, row_start, 0), axis=-1)
    tile_expert = jnp.minimum(
        jnp.sum((jnp.arange(n_tiles, dtype=jnp.int32)[:, None] >= tile_end[None, :])
                .astype(jnp.int32), axis=1),
        N_EXPERTS - 1)
    n_valid = tile_end[-1:]

    slabs = d // (2 * LANES)
    row_idx = (pos[:, :, None] * slabs + jnp.arange(slabs, dtype=jnp.int32)).reshape(1, -1)

    xs = _sc_scatter_rows(hn, row_idx, n_tiles * MOE_TM * slabs)
    ys = _moe(xs, w1, w3, w2, li, tile_expert, n_valid)
    yg = _sc_gather_rows(ys, row_idx).reshape(TOP_K, t * slabs, LANES)

    out = _ple(h1, yg, wcol, p[li].reshape(t, -1), g_ple[li][None, :], wg_bf, wp_bf)
    return out.reshape(b, s, d)
```

```python
import functools
import math

import jax
import jax.numpy as jnp
from jax import lax
from jax.experimental import pallas as pl
from jax.experimental.pallas import tpu as pltpu
from jax.experimental.pallas import tpu_sc as plsc

F32 = jnp.float32
BF16 = jnp.bfloat16

N_DIFF_HEADS = 8
DIFF_QK_DIM = 64
DIFF_V_DIM = 128
CHUNK = 64
NUM_BUCKETS = 32
MAX_DISTANCE = 128
CONV_WIDTH = 3
CONV_GROUP_DIM = 128
N_GROUPS = 4
EXPERTS_PER_GROUP = 8
N_EXPERTS = N_GROUPS * EXPERTS_PER_GROUP
TOP_K = 2
EPS = 1e-6
MASK_VALUE = -1e30
LAM_INIT = 0.8 - 0.6 * math.exp(-0.3 * 0)
LOG2E = math.log2(math.e)

LANES = 128
SUBLANES = 8
MXU_WIDTH = 256
VMEM_LIMIT_BYTES = 56 * 1024 * 1024

IN_PROJ_TM = 1024
IN_PROJ_TN = 512
ATTN_TQ = 512
ATTN_TK = 256
CONV_TS = 512
OUT_PROJ_TM = 512
MOE_TM = 256
PLE_TM = 512
ROUTER_COLS = 128
SC_WINDOW = 128
SC_NUM_CORES = 2
SC_NUM_SUBCORES = 16


def _cparams(semantics):
    return pltpu.CompilerParams(dimension_semantics=semantics,
                                vmem_limit_bytes=VMEM_LIMIT_BYTES)


def _store_routed_rows(o_ref, x):
    tm, d = x.shape
    half = d // 2
    slabs = half // LANES
    xr = x.astype(BF16).astype(F32)
    lo = lax.bitcast_convert_type(xr[:, :half], jnp.uint32)
    hi = lax.bitcast_convert_type(xr[:, half:], jnp.uint32)
    packed = (lo >> 16) | (hi & jnp.uint32(0xFFFF0000))
    for r in range(slabs):
        o_ref[pl.ds(r, tm, stride=slabs), :] = packed[:, r * LANES:(r + 1) * LANES]


def _load_routed_rows(x_ref, tm, dtype):
    slabs = x_ref.shape[0] // tm
    parts = [x_ref[pl.ds(r, tm, stride=slabs), :] for r in range(slabs)]
    lo = [lax.bitcast_convert_type(w << 16, F32).astype(dtype) for w in parts]
    hi = [lax.bitcast_convert_type(w & jnp.uint32(0xFFFF0000), F32).astype(dtype) for w in parts]
    return jnp.concatenate(lo + hi, axis=1)


def _sc_mesh():
    return plsc.VectorSubcoreMesh(core_axis_name="core", subcore_axis_name="subcore",
                                  num_cores=SC_NUM_CORES, num_subcores=SC_NUM_SUBCORES)


def _sc_scatter_rows(rows, idx, n_out_rows):
    n_src, n_idx = rows.shape[0], idx.shape[1]
    src_windows = n_src // SC_WINDOW

    @functools.partial(pl.kernel, mesh=_sc_mesh(), scratch_types=[],
                       out_type=jax.ShapeDtypeStruct((n_out_rows, LANES), rows.dtype))
    def scatter(x_hbm, i_hbm, o_hbm):
        def body(x_vmem, i_vmem):
            pltpu.sync_copy(x_vmem, o_hbm.at[i_vmem.at[0]])

        pltpu.emit_pipeline(
            body,
            grid=(n_idx // SC_WINDOW,),
            in_specs=[pl.BlockSpec((SC_WINDOW, LANES), lambda i: (lax.rem(i, src_windows), 0)),
                      pl.BlockSpec((1, SC_WINDOW), lambda i: (0, i))],
            out_specs=[],
            core_axis_name=("core", "subcore"),
            dimension_semantics=(pltpu.PARALLEL,),
        )(x_hbm, i_hbm)

    return scatter(rows, idx)


def _sc_gather_rows(table, idx):
    n_idx = idx.shape[1]

    @functools.partial(pl.kernel, mesh=_sc_mesh(), scratch_types=[],
                       out_type=jax.ShapeDtypeStruct((n_idx, LANES), table.dtype))
    def gather(t_hbm, i_hbm, o_hbm):
        def body(i_vmem, o_vmem):
            pltpu.sync_copy(t_hbm.at[i_vmem.at[0]], o_vmem)

        pltpu.emit_pipeline(
            body,
            grid=(n_idx // SC_WINDOW,),
            in_specs=[pl.BlockSpec((1, SC_WINDOW), lambda i: (0, i))],
            out_specs=[pl.BlockSpec((SC_WINDOW, LANES), lambda i: (i, 0))],
            core_axis_name=("core", "subcore"),
            dimension_semantics=(pltpu.PARALLEL,),
        )(i_hbm, o_hbm)

    return gather(table, idx)


def _in_proj_kernel(x_ref, g_ref, w_ref, *rest, qk_norm):
    if qk_norm:
        gqk_ref, gsum_ref, o_ref, xn_ref = rest
    else:
        o_ref, xn_ref = rest

    @pl.when(pl.program_id(1) == 0)
    def _():
        x = x_ref[...]
        ms = jnp.mean(x * x, axis=-1, keepdims=True)
        xn_ref[...] = (x * lax.rsqrt(ms + EPS) * g_ref[...]).astype(BF16)

    acc = jnp.dot(xn_ref[...], w_ref[...], preferred_element_type=F32)
    if not qk_norm:
        o_ref[...] = acc.astype(o_ref.dtype)
        return
    for c in range(acc.shape[1] // MXU_WIDTH):
        sl = slice(c * MXU_WIDTH, (c + 1) * MXU_WIDTH)
        a = acc[:, sl]
        sq = a * a
        hi = sq.astype(BF16)
        lo = (sq - hi.astype(F32)).astype(BF16)
        ss = (jnp.dot(hi, gsum_ref[...], preferred_element_type=F32)
              + jnp.dot(lo, gsum_ref[...], preferred_element_type=F32))
        y = a * lax.rsqrt(ss * (1.0 / DIFF_QK_DIM) + EPS)
        o_ref[:, sl] = (y * gqk_ref[:, sl]).astype(o_ref.dtype)


def _in_proj(x2, g_mix, w_in_bf, col0, n_out, gqk=None, gsum=None):
    t, d = x2.shape
    tm, tn = min(IN_PROJ_TM, t), IN_PROJ_TN
    qk_norm = gqk is not None
    jb0 = col0 // tn
    in_specs = [
        pl.BlockSpec((tm, d), lambda i, j: (i, 0)),
        pl.BlockSpec((1, d), lambda i, j: (0, 0)),
        pl.BlockSpec((d, tn), lambda i, j: (0, jb0 + j)),
    ]
    args = [x2, g_mix, w_in_bf]
    if qk_norm:
        in_specs += [pl.BlockSpec((1, tn), lambda i, j: (0, j)),
                     pl.BlockSpec((MXU_WIDTH, MXU_WIDTH), lambda i, j: (0, 0))]
        args += [gqk, gsum]
    return pl.pallas_call(
        functools.partial(_in_proj_kernel, qk_norm=qk_norm),
        grid=(t // tm, n_out // tn),
        in_specs=in_specs,
        out_specs=pl.BlockSpec((tm, tn), lambda i, j: (i, j)),
        out_shape=jax.ShapeDtypeStruct((t, n_out), BF16),
        scratch_shapes=[pltpu.VMEM((tm, d), BF16)],
        compiler_params=_cparams(("parallel", "arbitrary")),
        name="in_proj_qk" if qk_norm else "in_proj_vbcu",
    )(*args)


def _attn_kernel(lam_ref, q_ref, k_ref, v_ref, bias_ref, gsub_ref, o_ref,
                 qs_ref, m_ref, acc_ref, s0_ref, s1_ref, p0_ref, p1_ref, a0_ref, a1_ref, *, tq, tk):
    qi = pl.program_id(2)
    dv = DIFF_V_DIM
    s_bufs, p_bufs, a_bufs = (s0_ref, s1_ref), (p0_ref, p1_ref), (a0_ref, a1_ref)

    q = q_ref[0]
    lane = lax.broadcasted_iota(jnp.int32, q.shape, 1)
    zero = jnp.zeros_like(q)
    qs_ref[0:tq, :] = jnp.where(lane < DIFF_QK_DIM, q, zero)
    qs_ref[tq:, :] = jnp.where(lane < DIFF_QK_DIM, zero, q)
    acc_ref[...] = jnp.zeros_like(acc_ref)
    m_ref[...] = jnp.full_like(m_ref, MASK_VALUE)
    p1_ref[...] = jnp.zeros_like(p1_ref)
    a1_ref[...] = jnp.zeros_like(a1_ref)
    ones = jnp.ones((tk, dv), BF16)

    def block_start(step):
        return pl.multiple_of(jnp.maximum(step, 0) * tk, tk)

    def logits_stage(step, dst):
        dst[...] = lax.dot_general(qs_ref[...], k_ref[0, pl.ds(block_start(step), tk), :],
                                   (((1,), (1,)), ((), ())), preferred_element_type=F32)

    def pv_stage(step, buf):
        vx = jnp.concatenate([v_ref[0, pl.ds(block_start(step), tk), :], ones], axis=1)
        alpha = a_bufs[buf][...]
        acc_ref[...] = (jnp.concatenate([alpha, alpha], axis=1) * acc_ref[...]
                        + jnp.dot(p_bufs[buf][...], vx, preferred_element_type=F32))

    def softmax_stage(buf, bias_tile):
        s = s_bufs[buf][...]
        if bias_tile is not None:
            bias = bias_ref[0, bias_tile]
            s = jnp.concatenate([s[:tq] + bias, s[tq:] + bias], axis=0)
        m_prev = m_ref[...]
        m_new = jnp.maximum(m_prev, jnp.max(s, axis=-1, keepdims=True))
        a_bufs[buf][...] = jnp.exp2(m_prev - m_new)
        m_ref[...] = m_new
        p = jnp.exp2(s - jnp.concatenate([m_new] * (tk // LANES), axis=1))
        p_bufs[buf][...] = p.astype(BF16)

    def pair(step0, bias0, bias1, lookahead=True):
        logits_stage(step0 + 1, s_bufs[1])
        pv_stage(step0 - 1, 1)
        softmax_stage(0, bias0)
        if lookahead:
            logits_stage(step0 + 2, s_bufs[0])
        pv_stage(step0, 0)
        softmax_stage(1, bias1)

    logits_stage(0, s_bufs[0])

    n_far_pairs = jnp.maximum(qi - 1, 0)

    def far_quad(jj, carry):
        pair(4 * jj, None, None)
        pair(4 * jj + 2, None, None)
        return carry

    lax.fori_loop(0, n_far_pairs // 2, far_quad, 0)

    @pl.when(n_far_pairs % 2 == 1)
    def _():
        pair(2 * n_far_pairs - 2, None, None)

    @pl.when(qi > 0)
    def _():
        pair(2 * qi - 2, None, 2)

    pair(2 * qi, 0, 1, lookahead=False)
    pv_stage(2 * qi + 1, 1)

    acc = acc_ref[...]
    o = (acc[:tq, :dv] / acc[:tq, dv:]) - lam_ref[0] * (acc[tq:, :dv] / acc[tq:, dv:])
    ms = jnp.mean(o * o, axis=-1, keepdims=True)
    o_ref[0] = (o * lax.rsqrt(ms + EPS) * gsub_ref[...]).astype(o_ref.dtype)


def _attention(qk3, vbcu3, bias_tiles, gsub, lam):
    b, s, _ = qk3.shape
    h = N_DIFF_HEADS
    tq, tk = ATTN_TQ, ATTN_TK
    assert tq == 2 * tk and s % tq == 0
    kern = functools.partial(_attn_kernel, tq=tq, tk=tk)
    return pl.pallas_call(
        kern,
        grid=(b, h, s // tq),
        in_specs=[
            pl.BlockSpec(memory_space=pltpu.SMEM),
            pl.BlockSpec((1, tq, LANES), lambda bi, hi, qi: (bi, qi, hi)),
            pl.BlockSpec((1, s, LANES), lambda bi, hi, qi: (bi, 0, h + hi)),
            pl.BlockSpec((1, s, LANES), lambda bi, hi, qi: (bi, 0, hi)),
            pl.BlockSpec((1, 3, tq, tk), lambda bi, hi, qi: (hi, 0, 0, 0)),
            pl.BlockSpec((1, LANES), lambda bi, hi, qi: (0, 0)),
        ],
        out_specs=pl.BlockSpec((1, tq, LANES), lambda bi, hi, qi: (bi, qi, hi)),
        out_shape=jax.ShapeDtypeStruct((b, s, h * DIFF_V_DIM), BF16),
        scratch_shapes=[pltpu.VMEM((2 * tq, LANES), BF16),
                        pltpu.VMEM((2 * tq, LANES), F32),
                        pltpu.VMEM((2 * tq, 2 * DIFF_V_DIM), F32),
                        pltpu.VMEM((2 * tq, tk), F32),
                        pltpu.VMEM((2 * tq, tk), F32),
                        pltpu.VMEM((2 * tq, tk), BF16),
                        pltpu.VMEM((2 * tq, tk), BF16),
                        pltpu.VMEM((2 * tq, LANES), F32),
                        pltpu.VMEM((2 * tq, LANES), F32)],
        compiler_params=_cparams(("parallel", "parallel", "arbitrary")),
        name="diff_attention",
    )(lam, qk3, qk3, vbcu3, bias_tiles, gsub)


def _rel_bucket(rel):
    nb = NUM_BUCKETS // 2
    max_exact = nb // 2
    n = jnp.abs(rel)
    n2 = n * n
    large = max_exact + sum((n2 >= (max_exact * max_exact) * (2 ** k)).astype(jnp.int32)
                            for k in range(1, nb - max_exact))
    return jnp.where(rel > 0, nb, 0) + jnp.where(n < max_exact, n, large)


def _bias_kernel(rb_ref, o_ref, *, tk):
    h = pl.program_id(0)
    tq = o_ref.shape[2]
    qpos = lax.broadcasted_iota(jnp.int32, (tq, tk), 0)
    kcol = lax.broadcasted_iota(jnp.int32, (tq, tk), 1)
    chunk_shift = CHUNK.bit_length() - 1
    far_bias = rb_ref[NUM_BUCKETS // 2 - 1, h]
    for tile, offset in enumerate((0, tk, -tk)):
        kpos = kcol + offset
        bucket = _rel_bucket(kpos - qpos)
        bias = jnp.zeros((tq, tk), F32)
        for b in range(NUM_BUCKETS):
            bias = jnp.where(bucket == b, rb_ref[b, h], bias)
        mask = (kpos >> chunk_shift) <= (qpos >> chunk_shift)
        o_ref[0, tile] = jnp.where(mask, (bias - far_bias) * LOG2E, MASK_VALUE)


def _bias_tiles(rel_bias, tq, tk):
    assert tk >= MAX_DISTANCE
    assert CHUNK & (CHUNK - 1) == 0
    n_heads = rel_bias.shape[1]
    return pl.pallas_call(
        functools.partial(_bias_kernel, tk=tk),
        grid=(n_heads,),
        in_specs=[pl.BlockSpec(memory_space=pltpu.SMEM)],
        out_specs=pl.BlockSpec((1, 3, tq, tk), lambda h: (h, 0, 0, 0)),
        out_shape=jax.ShapeDtypeStruct((n_heads, 3, tq, tk), F32),
        compiler_params=_cparams(("parallel",)),
        name="bias_tiles",
    )(rel_bias)


def _conv_kernel(b_ref, c_ref, u_ref, cp_ref, up_ref, w_ref, cb_ref, g_ref, o_ref, buf_ref):
    si = pl.program_id(1)
    ts = o_ref.shape[1]
    cu = c_ref[0].astype(F32) * u_ref[0].astype(F32)
    prev = cp_ref[0].astype(F32) * up_ref[0].astype(F32)
    buf_ref[0:SUBLANES, :] = jnp.where(si > 0, prev, 0.0)
    buf_ref[SUBLANES:, :] = cu
    conv = (w_ref[0:1, :] * buf_ref[pl.ds(SUBLANES - 2, ts), :]
            + w_ref[1:2, :] * buf_ref[pl.ds(SUBLANES - 1, ts), :]
            + w_ref[2:3, :] * cu)
    z = b_ref[0].astype(F32) * (conv + cb_ref[...])
    for c in range(z.shape[1] // CONV_GROUP_DIM):
        sl = slice(c * CONV_GROUP_DIM, (c + 1) * CONV_GROUP_DIM)
        zc = z[:, sl]
        ms = jnp.mean(zc * zc, axis=-1, keepdims=True)
        o_ref[0, :, sl] = (zc * lax.rsqrt(ms + EPS) * g_ref[:, sl]).astype(o_ref.dtype)


def _short_conv(proj3, conv_w, conv_b, g_conv):
    b, s, n = proj3.shape
    dc = conv_w.shape[1]
    ts = min(CONV_TS, s)
    col0 = (n - 3 * dc) // dc
    halo = ts // SUBLANES

    def main(col):
        return pl.BlockSpec((1, ts, dc), lambda bi, si: (bi, si, col))

    def prev(col):
        return pl.BlockSpec((1, SUBLANES, dc),
                            lambda bi, si: (bi, jnp.maximum(si * halo - 1, 0), col))

    return pl.pallas_call(
        _conv_kernel,
        grid=(b, s // ts),
        in_specs=[main(col0), main(col0 + 1), main(col0 + 2), prev(col0 + 1), prev(col0 + 2),
                  pl.BlockSpec((CONV_WIDTH, dc), lambda bi, si: (0, 0)),
                  pl.BlockSpec((1, dc), lambda bi, si: (0, 0)),
                  pl.BlockSpec((1, dc), lambda bi, si: (0, 0))],
        out_specs=pl.BlockSpec((1, ts, dc), lambda bi, si: (bi, si, 0)),
        out_shape=jax.ShapeDtypeStruct((b, s, dc), BF16),
        scratch_shapes=[pltpu.VMEM((ts + SUBLANES, dc), F32)],
        compiler_params=_cparams(("parallel", "parallel")),
        name="short_conv",
    )(proj3, proj3, proj3, proj3, proj3, conv_w, conv_b, g_conv)


def _out_proj_kernel(x_ref, a_ref, c_ref, wa_ref, wc_ref, g_ref, wr_ref, br_ref, tri_ref,
                     h_ref, hn_ref, mi_ref, wcol_ref, cnt_ref, carry_ref):
    i = pl.program_id(0)
    tm = x_ref.shape[0]

    @pl.when(i == 0)
    def _():
        carry_ref[...] = jnp.zeros_like(carry_ref)

    h = (x_ref[...]
         + jnp.dot(a_ref[...], wa_ref[...], preferred_element_type=F32)
         + jnp.dot(c_ref[...], wc_ref[...], preferred_element_type=F32))
    h_ref[...] = h
    ms = jnp.mean(h * h, axis=-1, keepdims=True)
    hn = h * lax.rsqrt(ms + EPS) * g_ref[...]
    hn_hi = hn.astype(BF16)
    _store_routed_rows(hn_ref, hn)
    hn_lo = (hn - hn_hi.astype(F32)).astype(BF16)

    r_hi = jnp.dot(hn_hi, wr_ref[...], preferred_element_type=F32)
    r_lo = jnp.dot(hn_lo, wr_ref[...], preferred_element_type=F32)
    logits = (r_hi[:, :ROUTER_COLS] + r_hi[:, ROUTER_COLS:] + r_lo[:, :ROUTER_COLS]
              + br_ref[...])
    lt = logits.T

    e = EXPERTS_PER_GROUP
    row = lax.broadcasted_iota(jnp.int32, (e, tm), 0)

    def first_argmax(v):
        vmax = jnp.max(v, axis=0, keepdims=True)
        idx = jnp.min(jnp.where(v == vmax, row, e), axis=0, keepdims=True)
        return vmax, idx

    gl = lt[0:e]
    gmax, g = first_argmax(gl)
    p_g = 1.0 / jnp.sum(jnp.exp(gl - gmax), axis=0, keepdims=True)
    ing = lt[e:2 * e]
    for gi in range(1, N_GROUPS):
        ing = jnp.where(g == gi, lt[(gi + 1) * e:(gi + 2) * e], ing)
    v1, i1 = first_argmax(ing)
    v2, i2 = first_argmax(jnp.where(row == i1, -jnp.inf, ing))
    ex = jnp.exp(v2 - v1)
    w1 = p_g / (1.0 + ex)
    w2 = w1 * ex
    e1 = g * e + i1
    e2 = g * e + i2

    erow = lax.broadcasted_iota(jnp.int32, (N_EXPERTS, tm), 0)
    oh1 = (erow == e1).astype(F32)
    oh2 = (erow == e2).astype(F32)
    oh = jnp.concatenate([oh1, oh2], axis=0).astype(BF16)
    pre = jnp.dot(oh, tri_ref[...], preferred_element_type=F32)
    cnt1 = jnp.sum(oh1, axis=1, keepdims=True)
    cnt2 = jnp.sum(oh2, axis=1, keepdims=True)
    carry = carry_ref[:, 0:1]
    r1 = jnp.sum(oh1 * (pre[:N_EXPERTS] + carry), axis=0, keepdims=True)
    r2 = jnp.sum(oh2 * (pre[N_EXPERTS:] + carry + cnt1), axis=0, keepdims=True)
    new_carry = carry + cnt1 + cnt2
    carry_ref[...] = jnp.broadcast_to(new_carry, carry_ref.shape)
    cnt_ref[...] = jnp.broadcast_to(new_carry, cnt_ref.shape)

    mi_ref[0] = jnp.concatenate(
        [e1, e2, r1.astype(jnp.int32), r2.astype(jnp.int32),
         jnp.zeros((SUBLANES - 4, tm), jnp.int32)], axis=0)
    wrow = jnp.concatenate([w1, w2, jnp.zeros((ROUTER_COLS - 2, tm), F32)], axis=0)
    wcol_ref[...] = wrow.T


def _out_proj(x2, attn2, conv2, wo_a, wo_c, g_ffn, wr, br, tri):
    t, d = x2.shape
    tm = min(OUT_PROJ_TM, t)
    nt = t // tm
    da, dc = attn2.shape[1], conv2.shape[1]
    slabs = d // (2 * LANES)
    const = lambda i: (0, 0)
    return pl.pallas_call(
        _out_proj_kernel,
        grid=(nt,),
        in_specs=[
            pl.BlockSpec((tm, d), lambda i: (i, 0)),
            pl.BlockSpec((tm, da), lambda i: (i, 0)),
            pl.BlockSpec((tm, dc), lambda i: (i, 0)),
            pl.BlockSpec((da, d), const),
            pl.BlockSpec((dc, d), const),
            pl.BlockSpec((1, d), const),
            pl.BlockSpec((d, 2 * ROUTER_COLS), const),
            pl.BlockSpec((1, ROUTER_COLS), const),
            pl.BlockSpec((tm, tm), const),
        ],
        out_specs=[
            pl.BlockSpec((tm, d), lambda i: (i, 0)),
            pl.BlockSpec((tm * slabs, LANES), lambda i: (i, 0)),
            pl.BlockSpec((1, SUBLANES, tm), lambda i: (i, 0, 0)),
            pl.BlockSpec((tm, ROUTER_COLS), lambda i: (i, 0)),
            pl.BlockSpec((N_EXPERTS, LANES), const),
        ],
        out_shape=[
            jax.ShapeDtypeStruct((t, d), F32),
            jax.ShapeDtypeStruct((t * slabs, LANES), jnp.uint32),
            jax.ShapeDtypeStruct((nt, SUBLANES, tm), jnp.int32),
            jax.ShapeDtypeStruct((t, ROUTER_COLS), F32),
            jax.ShapeDtypeStruct((N_EXPERTS, LANES), F32),
        ],
        scratch_shapes=[pltpu.VMEM((N_EXPERTS, LANES), F32)],
        compiler_params=_cparams(("arbitrary",)),
        name="out_proj_router",
    )(x2, attn2, conv2, wo_a, wo_c, g_ffn, wr, br, tri)


def _moe_kernel(te_ref, nv_ref, x_ref, w1_ref, w3_ref, w2_ref, o_ref, w13_ref, w2b_ref):
    i = pl.program_id(0)
    dff = w2_ref.shape[2]

    @pl.when(i < nv_ref[0])
    def _():
        @pl.when(jnp.logical_or(i == 0, te_ref[i] != te_ref[jnp.maximum(i - 1, 0)]))
        def _():
            w13_ref[:, :dff] = w1_ref[0, 0].astype(BF16)
            w13_ref[:, dff:] = w3_ref[0, 0].astype(BF16)
            w2b_ref[...] = w2_ref[0, 0].astype(BF16)

        x = _load_routed_rows(x_ref, MOE_TM, BF16)
        ab = jnp.dot(x, w13_ref[...], preferred_element_type=F32)
        a, b = ab[:, :dff], ab[:, dff:]
        hid = (a * jax.nn.sigmoid(a) * b).astype(BF16)
        _store_routed_rows(o_ref, jnp.dot(hid, w2b_ref[...], preferred_element_type=F32))

    @pl.when(i >= nv_ref[0])
    def _():
        o_ref[...] = jnp.zeros_like(o_ref)


def _moe(xs, w1, w3, w2, layer, tile_expert, n_valid):
    d, dff = w2.shape[3], w2.shape[2]
    slabs = d // (2 * LANES)
    rows = MOE_TM * slabs
    nt = xs.shape[0] // rows
    grid_spec = pltpu.PrefetchScalarGridSpec(
        num_scalar_prefetch=2,
        grid=(nt,),
        in_specs=[
            pl.BlockSpec((rows, LANES), lambda i, te, nv: (jnp.minimum(i, nv[0] - 1), 0)),
            pl.BlockSpec((1, 1, d, dff), lambda i, te, nv: (layer, te[i], 0, 0)),
            pl.BlockSpec((1, 1, d, dff), lambda i, te, nv: (layer, te[i], 0, 0)),
            pl.BlockSpec((1, 1, dff, d), lambda i, te, nv: (layer, te[i], 0, 0)),
        ],
        out_specs=pl.BlockSpec((rows, LANES), lambda i, te, nv: (i, 0)),
        scratch_shapes=[pltpu.VMEM((d, 2 * dff), BF16), pltpu.VMEM((dff, d), BF16)],
    )
    return pl.pallas_call(
        _moe_kernel,
        grid_spec=grid_spec,
        out_shape=jax.ShapeDtypeStruct(xs.shape, jnp.uint32),
        compiler_params=_cparams(("arbitrary",)),
        name="moe_grouped",
    )(tile_expert, n_valid, xs, w1, w3, w2)


def _ple_kernel(h_ref, y1_ref, y2_ref, wcol_ref, p_ref, g_ref, wg_ref, wp_ref, o_ref):
    wcol = wcol_ref[...]
    tm = h_ref.shape[0]
    h = (h_ref[...]
         + wcol[:, 0:1] * _load_routed_rows(y1_ref.at[0], tm, F32)
         + wcol[:, 1:2] * _load_routed_rows(y2_ref.at[0], tm, F32))
    ms = jnp.mean(h * h, axis=-1, keepdims=True)
    hn = (h * lax.rsqrt(ms + EPS) * g_ref[...]).astype(BF16)
    gate = jax.nn.sigmoid(jnp.dot(hn, wg_ref[...], preferred_element_type=F32))
    emb = jnp.dot(p_ref[...].astype(BF16), wp_ref[...], preferred_element_type=F32)
    o_ref[...] = h + gate * emb


def _ple(h1, yg, wcol, p2, g_ple, wg, wp):
    t, d = h1.shape
    tm = min(PLE_TM, t)
    dp = p2.shape[1]
    slabs = d // (2 * LANES)
    const = lambda i: (0, 0)
    row = lambda i: (i, 0)
    return pl.pallas_call(
        _ple_kernel,
        grid=(t // tm,),
        in_specs=[
            pl.BlockSpec((tm, d), row),
            pl.BlockSpec((1, tm * slabs, LANES), lambda i: (0, i, 0)),
            pl.BlockSpec((1, tm * slabs, LANES), lambda i: (1, i, 0)),
            pl.BlockSpec((tm, ROUTER_COLS), row),
            pl.BlockSpec((tm, dp), row),
            pl.BlockSpec((1, d), const),
            pl.BlockSpec((d, d), const),
            pl.BlockSpec((dp, d), const),
        ],
        out_specs=pl.BlockSpec((tm, d), row),
        out_shape=jax.ShapeDtypeStruct((t, d), F32),
        compiler_params=_cparams(("parallel",)),
        name="combine_ple",
    )(h1, yg, yg, wcol, p2, g_ple, wg, wp)


def kernel(x, p, rel_bias, g_mix, w_in, g_q, g_k, lam_q1, lam_k1, lam_q2, lam_k2, g_subln,
           conv_w, conv_b, g_conv, w_o, g_ffn, w_group, b_group, w_expert, b_expert,
           w1, w3, w2, g_ple, w_ple_gate, w_ple_proj):
    depth = g_mix.shape[0]
    assert depth == 1
    li = 0
    b, s, d = x.shape
    t = b * s
    d_attn = N_DIFF_HEADS * DIFF_V_DIM
    x2 = x.reshape(t, d)

    n_groups_qk = d_attn // DIFF_QK_DIM
    gqk = jnp.concatenate([jnp.tile(g_q[li] * (DIFF_QK_DIM ** -0.5 * LOG2E), n_groups_qk),
                           jnp.tile(g_k[li], n_groups_qk)])[None, :].astype(F32)
    blk = jnp.arange(MXU_WIDTH) // DIFF_QK_DIM
    gsum = (blk[:, None] == blk[None, :]).astype(BF16)
    lam = (jnp.exp(jnp.sum(lam_q1[li] * lam_k1[li])) - jnp.exp(jnp.sum(lam_q2[li] * lam_k2[li]))
           + LAM_INIT).reshape(1).astype(F32)
    gsub = (g_subln[li] * (1.0 - LAM_INIT))[None, :].astype(F32)
    bias_tiles = _bias_tiles(rel_bias, ATTN_TQ, ATTN_TK)

    pad_g = EXPERTS_PER_GROUP - N_GROUPS
    pad_e = ROUTER_COLS - EXPERTS_PER_GROUP - N_EXPERTS
    wr_f32 = jnp.concatenate([w_group[li], jnp.zeros((d, pad_g), F32),
                              w_expert[li], jnp.zeros((d, pad_e), F32)], axis=1)
    wr_hi = wr_f32.astype(BF16)
    wr_lo = (wr_f32 - wr_hi.astype(F32)).astype(BF16)
    wr = jnp.concatenate([wr_hi, wr_lo], axis=1)
    br = jnp.concatenate([b_group[li], jnp.full((pad_g,), MASK_VALUE, F32),
                          b_expert[li], jnp.zeros((pad_e,), F32)])[None, :]
    tm_r = min(OUT_PROJ_TM, t)
    ar = jnp.arange(tm_r)
    tri = (ar[:, None] < ar[None, :]).astype(BF16)

    w_in_bf = w_in[li].astype(BF16)
    wo_bf = w_o[li].astype(BF16)
    wg_bf = w_ple_gate[li].astype(BF16)
    wp_bf = w_ple_proj[li].astype(BF16)

    n_qk = 2 * d_attn
    qk = _in_proj(x2, g_mix[li][None, :], w_in_bf, 0, n_qk, gqk, gsum)
    vbcu = _in_proj(x2, g_mix[li][None, :], w_in_bf, n_qk, w_in_bf.shape[1] - n_qk)
    vbcu3 = vbcu.reshape(b, s, -1)
    attn = _attention(qk.reshape(b, s, -1), vbcu3, bias_tiles, gsub, lam)
    conv = _short_conv(vbcu3, conv_w[li], conv_b[li][None, :], g_conv[li][None, :])
    h1, hn, meta_i, wcol, counts = _out_proj(
        x2, attn.reshape(t, -1), conv.reshape(t, -1), wo_bf[:d_attn], wo_bf[d_attn:],
        g_ffn[li][None, :], wr, br, tri)

    eid = jnp.transpose(meta_i[:, 0:2, :], (1, 0, 2)).reshape(2, t)
    rank = jnp.transpose(meta_i[:, 2:4, :], (1, 0, 2)).reshape(2, t)
    cnt = counts[:, 0].astype(jnp.int32)
    tiles_per = (cnt + MOE_TM - 1) // MOE_TM
    experts = jnp.arange(N_EXPERTS, dtype=jnp.int32)
    tile_end = jnp.sum(jnp.where(experts[:, None] <= experts[None, :], tiles_per[:, None], 0),
                       axis=0)
    row_start = (tile_end - tiles_per) * MOE_TM
    n_tiles = (TOP_K * t) // MOE_TM + N_EXPERTS
    pos = rank + jnp.sum(jnp.where(eid[..., None] == experts, row_start, 0), axis=-1)
    tile_expert = jnp.minimum(
        jnp.sum((jnp.arange(n_tiles, dtype=jnp.int32)[:, None] >= tile_end[None, :])
                .astype(jnp.int32), axis=1),
        N_EXPERTS - 1)
    n_valid = tile_end[-1:]

    slabs = d // (2 * LANES)
    row_idx = (pos[:, :, None] * slabs + jnp.arange(slabs, dtype=jnp.int32)).reshape(1, -1)

    xs = _sc_scatter_rows(hn, row_idx, n_tiles * MOE_TM * slabs)
    ys = _moe(xs, w1, w3, w2, li, tile_expert, n_valid)
    yg = _sc_gather_rows(ys, row_idx).reshape(TOP_K, t * slabs, LANES)

    out = _ple(h1, yg, wcol, p[li].reshape(t, -1), g_ple[li][None, :], wg_bf, wp_bf)
    return out.reshape(b, s, d)
```

```python
import functools
import math

import jax
import jax.numpy as jnp
from jax import lax
from jax.experimental import pallas as pl
from jax.experimental.pallas import tpu as pltpu
from jax.experimental.pallas import tpu_sc as plsc

F32 = jnp.float32
BF16 = jnp.bfloat16

N_DIFF_HEADS = 8
DIFF_QK_DIM = 64
DIFF_V_DIM = 128
CHUNK = 64
NUM_BUCKETS = 32
MAX_DISTANCE = 128
CONV_WIDTH = 3
CONV_GROUP_DIM = 128
N_GROUPS = 4
EXPERTS_PER_GROUP = 8
N_EXPERTS = N_GROUPS * EXPERTS_PER_GROUP
TOP_K = 2
EPS = 1e-6
MASK_VALUE = -1e30
LAM_INIT = 0.8 - 0.6 * math.exp(-0.3 * 0)
LOG2E = math.log2(math.e)

LANES = 128
SUBLANES = 8
MXU_WIDTH = 256
VMEM_LIMIT_BYTES = 56 * 1024 * 1024

IN_PROJ_TM = 1024
IN_PROJ_TN = 512
ATTN_TQ = 512
ATTN_TK = 256
CONV_TS = 512
OUT_PROJ_TM = 512
MOE_TM = 256
PLE_TM = 512
ROUTER_COLS = 128
SC_WINDOW = 128
SC_NUM_CORES = 2
SC_NUM_SUBCORES = 16


def _cparams(semantics):
    return pltpu.CompilerParams(dimension_semantics=semantics,
                                vmem_limit_bytes=VMEM_LIMIT_BYTES)


def _store_routed_rows(o_ref, x):
    tm, d = x.shape
    half = d // 2
    slabs = half // LANES
    xr = x.astype(BF16).astype(F32)
    lo = lax.bitcast_convert_type(xr[:, :half], jnp.uint32)
    hi = lax.bitcast_convert_type(xr[:, half:], jnp.uint32)
    packed = (lo >> 16) | (hi & jnp.uint32(0xFFFF0000))
    for r in range(slabs):
        o_ref[pl.ds(r, tm, stride=slabs), :] = packed[:, r * LANES:(r + 1) * LANES]


def _load_routed_rows(x_ref, tm, dtype):
    slabs = x_ref.shape[0] // tm
    parts = [x_ref[pl.ds(r, tm, stride=slabs), :] for r in range(slabs)]
    lo = [lax.bitcast_convert_type(w << 16, F32).astype(dtype) for w in parts]
    hi = [lax.bitcast_convert_type(w & jnp.uint32(0xFFFF0000), F32).astype(dtype) for w in parts]
    return jnp.concatenate(lo + hi, axis=1)


def _sc_mesh():
    return plsc.VectorSubcoreMesh(core_axis_name="core", subcore_axis_name="subcore",
                                  num_cores=SC_NUM_CORES, num_subcores=SC_NUM_SUBCORES)


def _sc_scatter_rows(rows, idx, n_out_rows):
    n_src, n_idx = rows.shape[0], idx.shape[1]
    src_windows = n_src // SC_WINDOW

    @functools.partial(pl.kernel, mesh=_sc_mesh(), scratch_types=[],
                       out_type=jax.ShapeDtypeStruct((n_out_rows, LANES), rows.dtype))
    def scatter(x_hbm, i_hbm, o_hbm):
        def body(x_vmem, i_vmem):
            pltpu.sync_copy(x_vmem, o_hbm.at[i_vmem.at[0]])

        pltpu.emit_pipeline(
            body,
            grid=(n_idx // SC_WINDOW,),
            in_specs=[pl.BlockSpec((SC_WINDOW, LANES), lambda i: (lax.rem(i, src_windows), 0)),
                      pl.BlockSpec((1, SC_WINDOW), lambda i: (0, i))],
            out_specs=[],
            core_axis_name=("core", "subcore"),
            dimension_semantics=(pltpu.PARALLEL,),
        )(x_hbm, i_hbm)

    return scatter(rows, idx)


def _sc_gather_rows(table, idx):
    n_idx = idx.shape[1]

    @functools.partial(pl.kernel, mesh=_sc_mesh(), scratch_types=[],
                       out_type=jax.ShapeDtypeStruct((n_idx, LANES), table.dtype))
    def gather(t_hbm, i_hbm, o_hbm):
        def body(i_vmem, o_vmem):
            pltpu.sync_copy(t_hbm.at[i_vmem.at[0]], o_vmem)

        pltpu.emit_pipeline(
            body,
            grid=(n_idx // SC_WINDOW,),
            in_specs=[pl.BlockSpec((1, SC_WINDOW), lambda i: (0, i))],
            out_specs=[pl.BlockSpec((SC_WINDOW, LANES), lambda i: (i, 0))],
            core_axis_name=("core", "subcore"),
            dimension_semantics=(pltpu.PARALLEL,),
        )(i_hbm, o_hbm)

    return gather(table, idx)


def _in_proj_kernel(x_ref, g_ref, w_ref, *rest, qk_norm):
    if qk_norm:
        gqk_ref, gsum_ref, o_ref, xn_ref = rest
    else:
        o_ref, xn_ref = rest

    @pl.when(pl.program_id(1) == 0)
    def _():
        x = x_ref[...]
        ms = jnp.mean(x * x, axis=-1, keepdims=True)
        xn_ref[...] = (x * lax.rsqrt(ms + EPS) * g_ref[...]).astype(BF16)

    acc = jnp.dot(xn_ref[...], w_ref[...], preferred_element_type=F32)
    if not qk_norm:
        o_ref[...] = acc.astype(o_ref.dtype)
        return
    for c in range(acc.shape[1] // MXU_WIDTH):
        sl = slice(c * MXU_WIDTH, (c + 1) * MXU_WIDTH)
        a = acc[:, sl]
        sq = a * a
        hi = sq.astype(BF16)
        lo = (sq - hi.astype(F32)).astype(BF16)
        ss = (jnp.dot(hi, gsum_ref[...], preferred_element_type=F32)
              + jnp.dot(lo, gsum_ref[...], preferred_element_type=F32))
        y = a * lax.rsqrt(ss * (1.0 / DIFF_QK_DIM) + EPS)
        o_ref[:, sl] = (y * gqk_ref[:, sl]).astype(o_ref.dtype)


def _in_proj(x2, g_mix, w_in_bf, col0, n_out, gqk=None, gsum=None):
    t, d = x2.shape
    tm, tn = min(IN_PROJ_TM, t), IN_PROJ_TN
    qk_norm = gqk is not None
    jb0 = col0 // tn
    in_specs = [
        pl.BlockSpec((tm, d), lambda i, j: (i, 0)),
        pl.BlockSpec((1, d), lambda i, j: (0, 0)),
        pl.BlockSpec((d, tn), lambda i, j: (0, jb0 + j)),
    ]
    args = [x2, g_mix, w_in_bf]
    if qk_norm:
        in_specs += [pl.BlockSpec((1, tn), lambda i, j: (0, j)),
                     pl.BlockSpec((MXU_WIDTH, MXU_WIDTH), lambda i, j: (0, 0))]
        args += [gqk, gsum]
    return pl.pallas_call(
        functools.partial(_in_proj_kernel, qk_norm=qk_norm),
        grid=(t // tm, n_out // tn),
        in_specs=in_specs,
        out_specs=pl.BlockSpec((tm, tn), lambda i, j: (i, j)),
        out_shape=jax.ShapeDtypeStruct((t, n_out), BF16),
        scratch_shapes=[pltpu.VMEM((tm, d), BF16)],
        compiler_params=_cparams(("parallel", "arbitrary")),
        name="in_proj_qk" if qk_norm else "in_proj_vbcu",
    )(*args)


def _attn_kernel(lam_ref, q_ref, k_ref, v_ref, bias_ref, gsub_ref, o_ref,
                 qs_ref, m_ref, acc_ref, s0_ref, s1_ref, p0_ref, p1_ref, a0_ref, a1_ref, *, tq, tk):
    qi = pl.program_id(2)
    dv = DIFF_V_DIM
    s_bufs, p_bufs, a_bufs = (s0_ref, s1_ref), (p0_ref, p1_ref), (a0_ref, a1_ref)

    q = q_ref[0]
    lane = lax.broadcasted_iota(jnp.int32, q.shape, 1)
    zero = jnp.zeros_like(q)
    qs_ref[0:tq, :] = jnp.where(lane < DIFF_QK_DIM, q, zero)
    qs_ref[tq:, :] = jnp.where(lane < DIFF_QK_DIM, zero, q)
    acc_ref[...] = jnp.zeros_like(acc_ref)
    m_ref[...] = jnp.full_like(m_ref, MASK_VALUE)
    p1_ref[...] = jnp.zeros_like(p1_ref)
    a1_ref[...] = jnp.zeros_like(a1_ref)
    ones = jnp.ones((tk, dv), BF16)

    def block_start(step):
        blk = jnp.where(step < 2, 2 * qi + jnp.maximum(step, 0), step - 2)
        return pl.multiple_of(blk * tk, tk)

    def logits_stage(step, dst):
        dst[...] = lax.dot_general(qs_ref[...], k_ref[0, pl.ds(block_start(step), tk), :],
                                   (((1,), (1,)), ((), ())), preferred_element_type=F32)

    def pv_stage(step, buf):
        vx = jnp.concatenate([v_ref[0, pl.ds(block_start(step), tk), :], ones], axis=1)
        alpha = a_bufs[buf][...]
        acc_ref[...] = (jnp.concatenate([alpha, alpha], axis=1) * acc_ref[...]
                        + jnp.dot(p_bufs[buf][...], vx, preferred_element_type=F32))

    def softmax_stage(buf, bias_tile):
        s = s_bufs[buf][...]
        if bias_tile is not None:
            bias = bias_ref[0, bias_tile]
            s = jnp.concatenate([s[:tq] + bias, s[tq:] + bias], axis=0)
        m_prev = m_ref[...]
        m_new = jnp.maximum(m_prev, jnp.max(s, axis=-1, keepdims=True))
        a_bufs[buf][...] = jnp.exp2(m_prev - m_new)
        m_ref[...] = m_new
        p = jnp.exp2(s - jnp.concatenate([m_new] * (tk // LANES), axis=1))
        p_bufs[buf][...] = p.astype(BF16)

    def pair(step0, bias0, bias1, lookahead=True):
        logits_stage(step0 + 1, s_bufs[1])
        pv_stage(step0 - 1, 1)
        softmax_stage(0, bias0)
        if lookahead:
            logits_stage(step0 + 2, s_bufs[0])
        pv_stage(step0, 0)
        softmax_stage(1, bias1)

    logits_stage(0, s_bufs[0])
    pair(0, 0, 1)

    n_far_pairs = jnp.maximum(qi - 1, 0)

    def far_quad(jj, carry):
        pair(4 * jj + 2, None, None)
        pair(4 * jj + 4, None, None)
        return carry

    lax.fori_loop(0, n_far_pairs // 2, far_quad, 0)

    @pl.when(n_far_pairs % 2 == 1)
    def _():
        pair(2 * n_far_pairs, None, None)

    @pl.when(qi > 0)
    def _():
        pair(2 * qi, None, 2, lookahead=False)

    pv_stage(2 * qi + 1, 1)

    acc = acc_ref[...]
    o = (acc[:tq, :dv] / acc[:tq, dv:]) - lam_ref[0] * (acc[tq:, :dv] / acc[tq:, dv:])
    ms = jnp.mean(o * o, axis=-1, keepdims=True)
    o_ref[0] = (o * lax.rsqrt(ms + EPS) * gsub_ref[...]).astype(o_ref.dtype)


def _attention(qk3, vbcu3, bias_tiles, gsub, lam):
    b, s, _ = qk3.shape
    h = N_DIFF_HEADS
    tq, tk = ATTN_TQ, ATTN_TK
    assert tq == 2 * tk and s % tq == 0
    kern = functools.partial(_attn_kernel, tq=tq, tk=tk)
    return pl.pallas_call(
        kern,
        grid=(b, h, s // tq),
        in_specs=[
            pl.BlockSpec(memory_space=pltpu.SMEM),
            pl.BlockSpec((1, tq, LANES), lambda bi, hi, qi: (bi, qi, hi)),
            pl.BlockSpec((1, s, LANES), lambda bi, hi, qi: (bi, 0, h + hi)),
            pl.BlockSpec((1, s, LANES), lambda bi, hi, qi: (bi, 0, hi)),
            pl.BlockSpec((1, 3, tq, tk), lambda bi, hi, qi: (hi, 0, 0, 0)),
            pl.BlockSpec((1, LANES), lambda bi, hi, qi: (0, 0)),
        ],
        out_specs=pl.BlockSpec((1, tq, LANES), lambda bi, hi, qi: (bi, qi, hi)),
        out_shape=jax.ShapeDtypeStruct((b, s, h * DIFF_V_DIM), BF16),
        scratch_shapes=[pltpu.VMEM((2 * tq, LANES), BF16),
                        pltpu.VMEM((2 * tq, LANES), F32),
                        pltpu.VMEM((2 * tq, 2 * DIFF_V_DIM), F32),
                        pltpu.VMEM((2 * tq, tk), F32),
                        pltpu.VMEM((2 * tq, tk), F32),
                        pltpu.VMEM((2 * tq, tk), BF16),
                        pltpu.VMEM((2 * tq, tk), BF16),
                        pltpu.VMEM((2 * tq, LANES), F32),
                        pltpu.VMEM((2 * tq, LANES), F32)],
        compiler_params=_cparams(("parallel", "parallel", "arbitrary")),
        name="diff_attention",
    )(lam, qk3, qk3, vbcu3, bias_tiles, gsub)


def _rel_bucket(rel):
    nb = NUM_BUCKETS // 2
    max_exact = nb // 2
    n = jnp.abs(rel)
    n2 = n * n
    large = max_exact + sum((n2 >= (max_exact * max_exact) * (2 ** k)).astype(jnp.int32)
                            for k in range(1, nb - max_exact))
    return jnp.where(rel > 0, nb, 0) + jnp.where(n < max_exact, n, large)


def _bias_kernel(rb_ref, o_ref, *, tk):
    h = pl.program_id(0)
    tq = o_ref.shape[2]
    qpos = lax.broadcasted_iota(jnp.int32, (tq, tk), 0)
    kcol = lax.broadcasted_iota(jnp.int32, (tq, tk), 1)
    chunk_shift = CHUNK.bit_length() - 1
    far_bias = rb_ref[NUM_BUCKETS // 2 - 1, h]
    for tile, offset in enumerate((0, tk, -tk)):
        kpos = kcol + offset
        bucket = _rel_bucket(kpos - qpos)
        bias = jnp.zeros((tq, tk), F32)
        for b in range(NUM_BUCKETS):
            bias = jnp.where(bucket == b, rb_ref[b, h], bias)
        mask = (kpos >> chunk_shift) <= (qpos >> chunk_shift)
        o_ref[0, tile] = jnp.where(mask, (bias - far_bias) * LOG2E, MASK_VALUE)


def _bias_tiles(rel_bias, tq, tk):
    assert tk >= MAX_DISTANCE
    assert CHUNK & (CHUNK - 1) == 0
    n_heads = rel_bias.shape[1]
    return pl.pallas_call(
        functools.partial(_bias_kernel, tk=tk),
        grid=(n_heads,),
        in_specs=[pl.BlockSpec(memory_space=pltpu.SMEM)],
        out_specs=pl.BlockSpec((1, 3, tq, tk), lambda h: (h, 0, 0, 0)),
        out_shape=jax.ShapeDtypeStruct((n_heads, 3, tq, tk), F32),
        compiler_params=_cparams(("parallel",)),
        name="bias_tiles",
    )(rel_bias)


def _conv_kernel(b_ref, c_ref, u_ref, cp_ref, up_ref, w_ref, cb_ref, g_ref, o_ref, buf_ref):
    si = pl.program_id(1)
    ts = o_ref.shape[1]
    cu = c_ref[0].astype(F32) * u_ref[0].astype(F32)
    prev = cp_ref[0].astype(F32) * up_ref[0].astype(F32)
    buf_ref[0:SUBLANES, :] = jnp.where(si > 0, prev, 0.0)
    buf_ref[SUBLANES:, :] = cu
    conv = (w_ref[0:1, :] * buf_ref[pl.ds(SUBLANES - 2, ts), :]
            + w_ref[1:2, :] * buf_ref[pl.ds(SUBLANES - 1, ts), :]
            + w_ref[2:3, :] * cu)
    z = b_ref[0].astype(F32) * (conv + cb_ref[...])
    for c in range(z.shape[1] // CONV_GROUP_DIM):
        sl = slice(c * CONV_GROUP_DIM, (c + 1) * CONV_GROUP_DIM)
        zc = z[:, sl]
        ms = jnp.mean(zc * zc, axis=-1, keepdims=True)
        o_ref[0, :, sl] = (zc * lax.rsqrt(ms + EPS) * g_ref[:, sl]).astype(o_ref.dtype)


def _short_conv(proj3, conv_w, conv_b, g_conv):
    b, s, n = proj3.shape
    dc = conv_w.shape[1]
    ts = min(CONV_TS, s)
    col0 = (n - 3 * dc) // dc
    halo = ts // SUBLANES

    def main(col):
        return pl.BlockSpec((1, ts, dc), lambda bi, si: (bi, si, col))

    def prev(col):
        return pl.BlockSpec((1, SUBLANES, dc),
                            lambda bi, si: (bi, jnp.maximum(si * halo - 1, 0), col))

    return pl.pallas_call(
        _conv_kernel,
        grid=(b, s // ts),
        in_specs=[main(col0), main(col0 + 1), main(col0 + 2), prev(col0 + 1), prev(col0 + 2),
                  pl.BlockSpec((CONV_WIDTH, dc), lambda bi, si: (0, 0)),
                  pl.BlockSpec((1, dc), lambda bi, si: (0, 0)),
                  pl.BlockSpec((1, dc), lambda bi, si: (0, 0))],
        out_specs=pl.BlockSpec((1, ts, dc), lambda bi, si: (bi, si, 0)),
        out_shape=jax.ShapeDtypeStruct((b, s, dc), BF16),
        scratch_shapes=[pltpu.VMEM((ts + SUBLANES, dc), F32)],
        compiler_params=_cparams(("parallel", "parallel")),
        name="short_conv",
    )(proj3, proj3, proj3, proj3, proj3, conv_w, conv_b, g_conv)


def _out_proj_kernel(x_ref, a_ref, c_ref, wa_ref, wc_ref, g_ref, wr_ref, br_ref, tri_ref,
                     h_ref, hn_ref, mi_ref, wcol_ref, cnt_ref, carry_ref):
    i = pl.program_id(0)
    tm = x_ref.shape[0]

    @pl.when(i == 0)
    def _():
        carry_ref[...] = jnp.zeros_like(carry_ref)

    h = (x_ref[...]
         + jnp.dot(a_ref[...], wa_ref[...], preferred_element_type=F32)
         + jnp.dot(c_ref[...], wc_ref[...], preferred_element_type=F32))
    h_ref[...] = h
    ms = jnp.mean(h * h, axis=-1, keepdims=True)
    hn = h * lax.rsqrt(ms + EPS) * g_ref[...]
    hn_hi = hn.astype(BF16)
    _store_routed_rows(hn_ref, hn)
    hn_lo = (hn - hn_hi.astype(F32)).astype(BF16)

    r_hi = jnp.dot(hn_hi, wr_ref[...], preferred_element_type=F32)
    r_lo = jnp.dot(hn_lo, wr_ref[...], preferred_element_type=F32)
    logits = (r_hi[:, :ROUTER_COLS] + r_hi[:, ROUTER_COLS:] + r_lo[:, :ROUTER_COLS]
              + br_ref[...])
    lt = logits.T

    e = EXPERTS_PER_GROUP
    row = lax.broadcasted_iota(jnp.int32, (e, tm), 0)

    def first_argmax(v):
        vmax = jnp.max(v, axis=0, keepdims=True)
        idx = jnp.min(jnp.where(v == vmax, row, e), axis=0, keepdims=True)
        return vmax, idx

    gl = lt[0:e]
    gmax, g = first_argmax(gl)
    p_g = 1.0 / jnp.sum(jnp.exp(gl - gmax), axis=0, keepdims=True)
    ing = lt[e:2 * e]
    for gi in range(1, N_GROUPS):
        ing = jnp.where(g == gi, lt[(gi + 1) * e:(gi + 2) * e], ing)
    v1, i1 = first_argmax(ing)
    v2, i2 = first_argmax(jnp.where(row == i1, -jnp.inf, ing))
    ex = jnp.exp(v2 - v1)
    w1 = p_g / (1.0 + ex)
    w2 = w1 * ex
    e1 = g * e + i1
    e2 = g * e + i2

    erow = lax.broadcasted_iota(jnp.int32, (N_EXPERTS, tm), 0)
    oh1 = (erow == e1).astype(F32)
    oh2 = (erow == e2).astype(F32)
    oh = jnp.concatenate([oh1, oh2], axis=0).astype(BF16)
    pre = jnp.dot(oh, tri_ref[...], preferred_element_type=F32)
    cnt1 = jnp.sum(oh1, axis=1, keepdims=True)
    cnt2 = jnp.sum(oh2, axis=1, keepdims=True)
    carry = carry_ref[:, 0:1]
    r1 = jnp.sum(oh1 * (pre[:N_EXPERTS] + carry), axis=0, keepdims=True)
    r2 = jnp.sum(oh2 * (pre[N_EXPERTS:] + carry + cnt1), axis=0, keepdims=True)
    new_carry = carry + cnt1 + cnt2
    carry_ref[...] = jnp.broadcast_to(new_carry, carry_ref.shape)
    cnt_ref[...] = jnp.broadcast_to(new_carry, cnt_ref.shape)

    mi_ref[0] = jnp.concatenate(
        [e1, e2, r1.astype(jnp.int32), r2.astype(jnp.int32),
         jnp.zeros((SUBLANES - 4, tm), jnp.int32)], axis=0)
    wrow = jnp.concatenate([w1, w2, jnp.zeros((ROUTER_COLS - 2, tm), F32)], axis=0)
    wcol_ref[...] = wrow.T


def _out_proj(x2, attn2, conv2, wo_a, wo_c, g_ffn, wr, br, tri):
    t, d = x2.shape
    tm = min(OUT_PROJ_TM, t)
    nt = t // tm
    da, dc = attn2.shape[1], conv2.shape[1]
    slabs = d // (2 * LANES)
    const = lambda i: (0, 0)
    return pl.pallas_call(
        _out_proj_kernel,
        grid=(nt,),
        in_specs=[
            pl.BlockSpec((tm, d), lambda i: (i, 0)),
            pl.BlockSpec((tm, da), lambda i: (i, 0)),
            pl.BlockSpec((tm, dc), lambda i: (i, 0)),
            pl.BlockSpec((da, d), const),
            pl.BlockSpec((dc, d), const),
            pl.BlockSpec((1, d), const),
            pl.BlockSpec((d, 2 * ROUTER_COLS), const),
            pl.BlockSpec((1, ROUTER_COLS), const),
            pl.BlockSpec((tm, tm), const),
        ],
        out_specs=[
            pl.BlockSpec((tm, d), lambda i: (i, 0)),
            pl.BlockSpec((tm * slabs, LANES), lambda i: (i, 0)),
            pl.BlockSpec((1, SUBLANES, tm), lambda i: (i, 0, 0)),
            pl.BlockSpec((tm, ROUTER_COLS), lambda i: (i, 0)),
            pl.BlockSpec((N_EXPERTS, LANES), const),
        ],
        out_shape=[
            jax.ShapeDtypeStruct((t, d), F32),
            jax.ShapeDtypeStruct((t * slabs, LANES), jnp.uint32),
            jax.ShapeDtypeStruct((nt, SUBLANES, tm), jnp.int32),
            jax.ShapeDtypeStruct((t, ROUTER_COLS), F32),
            jax.ShapeDtypeStruct((N_EXPERTS, LANES), F32),
        ],
        scratch_shapes=[pltpu.VMEM((N_EXPERTS, LANES), F32)],
        compiler_params=_cparams(("arbitrary",)),
        name="out_proj_router",
    )(x2, attn2, conv2, wo_a, wo_c, g_ffn, wr, br, tri)


def _moe_kernel(te_ref, nv_ref, x_ref, w1_ref, w3_ref, w2_ref, o_ref, w13_ref, w2b_ref):
    i = pl.program_id(0)
    dff = w2_ref.shape[2]

    @pl.when(i < nv_ref[0])
    def _():
        @pl.when(jnp.logical_or(i == 0, te_ref[i] != te_ref[jnp.maximum(i - 1, 0)]))
        def _():
            w13_ref[:, :dff] = w1_ref[0, 0].astype(BF16)
            w13_ref[:, dff:] = w3_ref[0, 0].astype(BF16)
            w2b_ref[...] = w2_ref[0, 0].astype(BF16)

        x = _load_routed_rows(x_ref, MOE_TM, BF16)
        ab = jnp.dot(x, w13_ref[...], preferred_element_type=F32)
        a, b = ab[:, :dff], ab[:, dff:]
        hid = (a * jax.nn.sigmoid(a) * b).astype(BF16)
        _store_routed_rows(o_ref, jnp.dot(hid, w2b_ref[...], preferred_element_type=F32))

    @pl.when(i >= nv_ref[0])
    def _():
        o_ref[...] = jnp.zeros_like(o_ref)


def _moe(xs, w1, w3, w2, layer, tile_expert, n_valid):
    d, dff = w2.shape[3], w2.shape[2]
    slabs = d // (2 * LANES)
    rows = MOE_TM * slabs
    nt = xs.shape[0] // rows
    grid_spec = pltpu.PrefetchScalarGridSpec(
        num_scalar_prefetch=2,
        grid=(nt,),
        in_specs=[
            pl.BlockSpec((rows, LANES), lambda i, te, nv: (jnp.minimum(i, nv[0] - 1), 0)),
            pl.BlockSpec((1, 1, d, dff), lambda i, te, nv: (layer, te[i], 0, 0)),
            pl.BlockSpec((1, 1, d, dff), lambda i, te, nv: (layer, te[i], 0, 0)),
            pl.BlockSpec((1, 1, dff, d), lambda i, te, nv: (layer, te[i], 0, 0)),
        ],
        out_specs=pl.BlockSpec((rows, LANES), lambda i, te, nv: (i, 0)),
        scratch_shapes=[pltpu.VMEM((d, 2 * dff), BF16), pltpu.VMEM((dff, d), BF16)],
    )
    return pl.pallas_call(
        _moe_kernel,
        grid_spec=grid_spec,
        out_shape=jax.ShapeDtypeStruct(xs.shape, jnp.uint32),
        compiler_params=_cparams(("arbitrary",)),
        name="moe_grouped",
    )(tile_expert, n_valid, xs, w1, w3, w2)


def _ple_kernel(h_ref, y1_ref, y2_ref, wcol_ref, p_ref, g_ref, wg_ref, wp_ref, o_ref):
    wcol = wcol_ref[...]
    tm = h_ref.shape[0]
    h = (h_ref[...]
         + wcol[:, 0:1] * _load_routed_rows(y1_ref.at[0], tm, F32)
         + wcol[:, 1:2] * _load_routed_rows(y2_ref.at[0], tm, F32))
    ms = jnp.mean(h * h, axis=-1, keepdims=True)
    hn = (h * lax.rsqrt(ms + EPS) * g_ref[...]).astype(BF16)
    gate = jax.nn.sigmoid(jnp.dot(hn, wg_ref[...], preferred_element_type=F32))
    emb = jnp.dot(p_ref[...].astype(BF16), wp_ref[...], preferred_element_type=F32)
    o_ref[...] = h + gate * emb


def _ple(h1, yg, wcol, p2, g_ple, wg, wp):
    t, d = h1.shape
    tm = min(PLE_TM, t)
    dp = p2.shape[1]
    slabs = d // (2 * LANES)
    const = lambda i: (0, 0)
    row = lambda i: (i, 0)
    return pl.pallas_call(
        _ple_kernel,
        grid=(t // tm,),
        in_specs=[
            pl.BlockSpec((tm, d), row),
            pl.BlockSpec((1, tm * slabs, LANES), lambda i: (0, i, 0)),
            pl.BlockSpec((1, tm * slabs, LANES), lambda i: (1, i, 0)),
            pl.BlockSpec((tm, ROUTER_COLS), row),
            pl.BlockSpec((tm, dp), row),
            pl.BlockSpec((1, d), const),
            pl.BlockSpec((d, d), const),
            pl.BlockSpec((dp, d), const),
        ],
        out_specs=pl.BlockSpec((tm, d), row),
        out_shape=jax.ShapeDtypeStruct((t, d), F32),
        compiler_params=_cparams(("parallel",)),
        name="combine_ple",
    )(h1, yg, yg, wcol, p2, g_ple, wg, wp)


def kernel(x, p, rel_bias, g_mix, w_in, g_q, g_k, lam_q1, lam_k1, lam_q2, lam_k2, g_subln,
           conv_w, conv_b, g_conv, w_o, g_ffn, w_group, b_group, w_expert, b_expert,
           w1, w3, w2, g_ple, w_ple_gate, w_ple_proj):
    depth = g_mix.shape[0]
    assert depth == 1
    li = 0
    b, s, d = x.shape
    t = b * s
    d_attn = N_DIFF_HEADS * DIFF_V_DIM
    x2 = x.reshape(t, d)

    n_groups_qk = d_attn // DIFF_QK_DIM
    gqk = jnp.concatenate([jnp.tile(g_q[li] * (DIFF_QK_DIM ** -0.5 * LOG2E), n_groups_qk),
                           jnp.tile(g_k[li], n_groups_qk)])[None, :].astype(F32)
    blk = jnp.arange(MXU_WIDTH) // DIFF_QK_DIM
    gsum = (blk[:, None] == blk[None, :]).astype(BF16)
    lam = (jnp.exp(jnp.sum(lam_q1[li] * lam_k1[li])) - jnp.exp(jnp.sum(lam_q2[li] * lam_k2[li]))
           + LAM_INIT).reshape(1).astype(F32)
    gsub = (g_subln[li] * (1.0 - LAM_INIT))[None, :].astype(F32)
    bias_tiles = _bias_tiles(rel_bias, ATTN_TQ, ATTN_TK)

    pad_g = EXPERTS_PER_GROUP - N_GROUPS
    pad_e = ROUTER_COLS - EXPERTS_PER_GROUP - N_EXPERTS
    wr_f32 = jnp.concatenate([w_group[li], jnp.zeros((d, pad_g), F32),
                              w_expert[li], jnp.zeros((d, pad_e), F32)], axis=1)
    wr_hi = wr_f32.astype(BF16)
    wr_lo = (wr_f32 - wr_hi.astype(F32)).astype(BF16)
    wr = jnp.concatenate([wr_hi, wr_lo], axis=1)
    br = jnp.concatenate([b_group[li], jnp.full((pad_g,), MASK_VALUE, F32),
                          b_expert[li], jnp.zeros((pad_e,), F32)])[None, :]
    tm_r = min(OUT_PROJ_TM, t)
    ar = jnp.arange(tm_r)
    tri = (ar[:, None] < ar[None, :]).astype(BF16)

    w_in_bf = w_in[li].astype(BF16)
    wo_bf = w_o[li].astype(BF16)
    wg_bf = w_ple_gate[li].astype(BF16)
    wp_bf = w_ple_proj[li].astype(BF16)

    n_qk = 2 * d_attn
    qk = _in_proj(x2, g_mix[li][None, :], w_in_bf, 0, n_qk, gqk, gsum)
    vbcu = _in_proj(x2, g_mix[li][None, :], w_in_bf, n_qk, w_in_bf.shape[1] - n_qk)
    vbcu3 = vbcu.reshape(b, s, -1)
    attn = _attention(qk.reshape(b, s, -1), vbcu3, bias_tiles, gsub, lam)
    conv = _short_conv(vbcu3, conv_w[li], conv_b[li][None, :], g_conv[li][None, :])
    h1, hn, meta_i, wcol, counts = _out_proj(
        x2, attn.reshape(t, -1), conv.reshape(t, -1), wo_bf[:d_attn], wo_bf[d_attn:],
        g_ffn[li][None, :], wr, br, tri)

    eid = jnp.transpose(meta_i[:, 0:2, :], (1, 0, 2)).reshape(2, t)
    rank = jnp.transpose(meta_i[:, 2:4, :], (1, 0, 2)).reshape(2, t)
    cnt = counts[:, 0].astype(jnp.int32)
    tiles_per = (cnt + MOE_TM - 1) // MOE_TM
    experts = jnp.arange(N_EXPERTS, dtype=jnp.int32)
    tile_end = jnp.sum(jnp.where(experts[:, None] <= experts[None, :], tiles_per[:, None], 0),
                       axis=0)
    row_start = (tile_end - tiles_per) * MOE_TM
    n_tiles = (TOP_K * t) // MOE_TM + N_EXPERTS
    pos = rank + jnp.sum(jnp.where(eid[..., None] == experts, row_start, 0), axis=-1)
    tile_expert = jnp.minimum(
        jnp.sum((jnp.arange(n_tiles, dtype=jnp.int32)[:, None] >= tile_end[None, :])
                .astype(jnp.int32), axis=1),
        N_EXPERTS - 1)
    n_valid = tile_end[-1:]

    slabs = d // (2 * LANES)
    row_idx = (pos[:, :, None] * slabs + jnp.arange(slabs, dtype=jnp.int32)).reshape(1, -1)

    xs = _sc_scatter_rows(hn, row_idx, n_tiles * MOE_TM * slabs)
    ys = _moe(xs, w1, w3, w2, li, tile_expert, n_valid)
    yg = _sc_gather_rows(ys, row_idx).reshape(TOP_K, t * slabs, LANES)

    out = _ple(h1, yg, wcol, p[li].reshape(t, -1), g_ple[li][None, :], wg_bf, wp_bf)
    return out.reshape(b, s, d)
```

```python
import functools
import math

import jax
import jax.numpy as jnp
from jax import lax
from jax.experimental import pallas as pl
from jax.experimental.pallas import tpu as pltpu
from jax.experimental.pallas import tpu_sc as plsc

F32 = jnp.float32
BF16 = jnp.bfloat16

N_DIFF_HEADS = 8
DIFF_QK_DIM = 64
DIFF_V_DIM = 128
CHUNK = 64
NUM_BUCKETS = 32
MAX_DISTANCE = 128
CONV_WIDTH = 3
CONV_GROUP_DIM = 128
N_GROUPS = 4
EXPERTS_PER_GROUP = 8
N_EXPERTS = N_GROUPS * EXPERTS_PER_GROUP
TOP_K = 2
EPS = 1e-6
MASK_VALUE = -1e30
LAM_INIT = 0.8 - 0.6 * math.exp(-0.3 * 0)
LOG2E = math.log2(math.e)

LANES = 128
SUBLANES = 8
MXU_WIDTH = 256
VMEM_LIMIT_BYTES = 56 * 1024 * 1024

IN_PROJ_TM = 1024
IN_PROJ_TN = 512
ATTN_TQ = 512
ATTN_TK = 256
CONV_TS = 512
OUT_PROJ_TM = 512
MOE_TM = 256
PLE_TM = 512
ROUTER_COLS = 128
SC_WINDOW = 128
SC_NUM_CORES = 2
SC_NUM_SUBCORES = 16


def _cparams(semantics):
    return pltpu.CompilerParams(dimension_semantics=semantics,
                                vmem_limit_bytes=VMEM_LIMIT_BYTES)


def _store_routed_rows(o_ref, x):
    tm, d = x.shape
    half = d // 2
    slabs = half // LANES
    xr = x.astype(BF16).astype(F32)
    lo = lax.bitcast_convert_type(xr[:, :half], jnp.uint32)
    hi = lax.bitcast_convert_type(xr[:, half:], jnp.uint32)
    packed = (lo >> 16) | (hi & jnp.uint32(0xFFFF0000))
    for r in range(slabs):
        o_ref[pl.ds(r, tm, stride=slabs), :] = packed[:, r * LANES:(r + 1) * LANES]


def _load_routed_rows(x_ref, tm, dtype):
    slabs = x_ref.shape[0] // tm
    parts = [x_ref[pl.ds(r, tm, stride=slabs), :] for r in range(slabs)]
    lo = [lax.bitcast_convert_type(w << 16, F32).astype(dtype) for w in parts]
    hi = [lax.bitcast_convert_type(w & jnp.uint32(0xFFFF0000), F32).astype(dtype) for w in parts]
    return jnp.concatenate(lo + hi, axis=1)


def _sc_mesh():
    return plsc.VectorSubcoreMesh(core_axis_name="core", subcore_axis_name="subcore",
                                  num_cores=SC_NUM_CORES, num_subcores=SC_NUM_SUBCORES)


def _sc_scatter_rows(rows, idx_a, idx_b, n_out_rows):
    n_src = rows.shape[0]

    @functools.partial(pl.kernel, mesh=_sc_mesh(), scratch_types=[],
                       out_type=jax.ShapeDtypeStruct((n_out_rows, LANES), rows.dtype))
    def scatter(x_hbm, ia_hbm, ib_hbm, o_hbm):
        def body(x_vmem, ia_vmem, ib_vmem):
            pltpu.sync_copy(x_vmem, o_hbm.at[ia_vmem.at[0]])
            pltpu.sync_copy(x_vmem, o_hbm.at[ib_vmem.at[0]])

        pltpu.emit_pipeline(
            body,
            grid=(n_src // SC_WINDOW,),
            in_specs=[pl.BlockSpec((SC_WINDOW, LANES), lambda i: (i, 0)),
                      pl.BlockSpec((1, SC_WINDOW), lambda i: (0, i)),
                      pl.BlockSpec((1, SC_WINDOW), lambda i: (0, i))],
            out_specs=[],
            core_axis_name=("core", "subcore"),
            dimension_semantics=(pltpu.PARALLEL,),
        )(x_hbm, ia_hbm, ib_hbm)

    return scatter(rows, idx_a, idx_b)


def _sc_gather_rows(table, idx):
    n_idx = idx.shape[1]

    @functools.partial(pl.kernel, mesh=_sc_mesh(), scratch_types=[],
                       out_type=jax.ShapeDtypeStruct((n_idx, LANES), table.dtype))
    def gather(t_hbm, i_hbm, o_hbm):
        def body(i_vmem, o_vmem):
            pltpu.sync_copy(t_hbm.at[i_vmem.at[0]], o_vmem)

        pltpu.emit_pipeline(
            body,
            grid=(n_idx // SC_WINDOW,),
            in_specs=[pl.BlockSpec((1, SC_WINDOW), lambda i: (0, i))],
            out_specs=[pl.BlockSpec((SC_WINDOW, LANES), lambda i: (i, 0))],
            core_axis_name=("core", "subcore"),
            dimension_semantics=(pltpu.PARALLEL,),
        )(i_hbm, o_hbm)

    return gather(table, idx)


def _in_proj_kernel(x_ref, g_ref, w_ref, *rest, qk_norm):
    if qk_norm:
        gqk_ref, gsum_ref, o_ref, xn_ref = rest
    else:
        o_ref, xn_ref = rest

    @pl.when(pl.program_id(1) == 0)
    def _():
        x = x_ref[...]
        ms = jnp.mean(x * x, axis=-1, keepdims=True)
        xn_ref[...] = (x * lax.rsqrt(ms + EPS) * g_ref[...]).astype(BF16)

    acc = jnp.dot(xn_ref[...], w_ref[...], preferred_element_type=F32)
    if not qk_norm:
        o_ref[...] = acc.astype(o_ref.dtype)
        return
    for c in range(acc.shape[1] // MXU_WIDTH):
        sl = slice(c * MXU_WIDTH, (c + 1) * MXU_WIDTH)
        a = acc[:, sl]
        sq = a * a
        hi = sq.astype(BF16)
        lo = (sq - hi.astype(F32)).astype(BF16)
        ss = (jnp.dot(hi, gsum_ref[...], preferred_element_type=F32)
              + jnp.dot(lo, gsum_ref[...], preferred_element_type=F32))
        y = a * lax.rsqrt(ss * (1.0 / DIFF_QK_DIM) + EPS)
        o_ref[:, sl] = (y * gqk_ref[:, sl]).astype(o_ref.dtype)


def _in_proj(x2, g_mix, w_in_bf, col0, n_out, gqk=None, gsum=None):
    t, d = x2.shape
    tm, tn = min(IN_PROJ_TM, t), IN_PROJ_TN
    qk_norm = gqk is not None
    jb0 = col0 // tn
    in_specs = [
        pl.BlockSpec((tm, d), lambda i, j: (i, 0)),
        pl.BlockSpec((1, d), lambda i, j: (0, 0)),
        pl.BlockSpec((d, tn), lambda i, j: (0, jb0 + j)),
    ]
    args = [x2, g_mix, w_in_bf]
    if qk_norm:
        in_specs += [pl.BlockSpec((1, tn), lambda i, j: (0, j)),
                     pl.BlockSpec((MXU_WIDTH, MXU_WIDTH), lambda i, j: (0, 0))]
        args += [gqk, gsum]
    return pl.pallas_call(
        functools.partial(_in_proj_kernel, qk_norm=qk_norm),
        grid=(t // tm, n_out // tn),
        in_specs=in_specs,
        out_specs=pl.BlockSpec((tm, tn), lambda i, j: (i, j)),
        out_shape=jax.ShapeDtypeStruct((t, n_out), BF16),
        scratch_shapes=[pltpu.VMEM((tm, d), BF16)],
        compiler_params=_cparams(("parallel", "arbitrary")),
        name="in_proj_qk" if qk_norm else "in_proj_vbcu",
    )(*args)


def _attn_kernel(lam_ref, q_ref, k_ref, v_ref, bias_ref, gsub_ref, o_ref,
                 qs_ref, m_ref, acc_ref, s0_ref, s1_ref, p0_ref, p1_ref, a0_ref, a1_ref, *, tq, tk):
    qi = pl.program_id(2)
    dv = DIFF_V_DIM
    s_bufs, p_bufs, a_bufs = (s0_ref, s1_ref), (p0_ref, p1_ref), (a0_ref, a1_ref)

    q = q_ref[0]
    lane = lax.broadcasted_iota(jnp.int32, q.shape, 1)
    zero = jnp.zeros_like(q)
    qs_ref[0:tq, :] = jnp.where(lane < DIFF_QK_DIM, q, zero)
    qs_ref[tq:, :] = jnp.where(lane < DIFF_QK_DIM, zero, q)
    ones = jnp.ones((tk, dv), BF16)

    def block_start(step):
        blk = jnp.where(step < 2, 2 * qi + jnp.maximum(step, 0), step - 2)
        return pl.multiple_of(blk * tk, tk)

    def logits_stage(step, dst):
        dst[...] = lax.dot_general(qs_ref[...], k_ref[0, pl.ds(block_start(step), tk), :],
                                   (((1,), (1,)), ((), ())), preferred_element_type=F32)

    def pv_stage(step, buf, first=False):
        vx = jnp.concatenate([v_ref[0, pl.ds(block_start(step), tk), :], ones], axis=1)
        pv = jnp.dot(p_bufs[buf][...], vx, preferred_element_type=F32)
        if first:
            acc_ref[...] = pv
        else:
            alpha = a_bufs[buf][...]
            acc_ref[...] = jnp.concatenate([alpha, alpha], axis=1) * acc_ref[...] + pv

    def softmax_stage(buf, bias_tile, first=False):
        s = s_bufs[buf][...]
        if bias_tile is not None:
            bias = bias_ref[0, bias_tile]
            s = jnp.concatenate([s[:tq] + bias, s[tq:] + bias], axis=0)
        row_max = jnp.max(s, axis=-1, keepdims=True)
        if first:
            m_new = jnp.broadcast_to(row_max, m_ref.shape)
        else:
            m_prev = m_ref[...]
            m_new = jnp.maximum(m_prev, row_max)
            a_bufs[buf][...] = jnp.exp2(m_prev - m_new)
        m_ref[...] = m_new
        p = jnp.exp2(s - jnp.concatenate([m_new] * (tk // LANES), axis=1))
        p_bufs[buf][...] = p.astype(BF16)

    def pair(step0, bias0, bias1, lookahead=True, first=False):
        logits_stage(step0 + 1, s_bufs[1])
        if not first:
            pv_stage(step0 - 1, 1)
        softmax_stage(0, bias0, first)
        if lookahead:
            logits_stage(step0 + 2, s_bufs[0])
        pv_stage(step0, 0, first)
        softmax_stage(1, bias1)

    logits_stage(0, s_bufs[0])
    pair(0, 0, 1, first=True)

    n_far_pairs = jnp.maximum(qi - 1, 0)

    def far_quad(jj, carry):
        pair(4 * jj + 2, None, None)
        pair(4 * jj + 4, None, None)
        return carry

    lax.fori_loop(0, n_far_pairs // 2, far_quad, 0)

    @pl.when(n_far_pairs % 2 == 1)
    def _():
        pair(2 * n_far_pairs, None, None)

    @pl.when(qi > 0)
    def _():
        pair(2 * qi, None, 2, lookahead=False)

    pv_stage(2 * qi + 1, 1)

    acc = acc_ref[...]
    o = (acc[:tq, :dv] / acc[:tq, dv:]) - lam_ref[0] * (acc[tq:, :dv] / acc[tq:, dv:])
    ms = jnp.mean(o * o, axis=-1, keepdims=True)
    o_ref[0] = (o * lax.rsqrt(ms + EPS) * gsub_ref[...]).astype(o_ref.dtype)


def _attention(qk3, vbcu3, bias_tiles, gsub, lam):
    b, s, _ = qk3.shape
    h = N_DIFF_HEADS
    tq, tk = ATTN_TQ, ATTN_TK
    assert tq == 2 * tk and s % tq == 0
    kern = functools.partial(_attn_kernel, tq=tq, tk=tk)
    return pl.pallas_call(
        kern,
        grid=(b, h, s // tq),
        in_specs=[
            pl.BlockSpec(memory_space=pltpu.SMEM),
            pl.BlockSpec((1, tq, LANES), lambda bi, hi, qi: (bi, qi, hi)),
            pl.BlockSpec((1, s, LANES), lambda bi, hi, qi: (bi, 0, h + hi)),
            pl.BlockSpec((1, s, LANES), lambda bi, hi, qi: (bi, 0, hi)),
            pl.BlockSpec((1, 3, tq, tk), lambda bi, hi, qi: (hi, 0, 0, 0)),
            pl.BlockSpec((1, LANES), lambda bi, hi, qi: (0, 0)),
        ],
        out_specs=pl.BlockSpec((1, tq, LANES), lambda bi, hi, qi: (bi, qi, hi)),
        out_shape=jax.ShapeDtypeStruct((b, s, h * DIFF_V_DIM), BF16),
        scratch_shapes=[pltpu.VMEM((2 * tq, LANES), BF16),
                        pltpu.VMEM((2 * tq, LANES), F32),
                        pltpu.VMEM((2 * tq, 2 * DIFF_V_DIM), F32),
                        pltpu.VMEM((2 * tq, tk), F32),
                        pltpu.VMEM((2 * tq, tk), F32),
                        pltpu.VMEM((2 * tq, tk), BF16),
                        pltpu.VMEM((2 * tq, tk), BF16),
                        pltpu.VMEM((2 * tq, LANES), F32),
                        pltpu.VMEM((2 * tq, LANES), F32)],
        compiler_params=_cparams(("parallel", "parallel", "arbitrary")),
        name="diff_attention",
    )(lam, qk3, qk3, vbcu3, bias_tiles, gsub)


def _rel_bucket(rel):
    nb = NUM_BUCKETS // 2
    max_exact = nb // 2
    n = jnp.abs(rel)
    n2 = n * n
    large = max_exact + sum((n2 >= (max_exact * max_exact) * (2 ** k)).astype(jnp.int32)
                            for k in range(1, nb - max_exact))
    return jnp.where(rel > 0, nb, 0) + jnp.where(n < max_exact, n, large)


def _bias_kernel(rb_ref, o_ref, *, tk):
    h = pl.program_id(0)
    tq = o_ref.shape[2]
    qpos = lax.broadcasted_iota(jnp.int32, (tq, tk), 0)
    kcol = lax.broadcasted_iota(jnp.int32, (tq, tk), 1)
    chunk_shift = CHUNK.bit_length() - 1
    far_bias = rb_ref[NUM_BUCKETS // 2 - 1, h]
    for tile, offset in enumerate((0, tk, -tk)):
        kpos = kcol + offset
        bucket = _rel_bucket(kpos - qpos)
        bias = jnp.zeros((tq, tk), F32)
        for b in range(NUM_BUCKETS):
            bias = jnp.where(bucket == b, rb_ref[b, h], bias)
        mask = (kpos >> chunk_shift) <= (qpos >> chunk_shift)
        o_ref[0, tile] = jnp.where(mask, (bias - far_bias) * LOG2E, MASK_VALUE)


def _bias_tiles(rel_bias, tq, tk):
    assert tk >= MAX_DISTANCE
    assert CHUNK & (CHUNK - 1) == 0
    n_heads = rel_bias.shape[1]
    return pl.pallas_call(
        functools.partial(_bias_kernel, tk=tk),
        grid=(n_heads,),
        in_specs=[pl.BlockSpec(memory_space=pltpu.SMEM)],
        out_specs=pl.BlockSpec((1, 3, tq, tk), lambda h: (h, 0, 0, 0)),
        out_shape=jax.ShapeDtypeStruct((n_heads, 3, tq, tk), F32),
        compiler_params=_cparams(("parallel",)),
        name="bias_tiles",
    )(rel_bias)


def _conv_kernel(b_ref, c_ref, u_ref, cp_ref, up_ref, w_ref, cb_ref, g_ref, o_ref, buf_ref):
    si = pl.program_id(1)
    ts = o_ref.shape[1]
    cu = c_ref[0].astype(F32) * u_ref[0].astype(F32)
    prev = cp_ref[0].astype(F32) * up_ref[0].astype(F32)
    buf_ref[0:SUBLANES, :] = jnp.where(si > 0, prev, 0.0)
    buf_ref[SUBLANES:, :] = cu
    conv = (w_ref[0:1, :] * buf_ref[pl.ds(SUBLANES - 2, ts), :]
            + w_ref[1:2, :] * buf_ref[pl.ds(SUBLANES - 1, ts), :]
            + w_ref[2:3, :] * cu)
    z = b_ref[0].astype(F32) * (conv + cb_ref[...])
    for c in range(z.shape[1] // CONV_GROUP_DIM):
        sl = slice(c * CONV_GROUP_DIM, (c + 1) * CONV_GROUP_DIM)
        zc = z[:, sl]
        ms = jnp.mean(zc * zc, axis=-1, keepdims=True)
        o_ref[0, :, sl] = (zc * lax.rsqrt(ms + EPS) * g_ref[:, sl]).astype(o_ref.dtype)


def _short_conv(proj3, conv_w, conv_b, g_conv):
    b, s, n = proj3.shape
    dc = conv_w.shape[1]
    ts = min(CONV_TS, s)
    col0 = (n - 3 * dc) // dc
    halo = ts // SUBLANES

    def main(col):
        return pl.BlockSpec((1, ts, dc), lambda bi, si: (bi, si, col))

    def prev(col):
        return pl.BlockSpec((1, SUBLANES, dc),
                            lambda bi, si: (bi, jnp.maximum(si * halo - 1, 0), col))

    return pl.pallas_call(
        _conv_kernel,
        grid=(b, s // ts),
        in_specs=[main(col0), main(col0 + 1), main(col0 + 2), prev(col0 + 1), prev(col0 + 2),
                  pl.BlockSpec((CONV_WIDTH, dc), lambda bi, si: (0, 0)),
                  pl.BlockSpec((1, dc), lambda bi, si: (0, 0)),
                  pl.BlockSpec((1, dc), lambda bi, si: (0, 0))],
        out_specs=pl.BlockSpec((1, ts, dc), lambda bi, si: (bi, si, 0)),
        out_shape=jax.ShapeDtypeStruct((b, s, dc), BF16),
        scratch_shapes=[pltpu.VMEM((ts + SUBLANES, dc), F32)],
        compiler_params=_cparams(("parallel", "parallel")),
        name="short_conv",
    )(proj3, proj3, proj3, proj3, proj3, conv_w, conv_b, g_conv)


def _out_proj_kernel(x_ref, a_ref, c_ref, wa_ref, wc_ref, g_ref, wr_ref, br_ref, tri_ref,
                     h_ref, hn_ref, mi_ref, wcol_ref, cnt_ref, carry_ref):
    i = pl.program_id(0)
    tm = x_ref.shape[0]

    @pl.when(i == 0)
    def _():
        carry_ref[...] = jnp.zeros_like(carry_ref)

    h = (x_ref[...]
         + jnp.dot(a_ref[...], wa_ref[...], preferred_element_type=F32)
         + jnp.dot(c_ref[...], wc_ref[...], preferred_element_type=F32))
    h_ref[...] = h
    ms = jnp.mean(h * h, axis=-1, keepdims=True)
    hn = h * lax.rsqrt(ms + EPS) * g_ref[...]
    hn_hi = hn.astype(BF16)
    _store_routed_rows(hn_ref, hn)
    hn_lo = (hn - hn_hi.astype(F32)).astype(BF16)

    r_hi = jnp.dot(hn_hi, wr_ref[...], preferred_element_type=F32)
    r_lo = jnp.dot(hn_lo, wr_ref[...], preferred_element_type=F32)
    logits = (r_hi[:, :ROUTER_COLS] + r_hi[:, ROUTER_COLS:] + r_lo[:, :ROUTER_COLS]
              + br_ref[...])
    lt = logits.T

    e = EXPERTS_PER_GROUP
    row = lax.broadcasted_iota(jnp.int32, (e, tm), 0)

    def first_argmax(v):
        vmax = jnp.max(v, axis=0, keepdims=True)
        idx = jnp.min(jnp.where(v == vmax, row, e), axis=0, keepdims=True)
        return vmax, idx

    gl = lt[0:e]
    gmax, g = first_argmax(gl)
    p_g = 1.0 / jnp.sum(jnp.exp(gl - gmax), axis=0, keepdims=True)
    ing = lt[e:2 * e]
    for gi in range(1, N_GROUPS):
        ing = jnp.where(g == gi, lt[(gi + 1) * e:(gi + 2) * e], ing)
    v1, i1 = first_argmax(ing)
    v2, i2 = first_argmax(jnp.where(row == i1, -jnp.inf, ing))
    ex = jnp.exp(v2 - v1)
    w1 = p_g / (1.0 + ex)
    w2 = w1 * ex
    e1 = g * e + i1
    e2 = g * e + i2

    erow = lax.broadcasted_iota(jnp.int32, (N_EXPERTS, tm), 0)
    oh1 = (erow == e1).astype(F32)
    oh2 = (erow == e2).astype(F32)
    oh = jnp.concatenate([oh1, oh2], axis=0).astype(BF16)
    pre = jnp.dot(oh, tri_ref[...], preferred_element_type=F32)
    cnt1 = jnp.sum(oh1, axis=1, keepdims=True)
    cnt2 = jnp.sum(oh2, axis=1, keepdims=True)
    carry = carry_ref[:, 0:1]
    r1 = jnp.sum(oh1 * (pre[:N_EXPERTS] + carry), axis=0, keepdims=True)
    r2 = jnp.sum(oh2 * (pre[N_EXPERTS:] + carry + cnt1), axis=0, keepdims=True)
    new_carry = carry + cnt1 + cnt2
    carry_ref[...] = jnp.broadcast_to(new_carry, carry_ref.shape)
    cnt_ref[...] = jnp.broadcast_to(new_carry, cnt_ref.shape)

    mi_ref[0] = jnp.concatenate(
        [e1, e2, r1.astype(jnp.int32), r2.astype(jnp.int32),
         jnp.zeros((SUBLANES - 4, tm), jnp.int32)], axis=0)
    wrow = jnp.concatenate([w1, w2, jnp.zeros((ROUTER_COLS - 2, tm), F32)], axis=0)
    wcol_ref[...] = wrow.T


def _out_proj(x2, attn2, conv2, wo_a, wo_c, g_ffn, wr, br, tri):
    t, d = x2.shape
    tm = min(OUT_PROJ_TM, t)
    nt = t // tm
    da, dc = attn2.shape[1], conv2.shape[1]
    slabs = d // (2 * LANES)
    const = lambda i: (0, 0)
    return pl.pallas_call(
        _out_proj_kernel,
        grid=(nt,),
        in_specs=[
            pl.BlockSpec((tm, d), lambda i: (i, 0)),
            pl.BlockSpec((tm, da), lambda i: (i, 0)),
            pl.BlockSpec((tm, dc), lambda i: (i, 0)),
            pl.BlockSpec((da, d), const),
            pl.BlockSpec((dc, d), const),
            pl.BlockSpec((1, d), const),
            pl.BlockSpec((d, 2 * ROUTER_COLS), const),
            pl.BlockSpec((1, ROUTER_COLS), const),
            pl.BlockSpec((tm, tm), const),
        ],
        out_specs=[
            pl.BlockSpec((tm, d), lambda i: (i, 0)),
            pl.BlockSpec((tm * slabs, LANES), lambda i: (i, 0)),
            pl.BlockSpec((1, SUBLANES, tm), lambda i: (i, 0, 0)),
            pl.BlockSpec((tm, ROUTER_COLS), lambda i: (i, 0)),
            pl.BlockSpec((N_EXPERTS, LANES), const),
        ],
        out_shape=[
            jax.ShapeDtypeStruct((t, d), F32),
            jax.ShapeDtypeStruct((t * slabs, LANES), jnp.uint32),
            jax.ShapeDtypeStruct((nt, SUBLANES, tm), jnp.int32),
            jax.ShapeDtypeStruct((t, ROUTER_COLS), F32),
            jax.ShapeDtypeStruct((N_EXPERTS, LANES), F32),
        ],
        scratch_shapes=[pltpu.VMEM((N_EXPERTS, LANES), F32)],
        compiler_params=_cparams(("arbitrary",)),
        name="out_proj_router",
    )(x2, attn2, conv2, wo_a, wo_c, g_ffn, wr, br, tri)


def _moe_kernel(te_ref, nv_ref, x_ref, w1_ref, w3_ref, w2_ref, o_ref, w13_ref, w2b_ref):
    i = pl.program_id(0)
    dff = w2_ref.shape[2]

    @pl.when(i < nv_ref[0])
    def _():
        @pl.when(jnp.logical_or(i == 0, te_ref[i] != te_ref[jnp.maximum(i - 1, 0)]))
        def _():
            w13_ref[:, :dff] = w1_ref[0, 0].astype(BF16)
            w13_ref[:, dff:] = w3_ref[0, 0].astype(BF16)
            w2b_ref[...] = w2_ref[0, 0].astype(BF16)

        x = _load_routed_rows(x_ref, MOE_TM, BF16)
        ab = jnp.dot(x, w13_ref[...], preferred_element_type=F32)
        a, b = ab[:, :dff], ab[:, dff:]
        hid = (a * jax.nn.sigmoid(a) * b).astype(BF16)
        _store_routed_rows(o_ref, jnp.dot(hid, w2b_ref[...], preferred_element_type=F32))

    @pl.when(i >= nv_ref[0])
    def _():
        o_ref[...] = jnp.zeros_like(o_ref)


def _moe(xs, w1, w3, w2, layer, tile_expert, n_valid):
    d, dff = w2.shape[3], w2.shape[2]
    slabs = d // (2 * LANES)
    rows = MOE_TM * slabs
    nt = xs.shape[0] // rows
    grid_spec = pltpu.PrefetchScalarGridSpec(
        num_scalar_prefetch=2,
        grid=(nt,),
        in_specs=[
            pl.BlockSpec((rows, LANES), lambda i, te, nv: (jnp.minimum(i, nv[0] - 1), 0)),
            pl.BlockSpec((1, 1, d, dff), lambda i, te, nv: (layer, te[i], 0, 0)),
            pl.BlockSpec((1, 1, d, dff), lambda i, te, nv: (layer, te[i], 0, 0)),
            pl.BlockSpec((1, 1, dff, d), lambda i, te, nv: (layer, te[i], 0, 0)),
        ],
        out_specs=pl.BlockSpec((rows, LANES), lambda i, te, nv: (i, 0)),
        scratch_shapes=[pltpu.VMEM((d, 2 * dff), BF16), pltpu.VMEM((dff, d), BF16)],
    )
    return pl.pallas_call(
        _moe_kernel,
        grid_spec=grid_spec,
        out_shape=jax.ShapeDtypeStruct(xs.shape, jnp.uint32),
        compiler_params=_cparams(("arbitrary",)),
        name="moe_grouped",
    )(tile_expert, n_valid, xs, w1, w3, w2)


def _ple_kernel(h_ref, y1_ref, y2_ref, wcol_ref, p_ref, g_ref, wg_ref, wp_ref, o_ref):
    wcol = wcol_ref[...]
    tm = h_ref.shape[0]
    h = (h_ref[...]
         + wcol[:, 0:1] * _load_routed_rows(y1_ref.at[0], tm, F32)
         + wcol[:, 1:2] * _load_routed_rows(y2_ref.at[0], tm, F32))
    ms = jnp.mean(h * h, axis=-1, keepdims=True)
    hn = (h * lax.rsqrt(ms + EPS) * g_ref[...]).astype(BF16)
    gate = jax.nn.sigmoid(jnp.dot(hn, wg_ref[...], preferred_element_type=F32))
    emb = jnp.dot(p_ref[...].astype(BF16), wp_ref[...], preferred_element_type=F32)
    o_ref[...] = h + gate * emb


def _ple(h1, yg, wcol, p2, g_ple, wg, wp):
    t, d = h1.shape
    tm = min(PLE_TM, t)
    dp = p2.shape[1]
    slabs = d // (2 * LANES)
    const = lambda i: (0, 0)
    row = lambda i: (i, 0)
    return pl.pallas_call(
        _ple_kernel,
        grid=(t // tm,),
        in_specs=[
            pl.BlockSpec((tm, d), row),
            pl.BlockSpec((1, tm * slabs, LANES), lambda i: (0, i, 0)),
            pl.BlockSpec((1, tm * slabs, LANES), lambda i: (1, i, 0)),
            pl.BlockSpec((tm, ROUTER_COLS), row),
            pl.BlockSpec((tm, dp), row),
            pl.BlockSpec((1, d), const),
            pl.BlockSpec((d, d), const),
            pl.BlockSpec((dp, d), const),
        ],
        out_specs=pl.BlockSpec((tm, d), row),
        out_shape=jax.ShapeDtypeStruct((t, d), F32),
        compiler_params=_cparams(("parallel",)),
        name="combine_ple",
    )(h1, yg, yg, wcol, p2, g_ple, wg, wp)


def kernel(x, p, rel_bias, g_mix, w_in, g_q, g_k, lam_q1, lam_k1, lam_q2, lam_k2, g_subln,
           conv_w, conv_b, g_conv, w_o, g_ffn, w_group, b_group, w_expert, b_expert,
           w1, w3, w2, g_ple, w_ple_gate, w_ple_proj):
    depth = g_mix.shape[0]
    assert depth == 1
    li = 0
    b, s, d = x.shape
    t = b * s
    d_attn = N_DIFF_HEADS * DIFF_V_DIM
    x2 = x.reshape(t, d)

    n_groups_qk = d_attn // DIFF_QK_DIM
    gqk = jnp.concatenate([jnp.tile(g_q[li] * (DIFF_QK_DIM ** -0.5 * LOG2E), n_groups_qk),
                           jnp.tile(g_k[li], n_groups_qk)])[None, :].astype(F32)
    blk = jnp.arange(MXU_WIDTH) // DIFF_QK_DIM
    gsum = (blk[:, None] == blk[None, :]).astype(BF16)
    lam = (jnp.exp(jnp.sum(lam_q1[li] * lam_k1[li])) - jnp.exp(jnp.sum(lam_q2[li] * lam_k2[li]))
           + LAM_INIT).reshape(1).astype(F32)
    gsub = (g_subln[li] * (1.0 - LAM_INIT))[None, :].astype(F32)
    bias_tiles = _bias_tiles(rel_bias, ATTN_TQ, ATTN_TK)

    pad_g = EXPERTS_PER_GROUP - N_GROUPS
    pad_e = ROUTER_COLS - EXPERTS_PER_GROUP - N_EXPERTS
    wr_f32 = jnp.concatenate([w_group[li], jnp.zeros((d, pad_g), F32),
                              w_expert[li], jnp.zeros((d, pad_e), F32)], axis=1)
    wr_hi = wr_f32.astype(BF16)
    wr_lo = (wr_f32 - wr_hi.astype(F32)).astype(BF16)
    wr = jnp.concatenate([wr_hi, wr_lo], axis=1)
    br = jnp.concatenate([b_group[li], jnp.full((pad_g,), MASK_VALUE, F32),
                          b_expert[li], jnp.zeros((pad_e,), F32)])[None, :]
    tm_r = min(OUT_PROJ_TM, t)
    ar = jnp.arange(tm_r)
    tri = (ar[:, None] < ar[None, :]).astype(BF16)

    w_in_bf = w_in[li].astype(BF16)
    wo_bf = w_o[li].astype(BF16)
    wg_bf = w_ple_gate[li].astype(BF16)
    wp_bf = w_ple_proj[li].astype(BF16)

    n_qk = 2 * d_attn
    qk = _in_proj(x2, g_mix[li][None, :], w_in_bf, 0, n_qk, gqk, gsum)
    vbcu = _in_proj(x2, g_mix[li][None, :], w_in_bf, n_qk, w_in_bf.shape[1] - n_qk)
    vbcu3 = vbcu.reshape(b, s, -1)
    attn = _attention(qk.reshape(b, s, -1), vbcu3, bias_tiles, gsub, lam)
    conv = _short_conv(vbcu3, conv_w[li], conv_b[li][None, :], g_conv[li][None, :])
    h1, hn, meta_i, wcol, counts = _out_proj(
        x2, attn.reshape(t, -1), conv.reshape(t, -1), wo_bf[:d_attn], wo_bf[d_attn:],
        g_ffn[li][None, :], wr, br, tri)

    eid = jnp.transpose(meta_i[:, 0:2, :], (1, 0, 2)).reshape(2, t)
    rank = jnp.transpose(meta_i[:, 2:4, :], (1, 0, 2)).reshape(2, t)
    cnt = counts[:, 0].astype(jnp.int32)
    tiles_per = (cnt + MOE_TM - 1) // MOE_TM
    experts = jnp.arange(N_EXPERTS, dtype=jnp.int32)
    tile_end = jnp.sum(jnp.where(experts[:, None] <= experts[None, :], tiles_per[:, None], 0),
                       axis=0)
    row_start = (tile_end - tiles_per) * MOE_TM
    n_tiles = (TOP_K * t) // MOE_TM + N_EXPERTS
    pos = rank + jnp.sum(jnp.where(eid[..., None] == experts, row_start, 0), axis=-1)
    tile_expert = jnp.minimum(
        jnp.sum((jnp.arange(n_tiles, dtype=jnp.int32)[:, None] >= tile_end[None, :])
                .astype(jnp.int32), axis=1),
        N_EXPERTS - 1)
    n_valid = tile_end[-1:]

    slabs = d // (2 * LANES)
    row_idx = (pos[:, :, None] * slabs + jnp.arange(slabs, dtype=jnp.int32)).reshape(TOP_K, -1)

    xs = _sc_scatter_rows(hn, row_idx[0:1], row_idx[1:2], n_tiles * MOE_TM * slabs)
    ys = _moe(xs, w1, w3, w2, li, tile_expert, n_valid)
    yg = _sc_gather_rows(ys, row_idx.reshape(1, -1)).reshape(TOP_K, t * slabs, LANES)

    out = _ple(h1, yg, wcol, p[li].reshape(t, -1), g_ple[li][None, :], wg_bf, wp_bf)
    return out.reshape(b, s, d)
```

```python
import functools
import math

import jax
import jax.numpy as jnp
from jax import lax
from jax.experimental import pallas as pl
from jax.experimental.pallas import tpu as pltpu
from jax.experimental.pallas import tpu_sc as plsc

F32 = jnp.float32
BF16 = jnp.bfloat16

N_DIFF_HEADS = 8
DIFF_QK_DIM = 64
DIFF_V_DIM = 128
CHUNK = 64
NUM_BUCKETS = 32
MAX_DISTANCE = 128
CONV_WIDTH = 3
CONV_GROUP_DIM = 128
N_GROUPS = 4
EXPERTS_PER_GROUP = 8
N_EXPERTS = N_GROUPS * EXPERTS_PER_GROUP
TOP_K = 2
EPS = 1e-6
MASK_VALUE = -1e30
LAM_INIT = 0.8 - 0.6 * math.exp(-0.3 * 0)
LOG2E = math.log2(math.e)

LANES = 128
SUBLANES = 8
MXU_WIDTH = 256
VMEM_LIMIT_BYTES = 56 * 1024 * 1024

IN_PROJ_TM = 1024
IN_PROJ_TN = 512
ATTN_TQ = 512
ATTN_TK = 256
CONV_TS = 512
OUT_PROJ_TM = 512
MOE_TM = 256
PLE_TM = 512
ROUTER_COLS = 128
SC_WINDOW = 128
SC_NUM_CORES = 2
SC_NUM_SUBCORES = 16


def _cparams(semantics):
    return pltpu.CompilerParams(dimension_semantics=semantics,
                                vmem_limit_bytes=VMEM_LIMIT_BYTES)


def _store_routed_rows(o_ref, x):
    d = x.shape[1]
    half = d // 2
    slabs = half // LANES
    xr = x.astype(BF16).astype(F32)
    lo = lax.bitcast_convert_type(xr[:, :half], jnp.uint32)
    hi = lax.bitcast_convert_type(xr[:, half:], jnp.uint32)
    packed = (lo >> 16) | (hi & jnp.uint32(0xFFFF0000))
    for r in range(slabs):
        o_ref[r] = packed[:, r * LANES:(r + 1) * LANES]


def _load_routed_rows(x_ref, dtype):
    parts = [x_ref[r] for r in range(x_ref.shape[0])]
    lo = [lax.bitcast_convert_type(w << 16, F32).astype(dtype) for w in parts]
    hi = [lax.bitcast_convert_type(w & jnp.uint32(0xFFFF0000), F32).astype(dtype) for w in parts]
    return jnp.concatenate(lo + hi, axis=1)


def _sc_mesh():
    return plsc.VectorSubcoreMesh(core_axis_name="core", subcore_axis_name="subcore",
                                  num_cores=SC_NUM_CORES, num_subcores=SC_NUM_SUBCORES)


def _sc_scatter_rows(rows, idx_a, idx_b, n_out_rows):
    n_src = rows.shape[0]

    @functools.partial(pl.kernel, mesh=_sc_mesh(), scratch_types=[],
                       out_type=jax.ShapeDtypeStruct((n_out_rows, LANES), rows.dtype))
    def scatter(x_hbm, ia_hbm, ib_hbm, o_hbm):
        def body(x_vmem, ia_vmem, ib_vmem):
            pltpu.sync_copy(x_vmem, o_hbm.at[ia_vmem.at[0]])
            pltpu.sync_copy(x_vmem, o_hbm.at[ib_vmem.at[0]])

        pltpu.emit_pipeline(
            body,
            grid=(n_src // SC_WINDOW,),
            in_specs=[pl.BlockSpec((SC_WINDOW, LANES), lambda i: (i, 0)),
                      pl.BlockSpec((1, SC_WINDOW), lambda i: (0, i)),
                      pl.BlockSpec((1, SC_WINDOW), lambda i: (0, i))],
            out_specs=[],
            core_axis_name=("core", "subcore"),
            dimension_semantics=(pltpu.PARALLEL,),
        )(x_hbm, ia_hbm, ib_hbm)

    return scatter(rows, idx_a, idx_b)


def _sc_gather_rows(table, idx):
    n_idx = idx.shape[1]

    @functools.partial(pl.kernel, mesh=_sc_mesh(), scratch_types=[],
                       out_type=jax.ShapeDtypeStruct((n_idx, LANES), table.dtype))
    def gather(t_hbm, i_hbm, o_hbm):
        def body(i_vmem, o_vmem):
            pltpu.sync_copy(t_hbm.at[i_vmem.at[0]], o_vmem)

        pltpu.emit_pipeline(
            body,
            grid=(n_idx // SC_WINDOW,),
            in_specs=[pl.BlockSpec((1, SC_WINDOW), lambda i: (0, i))],
            out_specs=[pl.BlockSpec((SC_WINDOW, LANES), lambda i: (i, 0))],
            core_axis_name=("core", "subcore"),
            dimension_semantics=(pltpu.PARALLEL,),
        )(i_hbm, o_hbm)

    return gather(table, idx)


def _in_proj_kernel(x_ref, g_ref, w_ref, *rest, qk_norm):
    if qk_norm:
        gqk_ref, gsum_ref, o_ref, xn_ref = rest
    else:
        o_ref, xn_ref = rest

    @pl.when(pl.program_id(1) == 0)
    def _():
        x = x_ref[...]
        ms = jnp.mean(x * x, axis=-1, keepdims=True)
        xn_ref[...] = (x * lax.rsqrt(ms + EPS) * g_ref[...]).astype(BF16)

    acc = jnp.dot(xn_ref[...], w_ref[...], preferred_element_type=F32)
    if not qk_norm:
        o_ref[...] = acc.astype(o_ref.dtype)
        return
    for c in range(acc.shape[1] // MXU_WIDTH):
        sl = slice(c * MXU_WIDTH, (c + 1) * MXU_WIDTH)
        a = acc[:, sl]
        sq = a * a
        hi = sq.astype(BF16)
        lo = (sq - hi.astype(F32)).astype(BF16)
        ss = (jnp.dot(hi, gsum_ref[...], preferred_element_type=F32)
              + jnp.dot(lo, gsum_ref[...], preferred_element_type=F32))
        y = a * lax.rsqrt(ss * (1.0 / DIFF_QK_DIM) + EPS)
        o_ref[:, sl] = (y * gqk_ref[:, sl]).astype(o_ref.dtype)


def _in_proj(x2, g_mix, w_in_bf, col0, n_out, gqk=None, gsum=None):
    t, d = x2.shape
    tm, tn = min(IN_PROJ_TM, t), IN_PROJ_TN
    qk_norm = gqk is not None
    jb0 = col0 // tn
    in_specs = [
        pl.BlockSpec((tm, d), lambda i, j: (i, 0)),
        pl.BlockSpec((1, d), lambda i, j: (0, 0)),
        pl.BlockSpec((d, tn), lambda i, j: (0, jb0 + j)),
    ]
    args = [x2, g_mix, w_in_bf]
    if qk_norm:
        in_specs += [pl.BlockSpec((1, tn), lambda i, j: (0, j)),
                     pl.BlockSpec((MXU_WIDTH, MXU_WIDTH), lambda i, j: (0, 0))]
        args += [gqk, gsum]
    return pl.pallas_call(
        functools.partial(_in_proj_kernel, qk_norm=qk_norm),
        grid=(t // tm, n_out // tn),
        in_specs=in_specs,
        out_specs=pl.BlockSpec((tm, tn), lambda i, j: (i, j)),
        out_shape=jax.ShapeDtypeStruct((t, n_out), BF16),
        scratch_shapes=[pltpu.VMEM((tm, d), BF16)],
        compiler_params=_cparams(("parallel", "arbitrary")),
        name="in_proj_qk" if qk_norm else "in_proj_vbcu",
    )(*args)


def _attn_kernel(lam_ref, q_ref, k_ref, v_ref, bias_ref, gsub_ref, o_ref,
                 qs_ref, m_ref, acc_ref, s0_ref, s1_ref, p0_ref, p1_ref, a0_ref, a1_ref, *, tq, tk):
    qi = pl.program_id(2)
    dv = DIFF_V_DIM
    s_bufs, p_bufs, a_bufs = (s0_ref, s1_ref), (p0_ref, p1_ref), (a0_ref, a1_ref)

    q = q_ref[0]
    lane = lax.broadcasted_iota(jnp.int32, q.shape, 1)
    zero = jnp.zeros_like(q)
    qs_ref[0:tq, :] = jnp.where(lane < DIFF_QK_DIM, q, zero)
    qs_ref[tq:, :] = jnp.where(lane < DIFF_QK_DIM, zero, q)
    ones = jnp.ones((tk, dv), BF16)

    def block_start(step):
        blk = jnp.where(step < 2, 2 * qi + jnp.maximum(step, 0), step - 2)
        return pl.multiple_of(blk * tk, tk)

    def logits_stage(step, dst):
        dst[...] = lax.dot_general(qs_ref[...], k_ref[0, pl.ds(block_start(step), tk), :],
                                   (((1,), (1,)), ((), ())), preferred_element_type=F32)

    def pv_stage(step, buf, first=False):
        vx = jnp.concatenate([v_ref[0, pl.ds(block_start(step), tk), :], ones], axis=1)
        pv = jnp.dot(p_bufs[buf][...], vx, preferred_element_type=F32)
        if first:
            acc_ref[...] = pv
        else:
            alpha = a_bufs[buf][...]
            acc_ref[...] = jnp.concatenate([alpha, alpha], axis=1) * acc_ref[...] + pv

    def softmax_stage(buf, bias_tile, first=False):
        s = s_bufs[buf][...]
        if bias_tile is not None:
            bias = bias_ref[0, bias_tile]
            s = jnp.concatenate([s[:tq] + bias, s[tq:] + bias], axis=0)
        row_max = jnp.max(s, axis=-1, keepdims=True)
        if first:
            m_new = jnp.broadcast_to(row_max, m_ref.shape)
        else:
            m_prev = m_ref[...]
            m_new = jnp.maximum(m_prev, row_max)
            a_bufs[buf][...] = jnp.exp2(m_prev - m_new)
        m_ref[...] = m_new
        p = jnp.exp2(s - jnp.concatenate([m_new] * (tk // LANES), axis=1))
        p_bufs[buf][...] = p.astype(BF16)

    def pair(step0, bias0, bias1, lookahead=True, first=False):
        logits_stage(step0 + 1, s_bufs[1])
        if not first:
            pv_stage(step0 - 1, 1)
        softmax_stage(0, bias0, first)
        if lookahead:
            logits_stage(step0 + 2, s_bufs[0])
        pv_stage(step0, 0, first)
        softmax_stage(1, bias1)

    logits_stage(0, s_bufs[0])
    pair(0, 0, 1, first=True)

    n_far_pairs = jnp.maximum(qi - 1, 0)

    def far_quad(jj, carry):
        pair(4 * jj + 2, None, None)
        pair(4 * jj + 4, None, None)
        return carry

    lax.fori_loop(0, n_far_pairs // 2, far_quad, 0)

    @pl.when(n_far_pairs % 2 == 1)
    def _():
        pair(2 * n_far_pairs, None, None)

    @pl.when(qi > 0)
    def _():
        pair(2 * qi, None, 2, lookahead=False)

    pv_stage(2 * qi + 1, 1)

    acc = acc_ref[...]
    o = (acc[:tq, :dv] / acc[:tq, dv:]) - lam_ref[0] * (acc[tq:, :dv] / acc[tq:, dv:])
    ms = jnp.mean(o * o, axis=-1, keepdims=True)
    o_ref[0] = (o * lax.rsqrt(ms + EPS) * gsub_ref[...]).astype(o_ref.dtype)


def _attention(qk3, vbcu3, bias_tiles, gsub, lam):
    b, s, _ = qk3.shape
    h = N_DIFF_HEADS
    tq, tk = ATTN_TQ, ATTN_TK
    assert tq == 2 * tk and s % tq == 0
    kern = functools.partial(_attn_kernel, tq=tq, tk=tk)
    return pl.pallas_call(
        kern,
        grid=(b, h, s // tq),
        in_specs=[
            pl.BlockSpec(memory_space=pltpu.SMEM),
            pl.BlockSpec((1, tq, LANES), lambda bi, hi, qi: (bi, qi, hi)),
            pl.BlockSpec((1, s, LANES), lambda bi, hi, qi: (bi, 0, h + hi)),
            pl.BlockSpec((1, s, LANES), lambda bi, hi, qi: (bi, 0, hi)),
            pl.BlockSpec((1, 3, tq, tk), lambda bi, hi, qi: (hi, 0, 0, 0)),
            pl.BlockSpec((1, LANES), lambda bi, hi, qi: (0, 0)),
        ],
        out_specs=pl.BlockSpec((1, tq, LANES), lambda bi, hi, qi: (bi, qi, hi)),
        out_shape=jax.ShapeDtypeStruct((b, s, h * DIFF_V_DIM), BF16),
        scratch_shapes=[pltpu.VMEM((2 * tq, LANES), BF16),
                        pltpu.VMEM((2 * tq, LANES), F32),
                        pltpu.VMEM((2 * tq, 2 * DIFF_V_DIM), F32),
                        pltpu.VMEM((2 * tq, tk), F32),
                        pltpu.VMEM((2 * tq, tk), F32),
                        pltpu.VMEM((2 * tq, tk), BF16),
                        pltpu.VMEM((2 * tq, tk), BF16),
                        pltpu.VMEM((2 * tq, LANES), F32),
                        pltpu.VMEM((2 * tq, LANES), F32)],
        compiler_params=_cparams(("parallel", "parallel", "arbitrary")),
        name="diff_attention",
    )(lam, qk3, qk3, vbcu3, bias_tiles, gsub)


def _rel_bucket(rel):
    nb = NUM_BUCKETS // 2
    max_exact = nb // 2
    n = jnp.abs(rel)
    n2 = n * n
    large = max_exact + sum((n2 >= (max_exact * max_exact) * (2 ** k)).astype(jnp.int32)
                            for k in range(1, nb - max_exact))
    return jnp.where(rel > 0, nb, 0) + jnp.where(n < max_exact, n, large)


def _bias_kernel(rb_ref, o_ref, *, tk):
    h = pl.program_id(0)
    tq = o_ref.shape[2]
    qpos = lax.broadcasted_iota(jnp.int32, (tq, tk), 0)
    kcol = lax.broadcasted_iota(jnp.int32, (tq, tk), 1)
    chunk_shift = CHUNK.bit_length() - 1
    far_bias = rb_ref[NUM_BUCKETS // 2 - 1, h]
    for tile, offset in enumerate((0, tk, -tk)):
        kpos = kcol + offset
        bucket = _rel_bucket(kpos - qpos)
        bias = jnp.zeros((tq, tk), F32)
        for b in range(NUM_BUCKETS):
            bias = jnp.where(bucket == b, rb_ref[b, h], bias)
        mask = (kpos >> chunk_shift) <= (qpos >> chunk_shift)
        o_ref[0, tile] = jnp.where(mask, (bias - far_bias) * LOG2E, MASK_VALUE)


def _bias_tiles(rel_bias, tq, tk):
    assert tk >= MAX_DISTANCE
    assert CHUNK & (CHUNK - 1) == 0
    n_heads = rel_bias.shape[1]
    return pl.pallas_call(
        functools.partial(_bias_kernel, tk=tk),
        grid=(n_heads,),
        in_specs=[pl.BlockSpec(memory_space=pltpu.SMEM)],
        out_specs=pl.BlockSpec((1, 3, tq, tk), lambda h: (h, 0, 0, 0)),
        out_shape=jax.ShapeDtypeStruct((n_heads, 3, tq, tk), F32),
        compiler_params=_cparams(("parallel",)),
        name="bias_tiles",
    )(rel_bias)


def _conv_kernel(b_ref, c_ref, u_ref, cp_ref, up_ref, w_ref, cb_ref, g_ref, o_ref, buf_ref):
    si = pl.program_id(1)
    ts = o_ref.shape[1]
    cu = c_ref[0].astype(F32) * u_ref[0].astype(F32)
    prev = cp_ref[0].astype(F32) * up_ref[0].astype(F32)
    buf_ref[0:SUBLANES, :] = jnp.where(si > 0, prev, 0.0)
    buf_ref[SUBLANES:, :] = cu
    conv = (w_ref[0:1, :] * buf_ref[pl.ds(SUBLANES - 2, ts), :]
            + w_ref[1:2, :] * buf_ref[pl.ds(SUBLANES - 1, ts), :]
            + w_ref[2:3, :] * cu)
    z = b_ref[0].astype(F32) * (conv + cb_ref[...])
    for c in range(z.shape[1] // CONV_GROUP_DIM):
        sl = slice(c * CONV_GROUP_DIM, (c + 1) * CONV_GROUP_DIM)
        zc = z[:, sl]
        ms = jnp.mean(zc * zc, axis=-1, keepdims=True)
        o_ref[0, :, sl] = (zc * lax.rsqrt(ms + EPS) * g_ref[:, sl]).astype(o_ref.dtype)


def _short_conv(proj3, conv_w, conv_b, g_conv):
    b, s, n = proj3.shape
    dc = conv_w.shape[1]
    ts = min(CONV_TS, s)
    col0 = (n - 3 * dc) // dc
    halo = ts // SUBLANES

    def main(col):
        return pl.BlockSpec((1, ts, dc), lambda bi, si: (bi, si, col))

    def prev(col):
        return pl.BlockSpec((1, SUBLANES, dc),
                            lambda bi, si: (bi, jnp.maximum(si * halo - 1, 0), col))

    return pl.pallas_call(
        _conv_kernel,
        grid=(b, s // ts),
        in_specs=[main(col0), main(col0 + 1), main(col0 + 2), prev(col0 + 1), prev(col0 + 2),
                  pl.BlockSpec((CONV_WIDTH, dc), lambda bi, si: (0, 0)),
                  pl.BlockSpec((1, dc), lambda bi, si: (0, 0)),
                  pl.BlockSpec((1, dc), lambda bi, si: (0, 0))],
        out_specs=pl.BlockSpec((1, ts, dc), lambda bi, si: (bi, si, 0)),
        out_shape=jax.ShapeDtypeStruct((b, s, dc), BF16),
        scratch_shapes=[pltpu.VMEM((ts + SUBLANES, dc), F32)],
        compiler_params=_cparams(("parallel", "parallel")),
        name="short_conv",
    )(proj3, proj3, proj3, proj3, proj3, conv_w, conv_b, g_conv)


def _out_proj_kernel(x_ref, a_ref, c_ref, wa_ref, wc_ref, g_ref, wr_ref, br_ref, tri_ref,
                     h_ref, hn_ref, mi_ref, wcol_ref, cnt_ref, carry_ref):
    i = pl.program_id(0)
    tm = x_ref.shape[0]

    @pl.when(i == 0)
    def _():
        carry_ref[...] = jnp.zeros_like(carry_ref)

    h = (x_ref[...]
         + jnp.dot(a_ref[...], wa_ref[...], preferred_element_type=F32)
         + jnp.dot(c_ref[...], wc_ref[...], preferred_element_type=F32))
    h_ref[...] = h
    ms = jnp.mean(h * h, axis=-1, keepdims=True)
    hn = h * lax.rsqrt(ms + EPS) * g_ref[...]
    hn_hi = hn.astype(BF16)
    _store_routed_rows(hn_ref, hn)
    hn_lo = (hn - hn_hi.astype(F32)).astype(BF16)

    r_hi = jnp.dot(hn_hi, wr_ref[...], preferred_element_type=F32)
    r_lo = jnp.dot(hn_lo, wr_ref[...], preferred_element_type=F32)
    logits = (r_hi[:, :ROUTER_COLS] + r_hi[:, ROUTER_COLS:] + r_lo[:, :ROUTER_COLS]
              + br_ref[...])
    lt = logits.T

    e = EXPERTS_PER_GROUP
    row = lax.broadcasted_iota(jnp.int32, (e, tm), 0)

    def first_argmax(v):
        vmax = jnp.max(v, axis=0, keepdims=True)
        idx = jnp.min(jnp.where(v == vmax, row, e), axis=0, keepdims=True)
        return vmax, idx

    gl = lt[0:e]
    gmax, g = first_argmax(gl)
    p_g = 1.0 / jnp.sum(jnp.exp(gl - gmax), axis=0, keepdims=True)
    ing = lt[e:2 * e]
    for gi in range(1, N_GROUPS):
        ing = jnp.where(g == gi, lt[(gi + 1) * e:(gi + 2) * e], ing)
    v1, i1 = first_argmax(ing)
    v2, i2 = first_argmax(jnp.where(row == i1, -jnp.inf, ing))
    ex = jnp.exp(v2 - v1)
    w1 = p_g / (1.0 + ex)
    w2 = w1 * ex
    e1 = g * e + i1
    e2 = g * e + i2

    erow = lax.broadcasted_iota(jnp.int32, (N_EXPERTS, tm), 0)
    oh1 = (erow == e1).astype(F32)
    oh2 = (erow == e2).astype(F32)
    oh = jnp.concatenate([oh1, oh2], axis=0).astype(BF16)
    pre = jnp.dot(oh, tri_ref[...], preferred_element_type=F32)
    cnt1 = jnp.sum(oh1, axis=1, keepdims=True)
    cnt2 = jnp.sum(oh2, axis=1, keepdims=True)
    carry = carry_ref[:, 0:1]
    r1 = jnp.sum(oh1 * (pre[:N_EXPERTS] + carry), axis=0, keepdims=True)
    r2 = jnp.sum(oh2 * (pre[N_EXPERTS:] + carry + cnt1), axis=0, keepdims=True)
    new_carry = carry + cnt1 + cnt2
    carry_ref[...] = jnp.broadcast_to(new_carry, carry_ref.shape)
    cnt_ref[...] = jnp.broadcast_to(new_carry, cnt_ref.shape)

    mi_ref[0] = jnp.concatenate(
        [e1, e2, r1.astype(jnp.int32), r2.astype(jnp.int32),
         jnp.zeros((SUBLANES - 4, tm), jnp.int32)], axis=0)
    wrow = jnp.concatenate([w1, w2, jnp.zeros((ROUTER_COLS - 2, tm), F32)], axis=0)
    wcol_ref[...] = wrow.T


def _out_proj(x2, attn2, conv2, wo_a, wo_c, g_ffn, wr, br, tri):
    t, d = x2.shape
    tm = min(OUT_PROJ_TM, t)
    nt = t // tm
    da, dc = attn2.shape[1], conv2.shape[1]
    slabs = d // (2 * LANES)
    const = lambda i: (0, 0)
    return pl.pallas_call(
        _out_proj_kernel,
        grid=(nt,),
        in_specs=[
            pl.BlockSpec((tm, d), lambda i: (i, 0)),
            pl.BlockSpec((tm, da), lambda i: (i, 0)),
            pl.BlockSpec((tm, dc), lambda i: (i, 0)),
            pl.BlockSpec((da, d), const),
            pl.BlockSpec((dc, d), const),
            pl.BlockSpec((1, d), const),
            pl.BlockSpec((d, 2 * ROUTER_COLS), const),
            pl.BlockSpec((1, ROUTER_COLS), const),
            pl.BlockSpec((tm, tm), const),
        ],
        out_specs=[
            pl.BlockSpec((tm, d), lambda i: (i, 0)),
            pl.BlockSpec((slabs, tm, LANES), lambda i: (0, i, 0)),
            pl.BlockSpec((1, SUBLANES, tm), lambda i: (i, 0, 0)),
            pl.BlockSpec((tm, ROUTER_COLS), lambda i: (i, 0)),
            pl.BlockSpec((N_EXPERTS, LANES), const),
        ],
        out_shape=[
            jax.ShapeDtypeStruct((t, d), F32),
            jax.ShapeDtypeStruct((slabs, t, LANES), jnp.uint32),
            jax.ShapeDtypeStruct((nt, SUBLANES, tm), jnp.int32),
            jax.ShapeDtypeStruct((t, ROUTER_COLS), F32),
            jax.ShapeDtypeStruct((N_EXPERTS, LANES), F32),
        ],
        scratch_shapes=[pltpu.VMEM((N_EXPERTS, LANES), F32)],
        compiler_params=_cparams(("arbitrary",)),
        name="out_proj_router",
    )(x2, attn2, conv2, wo_a, wo_c, g_ffn, wr, br, tri)


def _moe_kernel(te_ref, nv_ref, x_ref, w1_ref, w3_ref, w2_ref, o_ref, w13_ref, w2b_ref):
    i = pl.program_id(0)
    dff = w2_ref.shape[2]

    @pl.when(i < nv_ref[0])
    def _():
        @pl.when(jnp.logical_or(i == 0, te_ref[i] != te_ref[jnp.maximum(i - 1, 0)]))
        def _():
            w13_ref[:, :dff] = w1_ref[0, 0].astype(BF16)
            w13_ref[:, dff:] = w3_ref[0, 0].astype(BF16)
            w2b_ref[...] = w2_ref[0, 0].astype(BF16)

        x = _load_routed_rows(x_ref, BF16)
        ab = jnp.dot(x, w13_ref[...], preferred_element_type=F32)
        a, b = ab[:, :dff], ab[:, dff:]
        hid = (a * jax.nn.sigmoid(a) * b).astype(BF16)
        _store_routed_rows(o_ref, jnp.dot(hid, w2b_ref[...], preferred_element_type=F32))

    @pl.when(i >= nv_ref[0])
    def _():
        o_ref[...] = jnp.zeros_like(o_ref)


def _moe(xs, w1, w3, w2, layer, tile_expert, n_valid):
    d, dff = w2.shape[3], w2.shape[2]
    slabs = xs.shape[0]
    nt = xs.shape[1] // MOE_TM
    grid_spec = pltpu.PrefetchScalarGridSpec(
        num_scalar_prefetch=2,
        grid=(nt,),
        in_specs=[
            pl.BlockSpec((slabs, MOE_TM, LANES),
                         lambda i, te, nv: (0, jnp.minimum(i, nv[0] - 1), 0)),
            pl.BlockSpec((1, 1, d, dff), lambda i, te, nv: (layer, te[i], 0, 0)),
            pl.BlockSpec((1, 1, d, dff), lambda i, te, nv: (layer, te[i], 0, 0)),
            pl.BlockSpec((1, 1, dff, d), lambda i, te, nv: (layer, te[i], 0, 0)),
        ],
        out_specs=pl.BlockSpec((slabs, MOE_TM, LANES), lambda i, te, nv: (0, i, 0)),
        scratch_shapes=[pltpu.VMEM((d, 2 * dff), BF16), pltpu.VMEM((dff, d), BF16)],
    )
    return pl.pallas_call(
        _moe_kernel,
        grid_spec=grid_spec,
        out_shape=jax.ShapeDtypeStruct(xs.shape, jnp.uint32),
        compiler_params=_cparams(("arbitrary",)),
        name="moe_grouped",
    )(tile_expert, n_valid, xs, w1, w3, w2)


def _ple_kernel(h_ref, y1_ref, y2_ref, wcol_ref, p_ref, g_ref, wg_ref, wp_ref, o_ref):
    wcol = wcol_ref[...]
    h = (h_ref[...]
         + wcol[:, 0:1] * _load_routed_rows(y1_ref.at[0], F32)
         + wcol[:, 1:2] * _load_routed_rows(y2_ref.at[0], F32))
    ms = jnp.mean(h * h, axis=-1, keepdims=True)
    hn = (h * lax.rsqrt(ms + EPS) * g_ref[...]).astype(BF16)
    gate = jax.nn.sigmoid(jnp.dot(hn, wg_ref[...], preferred_element_type=F32))
    emb = jnp.dot(p_ref[...].astype(BF16), wp_ref[...], preferred_element_type=F32)
    o_ref[...] = h + gate * emb


def _ple(h1, yg, wcol, p2, g_ple, wg, wp):
    t, d = h1.shape
    tm = min(PLE_TM, t)
    dp = p2.shape[1]
    slabs = d // (2 * LANES)
    const = lambda i: (0, 0)
    row = lambda i: (i, 0)
    return pl.pallas_call(
        _ple_kernel,
        grid=(t // tm,),
        in_specs=[
            pl.BlockSpec((tm, d), row),
            pl.BlockSpec((1, slabs, tm, LANES), lambda i: (0, 0, i, 0)),
            pl.BlockSpec((1, slabs, tm, LANES), lambda i: (1, 0, i, 0)),
            pl.BlockSpec((tm, ROUTER_COLS), row),
            pl.BlockSpec((tm, dp), row),
            pl.BlockSpec((1, d), const),
            pl.BlockSpec((d, d), const),
            pl.BlockSpec((dp, d), const),
        ],
        out_specs=pl.BlockSpec((tm, d), row),
        out_shape=jax.ShapeDtypeStruct((t, d), F32),
        compiler_params=_cparams(("parallel",)),
        name="combine_ple",
    )(h1, yg, yg, wcol, p2, g_ple, wg, wp)


def kernel(x, p, rel_bias, g_mix, w_in, g_q, g_k, lam_q1, lam_k1, lam_q2, lam_k2, g_subln,
           conv_w, conv_b, g_conv, w_o, g_ffn, w_group, b_group, w_expert, b_expert,
           w1, w3, w2, g_ple, w_ple_gate, w_ple_proj):
    depth = g_mix.shape[0]
    assert depth == 1
    li = 0
    b, s, d = x.shape
    t = b * s
    d_attn = N_DIFF_HEADS * DIFF_V_DIM
    x2 = x.reshape(t, d)

    n_groups_qk = d_attn // DIFF_QK_DIM
    gqk = jnp.concatenate([jnp.tile(g_q[li] * (DIFF_QK_DIM ** -0.5 * LOG2E), n_groups_qk),
                           jnp.tile(g_k[li], n_groups_qk)])[None, :].astype(F32)
    blk = jnp.arange(MXU_WIDTH) // DIFF_QK_DIM
    gsum = (blk[:, None] == blk[None, :]).astype(BF16)
    lam = (jnp.exp(jnp.sum(lam_q1[li] * lam_k1[li])) - jnp.exp(jnp.sum(lam_q2[li] * lam_k2[li]))
           + LAM_INIT).reshape(1).astype(F32)
    gsub = (g_subln[li] * (1.0 - LAM_INIT))[None, :].astype(F32)
    bias_tiles = _bias_tiles(rel_bias, ATTN_TQ, ATTN_TK)

    pad_g = EXPERTS_PER_GROUP - N_GROUPS
    pad_e = ROUTER_COLS - EXPERTS_PER_GROUP - N_EXPERTS
    wr_f32 = jnp.concatenate([w_group[li], jnp.zeros((d, pad_g), F32),
                              w_expert[li], jnp.zeros((d, pad_e), F32)], axis=1)
    wr_hi = wr_f32.astype(BF16)
    wr_lo = (wr_f32 - wr_hi.astype(F32)).astype(BF16)
    wr = jnp.concatenate([wr_hi, wr_lo], axis=1)
    br = jnp.concatenate([b_group[li], jnp.full((pad_g,), MASK_VALUE, F32),
                          b_expert[li], jnp.zeros((pad_e,), F32)])[None, :]
    tm_r = min(OUT_PROJ_TM, t)
    ar = jnp.arange(tm_r)
    tri = (ar[:, None] < ar[None, :]).astype(BF16)

    w_in_bf = w_in[li].astype(BF16)
    wo_bf = w_o[li].astype(BF16)
    wg_bf = w_ple_gate[li].astype(BF16)
    wp_bf = w_ple_proj[li].astype(BF16)

    n_qk = 2 * d_attn
    qk = _in_proj(x2, g_mix[li][None, :], w_in_bf, 0, n_qk, gqk, gsum)
    vbcu = _in_proj(x2, g_mix[li][None, :], w_in_bf, n_qk, w_in_bf.shape[1] - n_qk)
    vbcu3 = vbcu.reshape(b, s, -1)
    attn = _attention(qk.reshape(b, s, -1), vbcu3, bias_tiles, gsub, lam)
    conv = _short_conv(vbcu3, conv_w[li], conv_b[li][None, :], g_conv[li][None, :])
    h1, hn, meta_i, wcol, counts = _out_proj(
        x2, attn.reshape(t, -1), conv.reshape(t, -1), wo_bf[:d_attn], wo_bf[d_attn:],
        g_ffn[li][None, :], wr, br, tri)

    eid = jnp.transpose(meta_i[:, 0:2, :], (1, 0, 2)).reshape(2, t)
    rank = jnp.transpose(meta_i[:, 2:4, :], (1, 0, 2)).reshape(2, t)
    cnt = counts[:, 0].astype(jnp.int32)
    tiles_per = (cnt + MOE_TM - 1) // MOE_TM
    experts = jnp.arange(N_EXPERTS, dtype=jnp.int32)
    tile_end = jnp.sum(jnp.where(experts[:, None] <= experts[None, :], tiles_per[:, None], 0),
                       axis=0)
    row_start = (tile_end - tiles_per) * MOE_TM
    n_tiles = (TOP_K * t) // MOE_TM + N_EXPERTS
    pos = rank + jnp.sum(jnp.where(eid[..., None] == experts, row_start, 0), axis=-1)
    tile_expert = jnp.minimum(
        jnp.sum((jnp.arange(n_tiles, dtype=jnp.int32)[:, None] >= tile_end[None, :])
                .astype(jnp.int32), axis=1),
        N_EXPERTS - 1)
    n_valid = tile_end[-1:]

    slabs = d // (2 * LANES)
    n_slots = n_tiles * MOE_TM
    slab_base = jnp.arange(slabs, dtype=jnp.int32) * n_slots
    row_idx = pos[:, None, :] + slab_base[None, :, None]

    xs = _sc_scatter_rows(hn.reshape(slabs * t, LANES), row_idx[0].reshape(1, -1),
                          row_idx[1].reshape(1, -1), slabs * n_slots)
    ys = _moe(xs.reshape(slabs, n_slots, LANES), w1, w3, w2, li, tile_expert, n_valid)
    yg = _sc_gather_rows(ys.reshape(slabs * n_slots, LANES), row_idx.reshape(1, -1))
    yg = yg.reshape(TOP_K, slabs, t, LANES)

    out = _ple(h1, yg, wcol, p[li].reshape(t, -1), g_ple[li][None, :], wg_bf, wp_bf)
    return out.reshape(b, s, d)
```

```python
import functools
import math

import jax
import jax.numpy as jnp
from jax import lax
from jax.experimental import pallas as pl
from jax.experimental.pallas import tpu as pltpu
from jax.experimental.pallas import tpu_sc as plsc

F32 = jnp.float32
BF16 = jnp.bfloat16

N_DIFF_HEADS = 8
DIFF_QK_DIM = 64
DIFF_V_DIM = 128
CHUNK = 64
NUM_BUCKETS = 32
MAX_DISTANCE = 128
CONV_WIDTH = 3
CONV_GROUP_DIM = 128
N_GROUPS = 4
EXPERTS_PER_GROUP = 8
N_EXPERTS = N_GROUPS * EXPERTS_PER_GROUP
TOP_K = 2
EPS = 1e-6
MASK_VALUE = -1e30
LAM_INIT = 0.8 - 0.6 * math.exp(-0.3 * 0)
LOG2E = math.log2(math.e)

LANES = 128
SUBLANES = 8
MXU_WIDTH = 256
VMEM_LIMIT_BYTES = 56 * 1024 * 1024

IN_PROJ_TM = 1024
IN_PROJ_TN = 512
ATTN_TQ = 512
ATTN_TK = 256
CONV_TS = 512
OUT_PROJ_TM = 512
MOE_TM = 256
PLE_TM = 512
ROUTER_COLS = 128
SC_WINDOW = 128
SC_NUM_CORES = 2
SC_NUM_SUBCORES = 16


def _cparams(semantics):
    return pltpu.CompilerParams(dimension_semantics=semantics,
                                vmem_limit_bytes=VMEM_LIMIT_BYTES)


def _store_routed_rows(o_ref, x):
    d = x.shape[1]
    half = d // 2
    slabs = half // LANES
    xr = x.astype(BF16).astype(F32)
    lo = lax.bitcast_convert_type(xr[:, :half], jnp.uint32)
    hi = lax.bitcast_convert_type(xr[:, half:], jnp.uint32)
    packed = (lo >> 16) | (hi & jnp.uint32(0xFFFF0000))
    for r in range(slabs):
        o_ref[r] = packed[:, r * LANES:(r + 1) * LANES]


def _load_routed_rows(x_ref, dtype):
    parts = [x_ref[r] for r in range(x_ref.shape[0])]
    lo = [lax.bitcast_convert_type(w << 16, F32).astype(dtype) for w in parts]
    hi = [lax.bitcast_convert_type(w & jnp.uint32(0xFFFF0000), F32).astype(dtype) for w in parts]
    return jnp.concatenate(lo + hi, axis=1)


def _sc_mesh():
    return plsc.VectorSubcoreMesh(core_axis_name="core", subcore_axis_name="subcore",
                                  num_cores=SC_NUM_CORES, num_subcores=SC_NUM_SUBCORES)


def _sc_scatter_rows(rows, idx_a, idx_b, n_out_rows):
    n_src = rows.shape[0]

    @functools.partial(pl.kernel, mesh=_sc_mesh(), scratch_types=[],
                       out_type=jax.ShapeDtypeStruct((n_out_rows, LANES), rows.dtype))
    def scatter(x_hbm, ia_hbm, ib_hbm, o_hbm):
        def body(x_vmem, ia_vmem, ib_vmem):
            pltpu.sync_copy(x_vmem, o_hbm.at[ia_vmem.at[0]])
            pltpu.sync_copy(x_vmem, o_hbm.at[ib_vmem.at[0]])

        pltpu.emit_pipeline(
            body,
            grid=(n_src // SC_WINDOW,),
            in_specs=[pl.BlockSpec((SC_WINDOW, LANES), lambda i: (i, 0)),
                      pl.BlockSpec((1, SC_WINDOW), lambda i: (0, i)),
                      pl.BlockSpec((1, SC_WINDOW), lambda i: (0, i))],
            out_specs=[],
            core_axis_name=("core", "subcore"),
            dimension_semantics=(pltpu.PARALLEL,),
        )(x_hbm, ia_hbm, ib_hbm)

    return scatter(rows, idx_a, idx_b)


def _sc_gather_rows(table, idx):
    n_idx = idx.shape[1]

    @functools.partial(pl.kernel, mesh=_sc_mesh(), scratch_types=[],
                       out_type=jax.ShapeDtypeStruct((n_idx, LANES), table.dtype))
    def gather(t_hbm, i_hbm, o_hbm):
        def body(i_vmem, o_vmem):
            pltpu.sync_copy(t_hbm.at[i_vmem.at[0]], o_vmem)

        pltpu.emit_pipeline(
            body,
            grid=(n_idx // SC_WINDOW,),
            in_specs=[pl.BlockSpec((1, SC_WINDOW), lambda i: (0, i))],
            out_specs=[pl.BlockSpec((SC_WINDOW, LANES), lambda i: (i, 0))],
            core_axis_name=("core", "subcore"),
            dimension_semantics=(pltpu.PARALLEL,),
        )(i_hbm, o_hbm)

    return gather(table, idx)


def _in_proj_kernel(x_ref, g_ref, w_ref, *rest, qk_norm):
    if qk_norm:
        gqk_ref, gsum_ref, o_ref, xn_ref = rest
    else:
        o_ref, xn_ref = rest

    @pl.when(pl.program_id(1) == 0)
    def _():
        x = x_ref[...]
        ms = jnp.mean(x * x, axis=-1, keepdims=True)
        xn_ref[...] = (x * lax.rsqrt(ms + EPS) * g_ref[...]).astype(BF16)

    acc = jnp.dot(xn_ref[...], w_ref[...], preferred_element_type=F32)
    if not qk_norm:
        o_ref[...] = acc.astype(o_ref.dtype)
        return
    for c in range(acc.shape[1] // MXU_WIDTH):
        sl = slice(c * MXU_WIDTH, (c + 1) * MXU_WIDTH)
        a = acc[:, sl]
        sq = a * a
        hi = sq.astype(BF16)
        lo = (sq - hi.astype(F32)).astype(BF16)
        ss = (jnp.dot(hi, gsum_ref[...], preferred_element_type=F32)
              + jnp.dot(lo, gsum_ref[...], preferred_element_type=F32))
        y = a * lax.rsqrt(ss * (1.0 / DIFF_QK_DIM) + EPS)
        o_ref[:, sl] = (y * gqk_ref[:, sl]).astype(o_ref.dtype)


def _in_proj(x2, g_mix, w_in_bf, col0, n_out, gqk=None, gsum=None):
    t, d = x2.shape
    tm, tn = min(IN_PROJ_TM, t), IN_PROJ_TN
    qk_norm = gqk is not None
    jb0 = col0 // tn
    in_specs = [
        pl.BlockSpec((tm, d), lambda i, j: (i, 0)),
        pl.BlockSpec((1, d), lambda i, j: (0, 0)),
        pl.BlockSpec((d, tn), lambda i, j: (0, jb0 + j)),
    ]
    args = [x2, g_mix, w_in_bf]
    if qk_norm:
        in_specs += [pl.BlockSpec((1, tn), lambda i, j: (0, j)),
                     pl.BlockSpec((MXU_WIDTH, MXU_WIDTH), lambda i, j: (0, 0))]
        args += [gqk, gsum]
    return pl.pallas_call(
        functools.partial(_in_proj_kernel, qk_norm=qk_norm),
        grid=(t // tm, n_out // tn),
        in_specs=in_specs,
        out_specs=pl.BlockSpec((tm, tn), lambda i, j: (i, j)),
        out_shape=jax.ShapeDtypeStruct((t, n_out), BF16),
        scratch_shapes=[pltpu.VMEM((tm, d), BF16)],
        compiler_params=_cparams(("parallel", "arbitrary")),
        name="in_proj_qk" if qk_norm else "in_proj_vbcu",
    )(*args)


def _attn_kernel(lam_ref, q_ref, k_ref, v_ref, bias_ref, gsub_ref, o_ref,
                 qs_ref, acc_ref, s0_ref, s1_ref, p0_ref, p1_ref, m0_ref, m1_ref, *, tq, tk):
    qi = pl.program_id(2)
    dv = DIFF_V_DIM
    s_bufs, p_bufs, m_bufs = (s0_ref, s1_ref), (p0_ref, p1_ref), (m0_ref, m1_ref)

    q = q_ref[0]
    lane = lax.broadcasted_iota(jnp.int32, q.shape, 1)
    zero = jnp.zeros_like(q)
    qs_ref[0:tq, :] = jnp.where(lane < DIFF_QK_DIM, q, zero)
    qs_ref[tq:, :] = jnp.where(lane < DIFF_QK_DIM, zero, q)
    ones = jnp.ones((tk, dv), BF16)

    def block_start(step):
        blk = jnp.where(step < 2, 2 * qi + jnp.maximum(step, 0), step - 2)
        return pl.multiple_of(blk * tk, tk)

    def logits_stage(step, dst):
        dst[...] = lax.dot_general(qs_ref[...], k_ref[0, pl.ds(block_start(step), tk), :],
                                   (((1,), (1,)), ((), ())), preferred_element_type=F32)

    def pv_stage(step, buf, first=False):
        vx = jnp.concatenate([v_ref[0, pl.ds(block_start(step), tk), :], ones], axis=1)
        pv = jnp.dot(p_bufs[buf][...], vx, preferred_element_type=F32)
        if first:
            acc_ref[...] = pv
        else:
            alpha = jnp.exp2(m_bufs[1 - buf][...] - m_bufs[buf][...])
            acc_ref[...] = jnp.concatenate([alpha, alpha], axis=1) * acc_ref[...] + pv

    def softmax_stage(buf, bias_tile, first=False):
        s = s_bufs[buf][...]
        if bias_tile is not None:
            bias = bias_ref[0, bias_tile]
            s = jnp.concatenate([s[:tq] + bias, s[tq:] + bias], axis=0)
        row_max = jnp.max(s, axis=-1, keepdims=True)
        if first:
            m_new = jnp.broadcast_to(row_max, m_bufs[buf].shape)
        else:
            m_new = jnp.maximum(m_bufs[1 - buf][...], row_max)
        m_bufs[buf][...] = m_new
        p = jnp.exp2(s - jnp.concatenate([m_new] * (tk // LANES), axis=1))
        p_bufs[buf][...] = p.astype(BF16)

    def pair(step0, bias0, bias1, lookahead=True, first=False):
        logits_stage(step0 + 1, s_bufs[1])
        if not first:
            pv_stage(step0 - 1, 1)
        softmax_stage(0, bias0, first)
        if lookahead:
            logits_stage(step0 + 2, s_bufs[0])
        pv_stage(step0, 0, first)
        softmax_stage(1, bias1)

    logits_stage(0, s_bufs[0])
    pair(0, 0, 1, first=True)

    n_far_pairs = jnp.maximum(qi - 1, 0)

    def far_quad(jj, carry):
        pair(4 * jj + 2, None, None)
        pair(4 * jj + 4, None, None)
        return carry

    lax.fori_loop(0, n_far_pairs // 2, far_quad, 0)

    @pl.when(n_far_pairs % 2 == 1)
    def _():
        pair(2 * n_far_pairs, None, None)

    @pl.when(qi > 0)
    def _():
        pair(2 * qi, None, 2, lookahead=False)

    pv_stage(2 * qi + 1, 1)

    acc = acc_ref[...]
    o = (acc[:tq, :dv] / acc[:tq, dv:]) - lam_ref[0] * (acc[tq:, :dv] / acc[tq:, dv:])
    ms = jnp.mean(o * o, axis=-1, keepdims=True)
    o_ref[0] = (o * lax.rsqrt(ms + EPS) * gsub_ref[...]).astype(o_ref.dtype)


def _attention(qk3, vbcu3, bias_tiles, gsub, lam):
    b, s, _ = qk3.shape
    h = N_DIFF_HEADS
    tq, tk = ATTN_TQ, ATTN_TK
    assert tq == 2 * tk and s % tq == 0
    kern = functools.partial(_attn_kernel, tq=tq, tk=tk)
    return pl.pallas_call(
        kern,
        grid=(b, h, s // tq),
        in_specs=[
            pl.BlockSpec(memory_space=pltpu.SMEM),
            pl.BlockSpec((1, tq, LANES), lambda bi, hi, qi: (bi, qi, hi)),
            pl.BlockSpec((1, s, LANES), lambda bi, hi, qi: (bi, 0, h + hi)),
            pl.BlockSpec((1, s, LANES), lambda bi, hi, qi: (bi, 0, hi)),
            pl.BlockSpec((1, 3, tq, tk), lambda bi, hi, qi: (hi, 0, 0, 0)),
            pl.BlockSpec((1, LANES), lambda bi, hi, qi: (0, 0)),
        ],
        out_specs=pl.BlockSpec((1, tq, LANES), lambda bi, hi, qi: (bi, qi, hi)),
        out_shape=jax.ShapeDtypeStruct((b, s, h * DIFF_V_DIM), BF16),
        scratch_shapes=[pltpu.VMEM((2 * tq, LANES), BF16),
                        pltpu.VMEM((2 * tq, 2 * DIFF_V_DIM), F32),
                        pltpu.VMEM((2 * tq, tk), F32),
                        pltpu.VMEM((2 * tq, tk), F32),
                        pltpu.VMEM((2 * tq, tk), BF16),
                        pltpu.VMEM((2 * tq, tk), BF16),
                        pltpu.VMEM((2 * tq, LANES), F32),
                        pltpu.VMEM((2 * tq, LANES), F32)],
        compiler_params=_cparams(("parallel", "parallel", "arbitrary")),
        name="diff_attention",
    )(lam, qk3, qk3, vbcu3, bias_tiles, gsub)


def _rel_bucket(rel):
    nb = NUM_BUCKETS // 2
    max_exact = nb // 2
    n = jnp.abs(rel)
    n2 = n * n
    large = max_exact + sum((n2 >= (max_exact * max_exact) * (2 ** k)).astype(jnp.int32)
                            for k in range(1, nb - max_exact))
    return jnp.where(rel > 0, nb, 0) + jnp.where(n < max_exact, n, large)


def _bias_kernel(rb_ref, o_ref, *, tk):
    h = pl.program_id(0)
    tq = o_ref.shape[2]
    qpos = lax.broadcasted_iota(jnp.int32, (tq, tk), 0)
    kcol = lax.broadcasted_iota(jnp.int32, (tq, tk), 1)
    chunk_shift = CHUNK.bit_length() - 1
    far_bias = rb_ref[NUM_BUCKETS // 2 - 1, h]
    for tile, offset in enumerate((0, tk, -tk)):
        kpos = kcol + offset
        bucket = _rel_bucket(kpos - qpos)
        bias = jnp.zeros((tq, tk), F32)
        for b in range(NUM_BUCKETS):
            bias = jnp.where(bucket == b, rb_ref[b, h], bias)
        mask = (kpos >> chunk_shift) <= (qpos >> chunk_shift)
        o_ref[0, tile] = jnp.where(mask, (bias - far_bias) * LOG2E, MASK_VALUE)


def _bias_tiles(rel_bias, tq, tk):
    assert tk >= MAX_DISTANCE
    assert CHUNK & (CHUNK - 1) == 0
    n_heads = rel_bias.shape[1]
    return pl.pallas_call(
        functools.partial(_bias_kernel, tk=tk),
        grid=(n_heads,),
        in_specs=[pl.BlockSpec(memory_space=pltpu.SMEM)],
        out_specs=pl.BlockSpec((1, 3, tq, tk), lambda h: (h, 0, 0, 0)),
        out_shape=jax.ShapeDtypeStruct((n_heads, 3, tq, tk), F32),
        compiler_params=_cparams(("parallel",)),
        name="bias_tiles",
    )(rel_bias)


def _conv_kernel(b_ref, c_ref, u_ref, cp_ref, up_ref, w_ref, cb_ref, g_ref, o_ref, buf_ref):
    si = pl.program_id(1)
    ts = o_ref.shape[1]
    cu = c_ref[0].astype(F32) * u_ref[0].astype(F32)
    prev = cp_ref[0].astype(F32) * up_ref[0].astype(F32)
    buf_ref[0:SUBLANES, :] = jnp.where(si > 0, prev, 0.0)
    buf_ref[SUBLANES:, :] = cu
    conv = (w_ref[0:1, :] * buf_ref[pl.ds(SUBLANES - 2, ts), :]
            + w_ref[1:2, :] * buf_ref[pl.ds(SUBLANES - 1, ts), :]
            + w_ref[2:3, :] * cu)
    z = b_ref[0].astype(F32) * (conv + cb_ref[...])
    for c in range(z.shape[1] // CONV_GROUP_DIM):
        sl = slice(c * CONV_GROUP_DIM, (c + 1) * CONV_GROUP_DIM)
        zc = z[:, sl]
        ms = jnp.mean(zc * zc, axis=-1, keepdims=True)
        o_ref[0, :, sl] = (zc * lax.rsqrt(ms + EPS) * g_ref[:, sl]).astype(o_ref.dtype)


def _short_conv(proj3, conv_w, conv_b, g_conv):
    b, s, n = proj3.shape
    dc = conv_w.shape[1]
    ts = min(CONV_TS, s)
    col0 = (n - 3 * dc) // dc
    halo = ts // SUBLANES

    def main(col):
        return pl.BlockSpec((1, ts, dc), lambda bi, si: (bi, si, col))

    def prev(col):
        return pl.BlockSpec((1, SUBLANES, dc),
                            lambda bi, si: (bi, jnp.maximum(si * halo - 1, 0), col))

    return pl.pallas_call(
        _conv_kernel,
        grid=(b, s // ts),
        in_specs=[main(col0), main(col0 + 1), main(col0 + 2), prev(col0 + 1), prev(col0 + 2),
                  pl.BlockSpec((CONV_WIDTH, dc), lambda bi, si: (0, 0)),
                  pl.BlockSpec((1, dc), lambda bi, si: (0, 0)),
                  pl.BlockSpec((1, dc), lambda bi, si: (0, 0))],
        out_specs=pl.BlockSpec((1, ts, dc), lambda bi, si: (bi, si, 0)),
        out_shape=jax.ShapeDtypeStruct((b, s, dc), BF16),
        scratch_shapes=[pltpu.VMEM((ts + SUBLANES, dc), F32)],
        compiler_params=_cparams(("parallel", "parallel")),
        name="short_conv",
    )(proj3, proj3, proj3, proj3, proj3, conv_w, conv_b, g_conv)


def _out_proj_kernel(x_ref, a_ref, c_ref, wa_ref, wc_ref, g_ref, wr_ref, br_ref, tri_ref,
                     h_ref, hn_ref, mi_ref, wcol_ref, cnt_ref, carry_ref):
    i = pl.program_id(0)
    tm = x_ref.shape[0]

    @pl.when(i == 0)
    def _():
        carry_ref[...] = jnp.zeros_like(carry_ref)

    h = (x_ref[...]
         + jnp.dot(a_ref[...], wa_ref[...], preferred_element_type=F32)
         + jnp.dot(c_ref[...], wc_ref[...], preferred_element_type=F32))
    h_ref[...] = h
    ms = jnp.mean(h * h, axis=-1, keepdims=True)
    hn = h * lax.rsqrt(ms + EPS) * g_ref[...]
    hn_hi = hn.astype(BF16)
    _store_routed_rows(hn_ref, hn)
    hn_lo = (hn - hn_hi.astype(F32)).astype(BF16)

    r_hi = jnp.dot(hn_hi, wr_ref[...], preferred_element_type=F32)
    r_lo = jnp.dot(hn_lo, wr_ref[...], preferred_element_type=F32)
    logits = (r_hi[:, :ROUTER_COLS] + r_hi[:, ROUTER_COLS:] + r_lo[:, :ROUTER_COLS]
              + br_ref[...])
    lt = logits.T

    e = EXPERTS_PER_GROUP
    row = lax.broadcasted_iota(jnp.int32, (e, tm), 0)

    def first_argmax(v):
        vmax = jnp.max(v, axis=0, keepdims=True)
        idx = jnp.min(jnp.where(v == vmax, row, e), axis=0, keepdims=True)
        return vmax, idx

    gl = lt[0:e]
    gmax, g = first_argmax(gl)
    p_g = 1.0 / jnp.sum(jnp.exp(gl - gmax), axis=0, keepdims=True)
    ing = lt[e:2 * e]
    for gi in range(1, N_GROUPS):
        ing = jnp.where(g == gi, lt[(gi + 1) * e:(gi + 2) * e], ing)
    v1, i1 = first_argmax(ing)
    v2, i2 = first_argmax(jnp.where(row == i1, -jnp.inf, ing))
    ex = jnp.exp(v2 - v1)
    w1 = p_g / (1.0 + ex)
    w2 = w1 * ex
    e1 = g * e + i1
    e2 = g * e + i2

    erow = lax.broadcasted_iota(jnp.int32, (N_EXPERTS, tm), 0)
    oh1 = (erow == e1).astype(F32)
    oh2 = (erow == e2).astype(F32)
    oh = jnp.concatenate([oh1, oh2], axis=0).astype(BF16)
    pre = jnp.dot(oh, tri_ref[...], preferred_element_type=F32)
    cnt1 = jnp.sum(oh1, axis=1, keepdims=True)
    cnt2 = jnp.sum(oh2, axis=1, keepdims=True)
    carry = carry_ref[:, 0:1]
    r1 = jnp.sum(oh1 * (pre[:N_EXPERTS] + carry), axis=0, keepdims=True)
    r2 = jnp.sum(oh2 * (pre[N_EXPERTS:] + carry + cnt1), axis=0, keepdims=True)
    new_carry = carry + cnt1 + cnt2
    carry_ref[...] = jnp.broadcast_to(new_carry, carry_ref.shape)
    cnt_ref[...] = jnp.broadcast_to(new_carry, cnt_ref.shape)

    mi_ref[0] = jnp.concatenate(
        [e1, e2, r1.astype(jnp.int32), r2.astype(jnp.int32),
         jnp.zeros((SUBLANES - 4, tm), jnp.int32)], axis=0)
    wrow = jnp.concatenate([w1, w2, jnp.zeros((ROUTER_COLS - 2, tm), F32)], axis=0)
    wcol_ref[...] = wrow.T


def _out_proj(x2, attn2, conv2, wo_a, wo_c, g_ffn, wr, br, tri):
    t, d = x2.shape
    tm = min(OUT_PROJ_TM, t)
    nt = t // tm
    da, dc = attn2.shape[1], conv2.shape[1]
    slabs = d // (2 * LANES)
    const = lambda i: (0, 0)
    return pl.pallas_call(
        _out_proj_kernel,
        grid=(nt,),
        in_specs=[
            pl.BlockSpec((tm, d), lambda i: (i, 0)),
            pl.BlockSpec((tm, da), lambda i: (i, 0)),
            pl.BlockSpec((tm, dc), lambda i: (i, 0)),
            pl.BlockSpec((da, d), const),
            pl.BlockSpec((dc, d), const),
            pl.BlockSpec((1, d), const),
            pl.BlockSpec((d, 2 * ROUTER_COLS), const),
            pl.BlockSpec((1, ROUTER_COLS), const),
            pl.BlockSpec((tm, tm), const),
        ],
        out_specs=[
            pl.BlockSpec((tm, d), lambda i: (i, 0)),
            pl.BlockSpec((slabs, tm, LANES), lambda i: (0, i, 0)),
            pl.BlockSpec((1, SUBLANES, tm), lambda i: (i, 0, 0)),
            pl.BlockSpec((tm, ROUTER_COLS), lambda i: (i, 0)),
            pl.BlockSpec((N_EXPERTS, LANES), const),
        ],
        out_shape=[
            jax.ShapeDtypeStruct((t, d), F32),
            jax.ShapeDtypeStruct((slabs, t, LANES), jnp.uint32),
            jax.ShapeDtypeStruct((nt, SUBLANES, tm), jnp.int32),
            jax.ShapeDtypeStruct((t, ROUTER_COLS), F32),
            jax.ShapeDtypeStruct((N_EXPERTS, LANES), F32),
        ],
        scratch_shapes=[pltpu.VMEM((N_EXPERTS, LANES), F32)],
        compiler_params=_cparams(("arbitrary",)),
        name="out_proj_router",
    )(x2, attn2, conv2, wo_a, wo_c, g_ffn, wr, br, tri)


def _moe_kernel(te_ref, nv_ref, x_ref, w1_ref, w3_ref, w2_ref, o_ref, w13_ref, w2b_ref):
    i = pl.program_id(0)
    dff = w2_ref.shape[2]

    @pl.when(i < nv_ref[0])
    def _():
        @pl.when(jnp.logical_or(i == 0, te_ref[i] != te_ref[jnp.maximum(i - 1, 0)]))
        def _():
            w13_ref[:, :dff] = w1_ref[0, 0].astype(BF16)
            w13_ref[:, dff:] = w3_ref[0, 0].astype(BF16)
            w2b_ref[...] = w2_ref[0, 0].astype(BF16)

        x = _load_routed_rows(x_ref, BF16)
        ab = jnp.dot(x, w13_ref[...], preferred_element_type=F32)
        a, b = ab[:, :dff], ab[:, dff:]
        hid = (a * jax.nn.sigmoid(a) * b).astype(BF16)
        _store_routed_rows(o_ref, jnp.dot(hid, w2b_ref[...], preferred_element_type=F32))

    @pl.when(i >= nv_ref[0])
    def _():
        o_ref[...] = jnp.zeros_like(o_ref)


def _moe(xs, w1, w3, w2, layer, tile_expert, n_valid):
    d, dff = w2.shape[3], w2.shape[2]
    slabs = xs.shape[0]
    nt = xs.shape[1] // MOE_TM
    grid_spec = pltpu.PrefetchScalarGridSpec(
        num_scalar_prefetch=2,
        grid=(nt,),
        in_specs=[
            pl.BlockSpec((slabs, MOE_TM, LANES),
                         lambda i, te, nv: (0, jnp.minimum(i, nv[0] - 1), 0)),
            pl.BlockSpec((1, 1, d, dff), lambda i, te, nv: (layer, te[i], 0, 0)),
            pl.BlockSpec((1, 1, d, dff), lambda i, te, nv: (layer, te[i], 0, 0)),
            pl.BlockSpec((1, 1, dff, d), lambda i, te, nv: (layer, te[i], 0, 0)),
        ],
        out_specs=pl.BlockSpec((slabs, MOE_TM, LANES), lambda i, te, nv: (0, i, 0)),
        scratch_shapes=[pltpu.VMEM((d, 2 * dff), BF16), pltpu.VMEM((dff, d), BF16)],
    )
    return pl.pallas_call(
        _moe_kernel,
        grid_spec=grid_spec,
        out_shape=jax.ShapeDtypeStruct(xs.shape, jnp.uint32),
        compiler_params=_cparams(("arbitrary",)),
        name="moe_grouped",
    )(tile_expert, n_valid, xs, w1, w3, w2)


def _ple_kernel(h_ref, y1_ref, y2_ref, wcol_ref, p_ref, g_ref, wg_ref, wp_ref, o_ref):
    wcol = wcol_ref[...]
    h = (h_ref[...]
         + wcol[:, 0:1] * _load_routed_rows(y1_ref.at[0], F32)
         + wcol[:, 1:2] * _load_routed_rows(y2_ref.at[0], F32))
    ms = jnp.mean(h * h, axis=-1, keepdims=True)
    hn = (h * lax.rsqrt(ms + EPS) * g_ref[...]).astype(BF16)
    gate = jax.nn.sigmoid(jnp.dot(hn, wg_ref[...], preferred_element_type=F32))
    emb = jnp.dot(p_ref[...].astype(BF16), wp_ref[...], preferred_element_type=F32)
    o_ref[...] = h + gate * emb


def _ple(h1, yg, wcol, p2, g_ple, wg, wp):
    t, d = h1.shape
    tm = min(PLE_TM, t)
    dp = p2.shape[1]
    slabs = d // (2 * LANES)
    const = lambda i: (0, 0)
    row = lambda i: (i, 0)
    return pl.pallas_call(
        _ple_kernel,
        grid=(t // tm,),
        in_specs=[
            pl.BlockSpec((tm, d), row),
            pl.BlockSpec((1, slabs, tm, LANES), lambda i: (0, 0, i, 0)),
            pl.BlockSpec((1, slabs, tm, LANES), lambda i: (1, 0, i, 0)),
            pl.BlockSpec((tm, ROUTER_COLS), row),
            pl.BlockSpec((tm, dp), row),
            pl.BlockSpec((1, d), const),
            pl.BlockSpec((d, d), const),
            pl.BlockSpec((dp, d), const),
        ],
        out_specs=pl.BlockSpec((tm, d), row),
        out_shape=jax.ShapeDtypeStruct((t, d), F32),
        compiler_params=_cparams(("parallel",)),
        name="combine_ple",
    )(h1, yg, yg, wcol, p2, g_ple, wg, wp)


def kernel(x, p, rel_bias, g_mix, w_in, g_q, g_k, lam_q1, lam_k1, lam_q2, lam_k2, g_subln,
           conv_w, conv_b, g_conv, w_o, g_ffn, w_group, b_group, w_expert, b_expert,
           w1, w3, w2, g_ple, w_ple_gate, w_ple_proj):
    depth = g_mix.shape[0]
    assert depth == 1
    li = 0
    b, s, d = x.shape
    t = b * s
    d_attn = N_DIFF_HEADS * DIFF_V_DIM
    x2 = x.reshape(t, d)

    n_groups_qk = d_attn // DIFF_QK_DIM
    gqk = jnp.concatenate([jnp.tile(g_q[li] * (DIFF_QK_DIM ** -0.5 * LOG2E), n_groups_qk),
                           jnp.tile(g_k[li], n_groups_qk)])[None, :].astype(F32)
    blk = jnp.arange(MXU_WIDTH) // DIFF_QK_DIM
    gsum = (blk[:, None] == blk[None, :]).astype(BF16)
    lam = (jnp.exp(jnp.sum(lam_q1[li] * lam_k1[li])) - jnp.exp(jnp.sum(lam_q2[li] * lam_k2[li]))
           + LAM_INIT).reshape(1).astype(F32)
    gsub = (g_subln[li] * (1.0 - LAM_INIT))[None, :].astype(F32)
    bias_tiles = _bias_tiles(rel_bias, ATTN_TQ, ATTN_TK)

    pad_g = EXPERTS_PER_GROUP - N_GROUPS
    pad_e = ROUTER_COLS - EXPERTS_PER_GROUP - N_EXPERTS
    wr_f32 = jnp.concatenate([w_group[li], jnp.zeros((d, pad_g), F32),
                              w_expert[li], jnp.zeros((d, pad_e), F32)], axis=1)
    wr_hi = wr_f32.astype(BF16)
    wr_lo = (wr_f32 - wr_hi.astype(F32)).astype(BF16)
    wr = jnp.concatenate([wr_hi, wr_lo], axis=1)
    br = jnp.concatenate([b_group[li], jnp.full((pad_g,), MASK_VALUE, F32),
                          b_expert[li], jnp.zeros((pad_e,), F32)])[None, :]
    tm_r = min(OUT_PROJ_TM, t)
    ar = jnp.arange(tm_r)
    tri = (ar[:, None] < ar[None, :]).astype(BF16)

    w_in_bf = w_in[li].astype(BF16)
    wo_bf = w_o[li].astype(BF16)
    wg_bf = w_ple_gate[li].astype(BF16)
    wp_bf = w_ple_proj[li].astype(BF16)

    n_qk = 2 * d_attn
    qk = _in_proj(x2, g_mix[li][None, :], w_in_bf, 0, n_qk, gqk, gsum)
    vbcu = _in_proj(x2, g_mix[li][None, :], w_in_bf, n_qk, w_in_bf.shape[1] - n_qk)
    vbcu3 = vbcu.reshape(b, s, -1)
    attn = _attention(qk.reshape(b, s, -1), vbcu3, bias_tiles, gsub, lam)
    conv = _short_conv(vbcu3, conv_w[li], conv_b[li][None, :], g_conv[li][None, :])
    h1, hn, meta_i, wcol, counts = _out_proj(
        x2, attn.reshape(t, -1), conv.reshape(t, -1), wo_bf[:d_attn], wo_bf[d_attn:],
        g_ffn[li][None, :], wr, br, tri)

    eid = jnp.transpose(meta_i[:, 0:2, :], (1, 0, 2)).reshape(2, t)
    rank = jnp.transpose(meta_i[:, 2:4, :], (1, 0, 2)).reshape(2, t)
    cnt = counts[:, 0].astype(jnp.int32)
    tiles_per = (cnt + MOE_TM - 1) // MOE_TM
    experts = jnp.arange(N_EXPERTS, dtype=jnp.int32)
    tile_end = jnp.sum(jnp.where(experts[:, None] <= experts[None, :], tiles_per[:, None], 0),
                       axis=0)
    row_start = (tile_end - tiles_per) * MOE_TM
    n_tiles = (TOP_K * t) // MOE_TM + N_EXPERTS
    pos = rank + jnp.sum(jnp.where(eid[..., None] == experts, row_start, 0), axis=-1)
    tile_expert = jnp.minimum(
        jnp.sum((jnp.arange(n_tiles, dtype=jnp.int32)[:, None] >= tile_end[None, :])
                .astype(jnp.int32), axis=1),
        N_EXPERTS - 1)
    n_valid = tile_end[-1:]

    slabs = d // (2 * LANES)
    n_slots = n_tiles * MOE_TM
    slab_base = jnp.arange(slabs, dtype=jnp.int32) * n_slots
    row_idx = pos[:, None, :] + slab_base[None, :, None]

    xs = _sc_scatter_rows(hn.reshape(slabs * t, LANES), row_idx[0].reshape(1, -1),
                          row_idx[1].reshape(1, -1), slabs * n_slots)
    ys = _moe(xs.reshape(slabs, n_slots, LANES), w1, w3, w2, li, tile_expert, n_valid)
    yg = _sc_gather_rows(ys.reshape(slabs * n_slots, LANES), row_idx.reshape(1, -1))
    yg = yg.reshape(TOP_K, slabs, t, LANES)

    out = _ple(h1, yg, wcol, p[li].reshape(t, -1), g_ple[li][None, :], wg_bf, wp_bf)
    return out.reshape(b, s, d)
```

```python
import functools
import math

import jax
import jax.numpy as jnp
from jax import lax
from jax.experimental import pallas as pl
from jax.experimental.pallas import tpu as pltpu
from jax.experimental.pallas import tpu_sc as plsc

F32 = jnp.float32
BF16 = jnp.bfloat16

N_DIFF_HEADS = 8
DIFF_QK_DIM = 64
DIFF_V_DIM = 128
CHUNK = 64
NUM_BUCKETS = 32
MAX_DISTANCE = 128
CONV_WIDTH = 3
CONV_GROUP_DIM = 128
N_GROUPS = 4
EXPERTS_PER_GROUP = 8
N_EXPERTS = N_GROUPS * EXPERTS_PER_GROUP
TOP_K = 2
EPS = 1e-6
MASK_VALUE = -1e30
LAM_INIT = 0.8 - 0.6 * math.exp(-0.3 * 0)
LOG2E = math.log2(math.e)

LANES = 128
SUBLANES = 8
MXU_WIDTH = 256
VMEM_LIMIT_BYTES = 56 * 1024 * 1024

IN_PROJ_TM = 1024
IN_PROJ_TN = 512
ATTN_TQ = 512
ATTN_TK = 256
CONV_TS = 512
OUT_PROJ_TM = 512
MOE_TM = 256
PLE_TM = 512
COMBINE_CHUNKS = 4
ROUTER_COLS = 128
SC_WINDOW = 128
SC_NUM_CORES = 2
SC_NUM_SUBCORES = 16


def _cparams(semantics):
    return pltpu.CompilerParams(dimension_semantics=semantics,
                                vmem_limit_bytes=VMEM_LIMIT_BYTES)


def _store_routed_rows(o_ref, x):
    d = x.shape[1]
    half = d // 2
    slabs = half // LANES
    xr = x.astype(BF16).astype(F32)
    lo = lax.bitcast_convert_type(xr[:, :half], jnp.uint32)
    hi = lax.bitcast_convert_type(xr[:, half:], jnp.uint32)
    packed = (lo >> 16) | (hi & jnp.uint32(0xFFFF0000))
    for r in range(slabs):
        o_ref[r] = packed[:, r * LANES:(r + 1) * LANES]


def _load_routed_rows(x_ref, dtype):
    parts = [x_ref[r] for r in range(x_ref.shape[0])]
    lo = [lax.bitcast_convert_type(w << 16, F32).astype(dtype) for w in parts]
    hi = [lax.bitcast_convert_type(w & jnp.uint32(0xFFFF0000), F32).astype(dtype) for w in parts]
    return jnp.concatenate(lo + hi, axis=1)


def _sc_mesh():
    return plsc.VectorSubcoreMesh(core_axis_name="core", subcore_axis_name="subcore",
                                  num_cores=SC_NUM_CORES, num_subcores=SC_NUM_SUBCORES)


def _sc_scatter_rows(rows, idx_a, idx_b, n_out_rows):
    n_src = rows.shape[0]

    @functools.partial(pl.kernel, mesh=_sc_mesh(), scratch_types=[],
                       out_type=jax.ShapeDtypeStruct((n_out_rows, LANES), rows.dtype))
    def scatter(x_hbm, ia_hbm, ib_hbm, o_hbm):
        def body(x_vmem, ia_vmem, ib_vmem):
            pltpu.sync_copy(x_vmem, o_hbm.at[ia_vmem.at[0]])
            pltpu.sync_copy(x_vmem, o_hbm.at[ib_vmem.at[0]])

        pltpu.emit_pipeline(
            body,
            grid=(n_src // SC_WINDOW,),
            in_specs=[pl.BlockSpec((SC_WINDOW, LANES), lambda i: (i, 0)),
                      pl.BlockSpec((1, SC_WINDOW), lambda i: (0, i)),
                      pl.BlockSpec((1, SC_WINDOW), lambda i: (0, i))],
            out_specs=[],
            core_axis_name=("core", "subcore"),
            dimension_semantics=(pltpu.PARALLEL,),
        )(x_hbm, ia_hbm, ib_hbm)

    return scatter(rows, idx_a, idx_b)


def _sc_gather_rows(table, idx):
    n_idx = idx.shape[1]

    @functools.partial(pl.kernel, mesh=_sc_mesh(), scratch_types=[],
                       out_type=jax.ShapeDtypeStruct((n_idx, LANES), table.dtype))
    def gather(t_hbm, i_hbm, o_hbm):
        def body(i_vmem, o_vmem):
            pltpu.sync_copy(t_hbm.at[i_vmem.at[0]], o_vmem)

        pltpu.emit_pipeline(
            body,
            grid=(n_idx // SC_WINDOW,),
            in_specs=[pl.BlockSpec((1, SC_WINDOW), lambda i: (0, i))],
            out_specs=[pl.BlockSpec((SC_WINDOW, LANES), lambda i: (i, 0))],
            core_axis_name=("core", "subcore"),
            dimension_semantics=(pltpu.PARALLEL,),
        )(i_hbm, o_hbm)

    return gather(table, idx)


def _in_proj_kernel(x_ref, g_ref, w_ref, *rest, qk_norm):
    if qk_norm:
        gqk_ref, gsum_ref, o_ref, xn_ref = rest
    else:
        o_ref, xn_ref = rest

    @pl.when(pl.program_id(1) == 0)
    def _():
        x = x_ref[...]
        ms = jnp.mean(x * x, axis=-1, keepdims=True)
        xn_ref[...] = (x * lax.rsqrt(ms + EPS) * g_ref[...]).astype(BF16)

    acc = jnp.dot(xn_ref[...], w_ref[...], preferred_element_type=F32)
    if not qk_norm:
        o_ref[...] = acc.astype(o_ref.dtype)
        return
    for c in range(acc.shape[1] // MXU_WIDTH):
        sl = slice(c * MXU_WIDTH, (c + 1) * MXU_WIDTH)
        a = acc[:, sl]
        sq = a * a
        hi = sq.astype(BF16)
        lo = (sq - hi.astype(F32)).astype(BF16)
        ss = (jnp.dot(hi, gsum_ref[...], preferred_element_type=F32)
              + jnp.dot(lo, gsum_ref[...], preferred_element_type=F32))
        y = a * lax.rsqrt(ss * (1.0 / DIFF_QK_DIM) + EPS)
        o_ref[:, sl] = (y * gqk_ref[:, sl]).astype(o_ref.dtype)


def _in_proj(x2, g_mix, w_in_bf, col0, n_out, gqk=None, gsum=None):
    t, d = x2.shape
    tm, tn = min(IN_PROJ_TM, t), IN_PROJ_TN
    qk_norm = gqk is not None
    jb0 = col0 // tn
    in_specs = [
        pl.BlockSpec((tm, d), lambda i, j: (i, 0)),
        pl.BlockSpec((1, d), lambda i, j: (0, 0)),
        pl.BlockSpec((d, tn), lambda i, j: (0, jb0 + j)),
    ]
    args = [x2, g_mix, w_in_bf]
    if qk_norm:
        in_specs += [pl.BlockSpec((1, tn), lambda i, j: (0, j)),
                     pl.BlockSpec((MXU_WIDTH, MXU_WIDTH), lambda i, j: (0, 0))]
        args += [gqk, gsum]
    return pl.pallas_call(
        functools.partial(_in_proj_kernel, qk_norm=qk_norm),
        grid=(t // tm, n_out // tn),
        in_specs=in_specs,
        out_specs=pl.BlockSpec((tm, tn), lambda i, j: (i, j)),
        out_shape=jax.ShapeDtypeStruct((t, n_out), BF16),
        scratch_shapes=[pltpu.VMEM((tm, d), BF16)],
        compiler_params=_cparams(("parallel", "arbitrary")),
        name="in_proj_qk" if qk_norm else "in_proj_vbcu",
    )(*args)


def _attn_kernel(lam_ref, q_ref, k_ref, v_ref, bias_ref, gsub_ref, o_ref,
                 qs_ref, acc_ref, s0_ref, s1_ref, p0_ref, p1_ref, m0_ref, m1_ref, *, tq, tk):
    qi = pl.program_id(2)
    dv = DIFF_V_DIM
    s_bufs, p_bufs, m_bufs = (s0_ref, s1_ref), (p0_ref, p1_ref), (m0_ref, m1_ref)

    q = q_ref[0]
    lane = lax.broadcasted_iota(jnp.int32, q.shape, 1)
    zero = jnp.zeros_like(q)
    qs_ref[0:tq, :] = jnp.where(lane < DIFF_QK_DIM, q, zero)
    qs_ref[tq:, :] = jnp.where(lane < DIFF_QK_DIM, zero, q)
    ones = jnp.ones((tk, dv), BF16)

    def block_start(step):
        blk = jnp.where(step < 2, 2 * qi + jnp.maximum(step, 0), step - 2)
        return pl.multiple_of(blk * tk, tk)

    def logits_stage(step, dst):
        dst[...] = lax.dot_general(qs_ref[...], k_ref[0, pl.ds(block_start(step), tk), :],
                                   (((1,), (1,)), ((), ())), preferred_element_type=F32)

    def pv_stage(step, buf, first=False):
        vx = jnp.concatenate([v_ref[0, pl.ds(block_start(step), tk), :], ones], axis=1)
        pv = jnp.dot(p_bufs[buf][...], vx, preferred_element_type=F32)
        if first:
            acc_ref[...] = pv
        else:
            alpha = jnp.exp2(m_bufs[1 - buf][...] - m_bufs[buf][...])
            acc_ref[...] = jnp.concatenate([alpha, alpha], axis=1) * acc_ref[...] + pv

    def softmax_stage(buf, bias_tile, first=False):
        s = s_bufs[buf][...]
        if bias_tile is not None:
            bias = bias_ref[0, bias_tile]
            s = jnp.concatenate([s[:tq] + bias, s[tq:] + bias], axis=0)
        row_max = jnp.max(s, axis=-1, keepdims=True)
        if first:
            m_new = jnp.broadcast_to(row_max, m_bufs[buf].shape)
        else:
            m_new = jnp.maximum(m_bufs[1 - buf][...], row_max)
        m_bufs[buf][...] = m_new
        p = jnp.exp2(s - jnp.concatenate([m_new] * (tk // LANES), axis=1))
        p_bufs[buf][...] = p.astype(BF16)

    def pair(step0, bias0, bias1, lookahead=True, first=False):
        logits_stage(step0 + 1, s_bufs[1])
        if not first:
            pv_stage(step0 - 1, 1)
        softmax_stage(0, bias0, first)
        if lookahead:
            logits_stage(step0 + 2, s_bufs[0])
        pv_stage(step0, 0, first)
        softmax_stage(1, bias1)

    logits_stage(0, s_bufs[0])
    pair(0, 0, 1, first=True)

    n_far_pairs = jnp.maximum(qi - 1, 0)

    def far_quad(jj, carry):
        pair(4 * jj + 2, None, None)
        pair(4 * jj + 4, None, None)
        return carry

    lax.fori_loop(0, n_far_pairs // 2, far_quad, 0)

    @pl.when(n_far_pairs % 2 == 1)
    def _():
        pair(2 * n_far_pairs, None, None)

    @pl.when(qi > 0)
    def _():
        pair(2 * qi, None, 2, lookahead=False)

    pv_stage(2 * qi + 1, 1)

    acc = acc_ref[...]
    o = (acc[:tq, :dv] / acc[:tq, dv:]) - lam_ref[0] * (acc[tq:, :dv] / acc[tq:, dv:])
    ms = jnp.mean(o * o, axis=-1, keepdims=True)
    o_ref[0] = (o * lax.rsqrt(ms + EPS) * gsub_ref[...]).astype(o_ref.dtype)


def _attention(qk3, vbcu3, bias_tiles, gsub, lam):
    b, s, _ = qk3.shape
    h = N_DIFF_HEADS
    tq, tk = ATTN_TQ, ATTN_TK
    assert tq == 2 * tk and s % tq == 0
    kern = functools.partial(_attn_kernel, tq=tq, tk=tk)
    return pl.pallas_call(
        kern,
        grid=(b, h, s // tq),
        in_specs=[
            pl.BlockSpec(memory_space=pltpu.SMEM),
            pl.BlockSpec((1, tq, LANES), lambda bi, hi, qi: (bi, qi, hi)),
            pl.BlockSpec((1, s, LANES), lambda bi, hi, qi: (bi, 0, h + hi)),
            pl.BlockSpec((1, s, LANES), lambda bi, hi, qi: (bi, 0, hi)),
            pl.BlockSpec((1, 3, tq, tk), lambda bi, hi, qi: (hi, 0, 0, 0)),
            pl.BlockSpec((1, LANES), lambda bi, hi, qi: (0, 0)),
        ],
        out_specs=pl.BlockSpec((1, tq, LANES), lambda bi, hi, qi: (bi, qi, hi)),
        out_shape=jax.ShapeDtypeStruct((b, s, h * DIFF_V_DIM), BF16),
        scratch_shapes=[pltpu.VMEM((2 * tq, LANES), BF16),
                        pltpu.VMEM((2 * tq, 2 * DIFF_V_DIM), F32),
                        pltpu.VMEM((2 * tq, tk), F32),
                        pltpu.VMEM((2 * tq, tk), F32),
                        pltpu.VMEM((2 * tq, tk), BF16),
                        pltpu.VMEM((2 * tq, tk), BF16),
                        pltpu.VMEM((2 * tq, LANES), F32),
                        pltpu.VMEM((2 * tq, LANES), F32)],
        compiler_params=_cparams(("parallel", "parallel", "arbitrary")),
        name="diff_attention",
    )(lam, qk3, qk3, vbcu3, bias_tiles, gsub)


def _rel_bucket(rel):
    nb = NUM_BUCKETS // 2
    max_exact = nb // 2
    n = jnp.abs(rel)
    n2 = n * n
    large = max_exact + sum((n2 >= (max_exact * max_exact) * (2 ** k)).astype(jnp.int32)
                            for k in range(1, nb - max_exact))
    return jnp.where(rel > 0, nb, 0) + jnp.where(n < max_exact, n, large)


def _bias_kernel(rb_ref, o_ref, *, tk):
    h = pl.program_id(0)
    tq = o_ref.shape[2]
    qpos = lax.broadcasted_iota(jnp.int32, (tq, tk), 0)
    kcol = lax.broadcasted_iota(jnp.int32, (tq, tk), 1)
    chunk_shift = CHUNK.bit_length() - 1
    far_bias = rb_ref[NUM_BUCKETS // 2 - 1, h]
    for tile, offset in enumerate((0, tk, -tk)):
        kpos = kcol + offset
        bucket = _rel_bucket(kpos - qpos)
        bias = jnp.zeros((tq, tk), F32)
        for b in range(NUM_BUCKETS):
            bias = jnp.where(bucket == b, rb_ref[b, h], bias)
        mask = (kpos >> chunk_shift) <= (qpos >> chunk_shift)
        o_ref[0, tile] = jnp.where(mask, (bias - far_bias) * LOG2E, MASK_VALUE)


def _bias_tiles(rel_bias, tq, tk):
    assert tk >= MAX_DISTANCE
    assert CHUNK & (CHUNK - 1) == 0
    n_heads = rel_bias.shape[1]
    return pl.pallas_call(
        functools.partial(_bias_kernel, tk=tk),
        grid=(n_heads,),
        in_specs=[pl.BlockSpec(memory_space=pltpu.SMEM)],
        out_specs=pl.BlockSpec((1, 3, tq, tk), lambda h: (h, 0, 0, 0)),
        out_shape=jax.ShapeDtypeStruct((n_heads, 3, tq, tk), F32),
        compiler_params=_cparams(("parallel",)),
        name="bias_tiles",
    )(rel_bias)


def _conv_kernel(b_ref, c_ref, u_ref, cp_ref, up_ref, w_ref, cb_ref, g_ref, o_ref, buf_ref):
    si = pl.program_id(1)
    ts = o_ref.shape[1]
    cu = c_ref[0].astype(F32) * u_ref[0].astype(F32)
    prev = cp_ref[0].astype(F32) * up_ref[0].astype(F32)
    buf_ref[0:SUBLANES, :] = jnp.where(si > 0, prev, 0.0)
    buf_ref[SUBLANES:, :] = cu
    conv = (w_ref[0:1, :] * buf_ref[pl.ds(SUBLANES - 2, ts), :]
            + w_ref[1:2, :] * buf_ref[pl.ds(SUBLANES - 1, ts), :]
            + w_ref[2:3, :] * cu)
    z = b_ref[0].astype(F32) * (conv + cb_ref[...])
    for c in range(z.shape[1] // CONV_GROUP_DIM):
        sl = slice(c * CONV_GROUP_DIM, (c + 1) * CONV_GROUP_DIM)
        zc = z[:, sl]
        ms = jnp.mean(zc * zc, axis=-1, keepdims=True)
        o_ref[0, :, sl] = (zc * lax.rsqrt(ms + EPS) * g_ref[:, sl]).astype(o_ref.dtype)


def _short_conv(proj3, conv_w, conv_b, g_conv):
    b, s, n = proj3.shape
    dc = conv_w.shape[1]
    ts = min(CONV_TS, s)
    col0 = (n - 3 * dc) // dc
    halo = ts // SUBLANES

    def main(col):
        return pl.BlockSpec((1, ts, dc), lambda bi, si: (bi, si, col))

    def prev(col):
        return pl.BlockSpec((1, SUBLANES, dc),
                            lambda bi, si: (bi, jnp.maximum(si * halo - 1, 0), col))

    return pl.pallas_call(
        _conv_kernel,
        grid=(b, s // ts),
        in_specs=[main(col0), main(col0 + 1), main(col0 + 2), prev(col0 + 1), prev(col0 + 2),
                  pl.BlockSpec((CONV_WIDTH, dc), lambda bi, si: (0, 0)),
                  pl.BlockSpec((1, dc), lambda bi, si: (0, 0)),
                  pl.BlockSpec((1, dc), lambda bi, si: (0, 0))],
        out_specs=pl.BlockSpec((1, ts, dc), lambda bi, si: (bi, si, 0)),
        out_shape=jax.ShapeDtypeStruct((b, s, dc), BF16),
        scratch_shapes=[pltpu.VMEM((ts + SUBLANES, dc), F32)],
        compiler_params=_cparams(("parallel", "parallel")),
        name="short_conv",
    )(proj3, proj3, proj3, proj3, proj3, conv_w, conv_b, g_conv)


def _out_proj_kernel(x_ref, a_ref, c_ref, wa_ref, wc_ref, g_ref, wr_ref, br_ref, tri_ref,
                     h_ref, hn_ref, mi_ref, wcol_ref, cnt_ref, carry_ref):
    i = pl.program_id(0)
    tm = x_ref.shape[0]

    @pl.when(i == 0)
    def _():
        carry_ref[...] = jnp.zeros_like(carry_ref)

    h = (x_ref[...]
         + jnp.dot(a_ref[...], wa_ref[...], preferred_element_type=F32)
         + jnp.dot(c_ref[...], wc_ref[...], preferred_element_type=F32))
    h_ref[...] = h
    ms = jnp.mean(h * h, axis=-1, keepdims=True)
    hn = h * lax.rsqrt(ms + EPS) * g_ref[...]
    hn_hi = hn.astype(BF16)
    _store_routed_rows(hn_ref, hn)
    hn_lo = (hn - hn_hi.astype(F32)).astype(BF16)

    r_hi = jnp.dot(hn_hi, wr_ref[...], preferred_element_type=F32)
    r_lo = jnp.dot(hn_lo, wr_ref[...], preferred_element_type=F32)
    logits = (r_hi[:, :ROUTER_COLS] + r_hi[:, ROUTER_COLS:] + r_lo[:, :ROUTER_COLS]
              + br_ref[...])
    lt = logits.T

    e = EXPERTS_PER_GROUP
    row = lax.broadcasted_iota(jnp.int32, (e, tm), 0)

    def first_argmax(v):
        vmax = jnp.max(v, axis=0, keepdims=True)
        idx = jnp.min(jnp.where(v == vmax, row, e), axis=0, keepdims=True)
        return vmax, idx

    gl = lt[0:e]
    gmax, g = first_argmax(gl)
    p_g = 1.0 / jnp.sum(jnp.exp(gl - gmax), axis=0, keepdims=True)
    ing = lt[e:2 * e]
    for gi in range(1, N_GROUPS):
        ing = jnp.where(g == gi, lt[(gi + 1) * e:(gi + 2) * e], ing)
    v1, i1 = first_argmax(ing)
    v2, i2 = first_argmax(jnp.where(row == i1, -jnp.inf, ing))
    ex = jnp.exp(v2 - v1)
    w1 = p_g / (1.0 + ex)
    w2 = w1 * ex
    e1 = g * e + i1
    e2 = g * e + i2

    erow = lax.broadcasted_iota(jnp.int32, (N_EXPERTS, tm), 0)
    oh1 = (erow == e1).astype(F32)
    oh2 = (erow == e2).astype(F32)
    oh = jnp.concatenate([oh1, oh2], axis=0).astype(BF16)
    pre = jnp.dot(oh, tri_ref[...], preferred_element_type=F32)
    cnt1 = jnp.sum(oh1, axis=1, keepdims=True)
    cnt2 = jnp.sum(oh2, axis=1, keepdims=True)
    carry = carry_ref[:, 0:1]
    r1 = jnp.sum(oh1 * (pre[:N_EXPERTS] + carry), axis=0, keepdims=True)
    r2 = jnp.sum(oh2 * (pre[N_EXPERTS:] + carry + cnt1), axis=0, keepdims=True)
    new_carry = carry + cnt1 + cnt2
    carry_ref[...] = jnp.broadcast_to(new_carry, carry_ref.shape)
    cnt_ref[...] = jnp.broadcast_to(new_carry, cnt_ref.shape)

    mi_ref[0] = jnp.concatenate(
        [e1, e2, r1.astype(jnp.int32), r2.astype(jnp.int32),
         jnp.zeros((SUBLANES - 4, tm), jnp.int32)], axis=0)
    wrow = jnp.concatenate([w1, w2, jnp.zeros((ROUTER_COLS - 2, tm), F32)], axis=0)
    wcol_ref[...] = wrow.T


def _out_proj(x2, attn2, conv2, wo_a, wo_c, g_ffn, wr, br, tri):
    t, d = x2.shape
    tm = min(OUT_PROJ_TM, t)
    nt = t // tm
    da, dc = attn2.shape[1], conv2.shape[1]
    slabs = d // (2 * LANES)
    const = lambda i: (0, 0)
    return pl.pallas_call(
        _out_proj_kernel,
        grid=(nt,),
        in_specs=[
            pl.BlockSpec((tm, d), lambda i: (i, 0)),
            pl.BlockSpec((tm, da), lambda i: (i, 0)),
            pl.BlockSpec((tm, dc), lambda i: (i, 0)),
            pl.BlockSpec((da, d), const),
            pl.BlockSpec((dc, d), const),
            pl.BlockSpec((1, d), const),
            pl.BlockSpec((d, 2 * ROUTER_COLS), const),
            pl.BlockSpec((1, ROUTER_COLS), const),
            pl.BlockSpec((tm, tm), const),
        ],
        out_specs=[
            pl.BlockSpec((tm, d), lambda i: (i, 0)),
            pl.BlockSpec((slabs, tm, LANES), lambda i: (0, i, 0)),
            pl.BlockSpec((1, SUBLANES, tm), lambda i: (i, 0, 0)),
            pl.BlockSpec((tm, ROUTER_COLS), lambda i: (i, 0)),
            pl.BlockSpec((N_EXPERTS, LANES), const),
        ],
        out_shape=[
            jax.ShapeDtypeStruct((t, d), F32),
            jax.ShapeDtypeStruct((slabs, t, LANES), jnp.uint32),
            jax.ShapeDtypeStruct((nt, SUBLANES, tm), jnp.int32),
            jax.ShapeDtypeStruct((t, ROUTER_COLS), F32),
            jax.ShapeDtypeStruct((N_EXPERTS, LANES), F32),
        ],
        scratch_shapes=[pltpu.VMEM((N_EXPERTS, LANES), F32)],
        compiler_params=_cparams(("arbitrary",)),
        name="out_proj_router",
    )(x2, attn2, conv2, wo_a, wo_c, g_ffn, wr, br, tri)


def _moe_kernel(te_ref, nv_ref, x_ref, w1_ref, w3_ref, w2_ref, o_ref, w13_ref, w2b_ref):
    i = pl.program_id(0)
    dff = w2_ref.shape[2]

    @pl.when(i < nv_ref[0])
    def _():
        @pl.when(jnp.logical_or(i == 0, te_ref[i] != te_ref[jnp.maximum(i - 1, 0)]))
        def _():
            w13_ref[:, :dff] = w1_ref[0, 0].astype(BF16)
            w13_ref[:, dff:] = w3_ref[0, 0].astype(BF16)
            w2b_ref[...] = w2_ref[0, 0].astype(BF16)

        x = _load_routed_rows(x_ref, BF16)
        ab = jnp.dot(x, w13_ref[...], preferred_element_type=F32)
        a, b = ab[:, :dff], ab[:, dff:]
        hid = (a * jax.nn.sigmoid(a) * b).astype(BF16)
        _store_routed_rows(o_ref, jnp.dot(hid, w2b_ref[...], preferred_element_type=F32))

    @pl.when(i >= nv_ref[0])
    def _():
        o_ref[...] = jnp.zeros_like(o_ref)


def _moe(xs, w1, w3, w2, layer, tile_expert, n_valid):
    d, dff = w2.shape[3], w2.shape[2]
    slabs = xs.shape[0]
    nt = xs.shape[1] // MOE_TM
    grid_spec = pltpu.PrefetchScalarGridSpec(
        num_scalar_prefetch=2,
        grid=(nt,),
        in_specs=[
            pl.BlockSpec((slabs, MOE_TM, LANES),
                         lambda i, te, nv: (0, jnp.minimum(i, nv[0] - 1), 0)),
            pl.BlockSpec((1, 1, d, dff), lambda i, te, nv: (layer, te[i], 0, 0)),
            pl.BlockSpec((1, 1, d, dff), lambda i, te, nv: (layer, te[i], 0, 0)),
            pl.BlockSpec((1, 1, dff, d), lambda i, te, nv: (layer, te[i], 0, 0)),
        ],
        out_specs=pl.BlockSpec((slabs, MOE_TM, LANES), lambda i, te, nv: (0, i, 0)),
        scratch_shapes=[pltpu.VMEM((d, 2 * dff), BF16), pltpu.VMEM((dff, d), BF16)],
    )
    return pl.pallas_call(
        _moe_kernel,
        grid_spec=grid_spec,
        out_shape=jax.ShapeDtypeStruct(xs.shape, jnp.uint32),
        compiler_params=_cparams(("arbitrary",)),
        name="moe_grouped",
    )(tile_expert, n_valid, xs, w1, w3, w2)


def _ple_kernel(h_ref, y1_ref, y2_ref, wcol_ref, p_ref, g_ref, wg_ref, wp_ref, *rest):
    o_ref = rest[-1]
    wcol = wcol_ref[...]
    h = (h_ref[...]
         + wcol[:, 0:1] * _load_routed_rows(y1_ref.at[0], F32)
         + wcol[:, 1:2] * _load_routed_rows(y2_ref.at[0], F32))
    ms = jnp.mean(h * h, axis=-1, keepdims=True)
    hn = (h * lax.rsqrt(ms + EPS) * g_ref[...]).astype(BF16)
    gate = jax.nn.sigmoid(jnp.dot(hn, wg_ref[...], preferred_element_type=F32))
    emb = jnp.dot(p_ref[...].astype(BF16), wp_ref[...], preferred_element_type=F32)
    o_ref[...] = h + gate * emb


def _ple(h1, yg, wcol, p2, g_ple, wg, wp, row0, out_prev):
    t, d = h1.shape
    n = yg.shape[2]
    tm = min(PLE_TM, n)
    dp = p2.shape[1]
    slabs = d // (2 * LANES)
    b0 = row0 // tm
    const = lambda i: (0, 0)
    row = lambda i: (b0 + i, 0)
    in_specs = [
        pl.BlockSpec((tm, d), row),
        pl.BlockSpec((1, slabs, tm, LANES), lambda i: (0, 0, i, 0)),
        pl.BlockSpec((1, slabs, tm, LANES), lambda i: (1, 0, i, 0)),
        pl.BlockSpec((tm, ROUTER_COLS), row),
        pl.BlockSpec((tm, dp), row),
        pl.BlockSpec((1, d), const),
        pl.BlockSpec((d, d), const),
        pl.BlockSpec((dp, d), const),
    ]
    args = [h1, yg, yg, wcol, p2, g_ple, wg, wp]
    aliases = {}
    if out_prev is not None:
        in_specs.append(pl.BlockSpec(memory_space=pl.ANY))
        aliases = {len(args): 0}
        args.append(out_prev)
    return pl.pallas_call(
        _ple_kernel,
        grid=(n // tm,),
        in_specs=in_specs,
        out_specs=pl.BlockSpec((tm, d), row),
        out_shape=jax.ShapeDtypeStruct((t, d), F32),
        input_output_aliases=aliases,
        compiler_params=_cparams(("parallel",)),
        name="combine_ple",
    )(*args)


def kernel(x, p, rel_bias, g_mix, w_in, g_q, g_k, lam_q1, lam_k1, lam_q2, lam_k2, g_subln,
           conv_w, conv_b, g_conv, w_o, g_ffn, w_group, b_group, w_expert, b_expert,
           w1, w3, w2, g_ple, w_ple_gate, w_ple_proj):
    depth = g_mix.shape[0]
    assert depth == 1
    li = 0
    b, s, d = x.shape
    t = b * s
    d_attn = N_DIFF_HEADS * DIFF_V_DIM
    x2 = x.reshape(t, d)

    n_groups_qk = d_attn // DIFF_QK_DIM
    gqk = jnp.concatenate([jnp.tile(g_q[li] * (DIFF_QK_DIM ** -0.5 * LOG2E), n_groups_qk),
                           jnp.tile(g_k[li], n_groups_qk)])[None, :].astype(F32)
    blk = jnp.arange(MXU_WIDTH) // DIFF_QK_DIM
    gsum = (blk[:, None] == blk[None, :]).astype(BF16)
    lam = (jnp.exp(jnp.sum(lam_q1[li] * lam_k1[li])) - jnp.exp(jnp.sum(lam_q2[li] * lam_k2[li]))
           + LAM_INIT).reshape(1).astype(F32)
    gsub = (g_subln[li] * (1.0 - LAM_INIT))[None, :].astype(F32)
    bias_tiles = _bias_tiles(rel_bias, ATTN_TQ, ATTN_TK)

    pad_g = EXPERTS_PER_GROUP - N_GROUPS
    pad_e = ROUTER_COLS - EXPERTS_PER_GROUP - N_EXPERTS
    wr_f32 = jnp.concatenate([w_group[li], jnp.zeros((d, pad_g), F32),
                              w_expert[li], jnp.zeros((d, pad_e), F32)], axis=1)
    wr_hi = wr_f32.astype(BF16)
    wr_lo = (wr_f32 - wr_hi.astype(F32)).astype(BF16)
    wr = jnp.concatenate([wr_hi, wr_lo], axis=1)
    br = jnp.concatenate([b_group[li], jnp.full((pad_g,), MASK_VALUE, F32),
                          b_expert[li], jnp.zeros((pad_e,), F32)])[None, :]
    tm_r = min(OUT_PROJ_TM, t)
    ar = jnp.arange(tm_r)
    tri = (ar[:, None] < ar[None, :]).astype(BF16)

    w_in_bf = w_in[li].astype(BF16)
    wo_bf = w_o[li].astype(BF16)
    wg_bf = w_ple_gate[li].astype(BF16)
    wp_bf = w_ple_proj[li].astype(BF16)

    n_qk = 2 * d_attn
    qk = _in_proj(x2, g_mix[li][None, :], w_in_bf, 0, n_qk, gqk, gsum)
    vbcu = _in_proj(x2, g_mix[li][None, :], w_in_bf, n_qk, w_in_bf.shape[1] - n_qk)
    vbcu3 = vbcu.reshape(b, s, -1)
    attn = _attention(qk.reshape(b, s, -1), vbcu3, bias_tiles, gsub, lam)
    conv = _short_conv(vbcu3, conv_w[li], conv_b[li][None, :], g_conv[li][None, :])
    h1, hn, meta_i, wcol, counts = _out_proj(
        x2, attn.reshape(t, -1), conv.reshape(t, -1), wo_bf[:d_attn], wo_bf[d_attn:],
        g_ffn[li][None, :], wr, br, tri)

    eid = jnp.transpose(meta_i[:, 0:2, :], (1, 0, 2)).reshape(2, t)
    rank = jnp.transpose(meta_i[:, 2:4, :], (1, 0, 2)).reshape(2, t)
    cnt = counts[:, 0].astype(jnp.int32)
    tiles_per = (cnt + MOE_TM - 1) // MOE_TM
    experts = jnp.arange(N_EXPERTS, dtype=jnp.int32)
    tile_end = jnp.sum(jnp.where(experts[:, None] <= experts[None, :], tiles_per[:, None], 0),
                       axis=0)
    row_start = (tile_end - tiles_per) * MOE_TM
    n_tiles = (TOP_K * t) // MOE_TM + N_EXPERTS
    pos = rank + jnp.sum(jnp.where(eid[..., None] == experts, row_start, 0), axis=-1)
    tile_expert = jnp.minimum(
        jnp.sum((jnp.arange(n_tiles, dtype=jnp.int32)[:, None] >= tile_end[None, :])
                .astype(jnp.int32), axis=1),
        N_EXPERTS - 1)
    n_valid = tile_end[-1:]

    slabs = d // (2 * LANES)
    n_slots = n_tiles * MOE_TM
    slab_base = jnp.arange(slabs, dtype=jnp.int32) * n_slots
    row_idx = pos[:, None, :] + slab_base[None, :, None]

    xs = _sc_scatter_rows(hn.reshape(slabs * t, LANES), row_idx[0].reshape(1, -1),
                          row_idx[1].reshape(1, -1), slabs * n_slots)
    ys = _moe(xs.reshape(slabs, n_slots, LANES), w1, w3, w2, li, tile_expert, n_valid)
    ys_rows = ys.reshape(slabs * n_slots, LANES)
    p2 = p[li].reshape(t, -1)
    tc = t // COMBINE_CHUNKS
    out = None
    for c in range(COMBINE_CHUNKS):
        idx_c = row_idx[:, :, c * tc:(c + 1) * tc].reshape(1, -1)
        yg = _sc_gather_rows(ys_rows, idx_c).reshape(TOP_K, slabs, tc, LANES)
        out = _ple(h1, yg, wcol, p2, g_ple[li][None, :], wg_bf, wp_bf, c * tc, out)
    return out.reshape(b, s, d)
```

```python
import functools
import math

import jax
import jax.numpy as jnp
from jax import lax
from jax.experimental import pallas as pl
from jax.experimental.pallas import tpu as pltpu
from jax.experimental.pallas import tpu_sc as plsc

F32 = jnp.float32
BF16 = jnp.bfloat16

N_DIFF_HEADS = 8
DIFF_QK_DIM = 64
DIFF_V_DIM = 128
CHUNK = 64
NUM_BUCKETS = 32
MAX_DISTANCE = 128
CONV_WIDTH = 3
CONV_GROUP_DIM = 128
N_GROUPS = 4
EXPERTS_PER_GROUP = 8
N_EXPERTS = N_GROUPS * EXPERTS_PER_GROUP
TOP_K = 2
EPS = 1e-6
MASK_VALUE = -1e30
LAM_INIT = 0.8 - 0.6 * math.exp(-0.3 * 0)
LOG2E = math.log2(math.e)

LANES = 128
SUBLANES = 8
MXU_WIDTH = 256
VMEM_LIMIT_BYTES = 56 * 1024 * 1024

IN_PROJ_TM = 1024
IN_PROJ_TN = 1024
ATTN_TQ = 512
ATTN_TK = 256
CONV_TS = 512
OUT_PROJ_TM = 512
MOE_TM = 256
PLE_TM = 512
COMBINE_CHUNKS = 4
ROUTER_COLS = 128
SC_WINDOW = 128
SC_NUM_CORES = 2
SC_NUM_SUBCORES = 16


def _cparams(semantics):
    return pltpu.CompilerParams(dimension_semantics=semantics,
                                vmem_limit_bytes=VMEM_LIMIT_BYTES)


def _store_routed_rows(o_ref, x):
    d = x.shape[1]
    half = d // 2
    slabs = half // LANES
    xr = x.astype(BF16).astype(F32)
    lo = lax.bitcast_convert_type(xr[:, :half], jnp.uint32)
    hi = lax.bitcast_convert_type(xr[:, half:], jnp.uint32)
    packed = (lo >> 16) | (hi & jnp.uint32(0xFFFF0000))
    for r in range(slabs):
        o_ref[r] = packed[:, r * LANES:(r + 1) * LANES]


def _load_routed_rows(x_ref, dtype):
    parts = [x_ref[r] for r in range(x_ref.shape[0])]
    lo = [lax.bitcast_convert_type(w << 16, F32).astype(dtype) for w in parts]
    hi = [lax.bitcast_convert_type(w & jnp.uint32(0xFFFF0000), F32).astype(dtype) for w in parts]
    return jnp.concatenate(lo + hi, axis=1)


def _sc_mesh():
    return plsc.VectorSubcoreMesh(core_axis_name="core", subcore_axis_name="subcore",
                                  num_cores=SC_NUM_CORES, num_subcores=SC_NUM_SUBCORES)


def _sc_scatter_rows(rows, idx_a, idx_b, n_out_rows):
    n_src = rows.shape[0]

    @functools.partial(pl.kernel, mesh=_sc_mesh(), scratch_types=[],
                       out_type=jax.ShapeDtypeStruct((n_out_rows, LANES), rows.dtype))
    def scatter(x_hbm, ia_hbm, ib_hbm, o_hbm):
        def body(x_vmem, ia_vmem, ib_vmem):
            pltpu.sync_copy(x_vmem, o_hbm.at[ia_vmem.at[0]])
            pltpu.sync_copy(x_vmem, o_hbm.at[ib_vmem.at[0]])

        pltpu.emit_pipeline(
            body,
            grid=(n_src // SC_WINDOW,),
            in_specs=[pl.BlockSpec((SC_WINDOW, LANES), lambda i: (i, 0)),
                      pl.BlockSpec((1, SC_WINDOW), lambda i: (0, i)),
                      pl.BlockSpec((1, SC_WINDOW), lambda i: (0, i))],
            out_specs=[],
            core_axis_name=("core", "subcore"),
            dimension_semantics=(pltpu.PARALLEL,),
        )(x_hbm, ia_hbm, ib_hbm)

    return scatter(rows, idx_a, idx_b)


def _sc_gather_rows(table, idx):
    n_idx = idx.shape[1]

    @functools.partial(pl.kernel, mesh=_sc_mesh(), scratch_types=[],
                       out_type=jax.ShapeDtypeStruct((n_idx, LANES), table.dtype))
    def gather(t_hbm, i_hbm, o_hbm):
        def body(i_vmem, o_vmem):
            pltpu.sync_copy(t_hbm.at[i_vmem.at[0]], o_vmem)

        pltpu.emit_pipeline(
            body,
            grid=(n_idx // SC_WINDOW,),
            in_specs=[pl.BlockSpec((1, SC_WINDOW), lambda i: (0, i))],
            out_specs=[pl.BlockSpec((SC_WINDOW, LANES), lambda i: (i, 0))],
            core_axis_name=("core", "subcore"),
            dimension_semantics=(pltpu.PARALLEL,),
        )(i_hbm, o_hbm)

    return gather(table, idx)


def _in_proj_kernel(x_ref, g_ref, w_ref, *rest, qk_norm):
    if qk_norm:
        gqk_ref, gsum_ref, o_ref, xn_ref = rest
    else:
        o_ref, xn_ref = rest

    @pl.when(pl.program_id(1) == 0)
    def _():
        x = x_ref[...]
        ms = jnp.mean(x * x, axis=-1, keepdims=True)
        xn_ref[...] = (x * lax.rsqrt(ms + EPS) * g_ref[...]).astype(BF16)

    acc = jnp.dot(xn_ref[...], w_ref[...], preferred_element_type=F32)
    if not qk_norm:
        o_ref[...] = acc.astype(o_ref.dtype)
        return
    for c in range(acc.shape[1] // MXU_WIDTH):
        sl = slice(c * MXU_WIDTH, (c + 1) * MXU_WIDTH)
        a = acc[:, sl]
        sq = a * a
        hi = sq.astype(BF16)
        lo = (sq - hi.astype(F32)).astype(BF16)
        ss = (jnp.dot(hi, gsum_ref[...], preferred_element_type=F32)
              + jnp.dot(lo, gsum_ref[...], preferred_element_type=F32))
        y = a * lax.rsqrt(ss * (1.0 / DIFF_QK_DIM) + EPS)
        o_ref[:, sl] = (y * gqk_ref[:, sl]).astype(o_ref.dtype)


def _in_proj(x2, g_mix, w_in_bf, col0, n_out, gqk=None, gsum=None):
    t, d = x2.shape
    tm, tn = min(IN_PROJ_TM, t), IN_PROJ_TN
    qk_norm = gqk is not None
    jb0 = col0 // tn
    in_specs = [
        pl.BlockSpec((tm, d), lambda i, j: (i, 0)),
        pl.BlockSpec((1, d), lambda i, j: (0, 0)),
        pl.BlockSpec((d, tn), lambda i, j: (0, jb0 + j)),
    ]
    args = [x2, g_mix, w_in_bf]
    if qk_norm:
        in_specs += [pl.BlockSpec((1, tn), lambda i, j: (0, j)),
                     pl.BlockSpec((MXU_WIDTH, MXU_WIDTH), lambda i, j: (0, 0))]
        args += [gqk, gsum]
    return pl.pallas_call(
        functools.partial(_in_proj_kernel, qk_norm=qk_norm),
        grid=(t // tm, n_out // tn),
        in_specs=in_specs,
        out_specs=pl.BlockSpec((tm, tn), lambda i, j: (i, j)),
        out_shape=jax.ShapeDtypeStruct((t, n_out), BF16),
        scratch_shapes=[pltpu.VMEM((tm, d), BF16)],
        compiler_params=_cparams(("parallel", "arbitrary")),
        name="in_proj_qk" if qk_norm else "in_proj_vbcu",
    )(*args)


def _attn_kernel(lam_ref, q_ref, k_ref, v_ref, bias_ref, gsub_ref, o_ref,
                 qs_ref, acc_ref, s0_ref, s1_ref, p0_ref, p1_ref, m0_ref, m1_ref, *, tq, tk):
    qi = pl.program_id(2)
    dv = DIFF_V_DIM
    s_bufs, p_bufs, m_bufs = (s0_ref, s1_ref), (p0_ref, p1_ref), (m0_ref, m1_ref)

    q = q_ref[0]
    lane = lax.broadcasted_iota(jnp.int32, q.shape, 1)
    zero = jnp.zeros_like(q)
    qs_ref[0:tq, :] = jnp.where(lane < DIFF_QK_DIM, q, zero)
    qs_ref[tq:, :] = jnp.where(lane < DIFF_QK_DIM, zero, q)
    ones = jnp.ones((tk, dv), BF16)

    def block_start(step):
        blk = jnp.where(step < 2, 2 * qi + jnp.maximum(step, 0), step - 2)
        return pl.multiple_of(blk * tk, tk)

    def logits_stage(step, dst):
        dst[...] = lax.dot_general(qs_ref[...], k_ref[0, pl.ds(block_start(step), tk), :],
                                   (((1,), (1,)), ((), ())), preferred_element_type=F32)

    def pv_stage(step, buf, first=False):
        vx = jnp.concatenate([v_ref[0, pl.ds(block_start(step), tk), :], ones], axis=1)
        pv = jnp.dot(p_bufs[buf][...], vx, preferred_element_type=F32)
        if first:
            acc_ref[...] = pv
        else:
            alpha = jnp.exp2(m_bufs[1 - buf][...] - m_bufs[buf][...])
            acc_ref[...] = jnp.concatenate([alpha, alpha], axis=1) * acc_ref[...] + pv

    def softmax_stage(buf, bias_tile, first=False):
        s = s_bufs[buf][...]
        if bias_tile is not None:
            bias = bias_ref[0, bias_tile]
            s = jnp.concatenate([s[:tq] + bias, s[tq:] + bias], axis=0)
        row_max = jnp.max(s, axis=-1, keepdims=True)
        if first:
            m_new = jnp.broadcast_to(row_max, m_bufs[buf].shape)
        else:
            m_new = jnp.maximum(m_bufs[1 - buf][...], row_max)
        m_bufs[buf][...] = m_new
        p = jnp.exp2(s - jnp.concatenate([m_new] * (tk // LANES), axis=1))
        p_bufs[buf][...] = p.astype(BF16)

    def pair(step0, bias0, bias1, lookahead=True, first=False):
        logits_stage(step0 + 1, s_bufs[1])
        if not first:
            pv_stage(step0 - 1, 1)
        softmax_stage(0, bias0, first)
        if lookahead:
            logits_stage(step0 + 2, s_bufs[0])
        pv_stage(step0, 0, first)
        softmax_stage(1, bias1)

    logits_stage(0, s_bufs[0])
    pair(0, 0, 1, first=True)

    n_far_pairs = jnp.maximum(qi - 1, 0)

    def far_quad(jj, carry):
        pair(4 * jj + 2, None, None)
        pair(4 * jj + 4, None, None)
        return carry

    lax.fori_loop(0, n_far_pairs // 2, far_quad, 0)

    @pl.when(n_far_pairs % 2 == 1)
    def _():
        pair(2 * n_far_pairs, None, None)

    @pl.when(qi > 0)
    def _():
        pair(2 * qi, None, 2, lookahead=False)

    pv_stage(2 * qi + 1, 1)

    acc = acc_ref[...]
    o = (acc[:tq, :dv] / acc[:tq, dv:]) - lam_ref[0] * (acc[tq:, :dv] / acc[tq:, dv:])
    ms = jnp.mean(o * o, axis=-1, keepdims=True)
    o_ref[0] = (o * lax.rsqrt(ms + EPS) * gsub_ref[...]).astype(o_ref.dtype)


def _attention(qk3, vbcu3, bias_tiles, gsub, lam):
    b, s, _ = qk3.shape
    h = N_DIFF_HEADS
    tq, tk = ATTN_TQ, ATTN_TK
    assert tq == 2 * tk and s % tq == 0
    kern = functools.partial(_attn_kernel, tq=tq, tk=tk)
    return pl.pallas_call(
        kern,
        grid=(b, h, s // tq),
        in_specs=[
            pl.BlockSpec(memory_space=pltpu.SMEM),
            pl.BlockSpec((1, tq, LANES), lambda bi, hi, qi: (bi, qi, hi)),
            pl.BlockSpec((1, s, LANES), lambda bi, hi, qi: (bi, 0, h + hi)),
            pl.BlockSpec((1, s, LANES), lambda bi, hi, qi: (bi, 0, hi)),
            pl.BlockSpec((1, 3, tq, tk), lambda bi, hi, qi: (hi, 0, 0, 0)),
            pl.BlockSpec((1, LANES), lambda bi, hi, qi: (0, 0)),
        ],
        out_specs=pl.BlockSpec((1, tq, LANES), lambda bi, hi, qi: (bi, qi, hi)),
        out_shape=jax.ShapeDtypeStruct((b, s, h * DIFF_V_DIM), BF16),
        scratch_shapes=[pltpu.VMEM((2 * tq, LANES), BF16),
                        pltpu.VMEM((2 * tq, 2 * DIFF_V_DIM), F32),
                        pltpu.VMEM((2 * tq, tk), F32),
                        pltpu.VMEM((2 * tq, tk), F32),
                        pltpu.VMEM((2 * tq, tk), BF16),
                        pltpu.VMEM((2 * tq, tk), BF16),
                        pltpu.VMEM((2 * tq, LANES), F32),
                        pltpu.VMEM((2 * tq, LANES), F32)],
        compiler_params=_cparams(("parallel", "parallel", "arbitrary")),
        name="diff_attention",
    )(lam, qk3, qk3, vbcu3, bias_tiles, gsub)


def _rel_bucket(rel):
    nb = NUM_BUCKETS // 2
    max_exact = nb // 2
    n = jnp.abs(rel)
    n2 = n * n
    large = max_exact + sum((n2 >= (max_exact * max_exact) * (2 ** k)).astype(jnp.int32)
                            for k in range(1, nb - max_exact))
    return jnp.where(rel > 0, nb, 0) + jnp.where(n < max_exact, n, large)


def _bias_kernel(rb_ref, o_ref, *, tk):
    h = pl.program_id(0)
    tq = o_ref.shape[2]
    qpos = lax.broadcasted_iota(jnp.int32, (tq, tk), 0)
    kcol = lax.broadcasted_iota(jnp.int32, (tq, tk), 1)
    chunk_shift = CHUNK.bit_length() - 1
    far_bias = rb_ref[NUM_BUCKETS // 2 - 1, h]
    for tile, offset in enumerate((0, tk, -tk)):
        kpos = kcol + offset
        bucket = _rel_bucket(kpos - qpos)
        bias = jnp.zeros((tq, tk), F32)
        for b in range(NUM_BUCKETS):
            bias = jnp.where(bucket == b, rb_ref[b, h], bias)
        mask = (kpos >> chunk_shift) <= (qpos >> chunk_shift)
        o_ref[0, tile] = jnp.where(mask, (bias - far_bias) * LOG2E, MASK_VALUE)


def _bias_tiles(rel_bias, tq, tk):
    assert tk >= MAX_DISTANCE
    assert CHUNK & (CHUNK - 1) == 0
    n_heads = rel_bias.shape[1]
    return pl.pallas_call(
        functools.partial(_bias_kernel, tk=tk),
        grid=(n_heads,),
        in_specs=[pl.BlockSpec(memory_space=pltpu.SMEM)],
        out_specs=pl.BlockSpec((1, 3, tq, tk), lambda h: (h, 0, 0, 0)),
        out_shape=jax.ShapeDtypeStruct((n_heads, 3, tq, tk), F32),
        compiler_params=_cparams(("parallel",)),
        name="bias_tiles",
    )(rel_bias)


def _conv_kernel(b_ref, c_ref, u_ref, cp_ref, up_ref, w_ref, cb_ref, g_ref, o_ref, buf_ref):
    si = pl.program_id(1)
    ts = o_ref.shape[1]
    cu = c_ref[0].astype(F32) * u_ref[0].astype(F32)
    prev = cp_ref[0].astype(F32) * up_ref[0].astype(F32)
    buf_ref[0:SUBLANES, :] = jnp.where(si > 0, prev, 0.0)
    buf_ref[SUBLANES:, :] = cu
    conv = (w_ref[0:1, :] * buf_ref[pl.ds(SUBLANES - 2, ts), :]
            + w_ref[1:2, :] * buf_ref[pl.ds(SUBLANES - 1, ts), :]
            + w_ref[2:3, :] * cu)
    z = b_ref[0].astype(F32) * (conv + cb_ref[...])
    for c in range(z.shape[1] // CONV_GROUP_DIM):
        sl = slice(c * CONV_GROUP_DIM, (c + 1) * CONV_GROUP_DIM)
        zc = z[:, sl]
        ms = jnp.mean(zc * zc, axis=-1, keepdims=True)
        o_ref[0, :, sl] = (zc * lax.rsqrt(ms + EPS) * g_ref[:, sl]).astype(o_ref.dtype)


def _short_conv(proj3, conv_w, conv_b, g_conv):
    b, s, n = proj3.shape
    dc = conv_w.shape[1]
    ts = min(CONV_TS, s)
    col0 = (n - 3 * dc) // dc
    halo = ts // SUBLANES

    def main(col):
        return pl.BlockSpec((1, ts, dc), lambda bi, si: (bi, si, col))

    def prev(col):
        return pl.BlockSpec((1, SUBLANES, dc),
                            lambda bi, si: (bi, jnp.maximum(si * halo - 1, 0), col))

    return pl.pallas_call(
        _conv_kernel,
        grid=(b, s // ts),
        in_specs=[main(col0), main(col0 + 1), main(col0 + 2), prev(col0 + 1), prev(col0 + 2),
                  pl.BlockSpec((CONV_WIDTH, dc), lambda bi, si: (0, 0)),
                  pl.BlockSpec((1, dc), lambda bi, si: (0, 0)),
                  pl.BlockSpec((1, dc), lambda bi, si: (0, 0))],
        out_specs=pl.BlockSpec((1, ts, dc), lambda bi, si: (bi, si, 0)),
        out_shape=jax.ShapeDtypeStruct((b, s, dc), BF16),
        scratch_shapes=[pltpu.VMEM((ts + SUBLANES, dc), F32)],
        compiler_params=_cparams(("parallel", "parallel")),
        name="short_conv",
    )(proj3, proj3, proj3, proj3, proj3, conv_w, conv_b, g_conv)


def _out_proj_kernel(x_ref, a_ref, c_ref, wa_ref, wc_ref, g_ref, wr_ref, br_ref, tri_ref,
                     h_ref, hn_ref, mi_ref, wcol_ref, cnt_ref, carry_ref):
    i = pl.program_id(0)
    tm = x_ref.shape[0]

    @pl.when(i == 0)
    def _():
        carry_ref[...] = jnp.zeros_like(carry_ref)

    h = (x_ref[...]
         + jnp.dot(a_ref[...], wa_ref[...], preferred_element_type=F32)
         + jnp.dot(c_ref[...], wc_ref[...], preferred_element_type=F32))
    h_ref[...] = h
    ms = jnp.mean(h * h, axis=-1, keepdims=True)
    hn = h * lax.rsqrt(ms + EPS) * g_ref[...]
    hn_hi = hn.astype(BF16)
    _store_routed_rows(hn_ref, hn)
    hn_lo = (hn - hn_hi.astype(F32)).astype(BF16)

    r_hi = jnp.dot(hn_hi, wr_ref[...], preferred_element_type=F32)
    r_lo = jnp.dot(hn_lo, wr_ref[...], preferred_element_type=F32)
    logits = (r_hi[:, :ROUTER_COLS] + r_hi[:, ROUTER_COLS:] + r_lo[:, :ROUTER_COLS]
              + br_ref[...])
    lt = logits.T

    e = EXPERTS_PER_GROUP
    row = lax.broadcasted_iota(jnp.int32, (e, tm), 0)

    def first_argmax(v):
        vmax = jnp.max(v, axis=0, keepdims=True)
        idx = jnp.min(jnp.where(v == vmax, row, e), axis=0, keepdims=True)
        return vmax, idx

    gl = lt[0:e]
    gmax, g = first_argmax(gl)
    p_g = 1.0 / jnp.sum(jnp.exp(gl - gmax), axis=0, keepdims=True)
    ing = lt[e:2 * e]
    for gi in range(1, N_GROUPS):
        ing = jnp.where(g == gi, lt[(gi + 1) * e:(gi + 2) * e], ing)
    v1, i1 = first_argmax(ing)
    v2, i2 = first_argmax(jnp.where(row == i1, -jnp.inf, ing))
    ex = jnp.exp(v2 - v1)
    w1 = p_g / (1.0 + ex)
    w2 = w1 * ex
    e1 = g * e + i1
    e2 = g * e + i2

    erow = lax.broadcasted_iota(jnp.int32, (N_EXPERTS, tm), 0)
    oh1 = (erow == e1).astype(F32)
    oh2 = (erow == e2).astype(F32)
    oh = jnp.concatenate([oh1, oh2], axis=0).astype(BF16)
    pre = jnp.dot(oh, tri_ref[...], preferred_element_type=F32)
    cnt1 = jnp.sum(oh1, axis=1, keepdims=True)
    cnt2 = jnp.sum(oh2, axis=1, keepdims=True)
    carry = carry_ref[:, 0:1]
    r1 = jnp.sum(oh1 * (pre[:N_EXPERTS] + carry), axis=0, keepdims=True)
    r2 = jnp.sum(oh2 * (pre[N_EXPERTS:] + carry + cnt1), axis=0, keepdims=True)
    new_carry = carry + cnt1 + cnt2
    carry_ref[...] = jnp.broadcast_to(new_carry, carry_ref.shape)
    cnt_ref[...] = jnp.broadcast_to(new_carry, cnt_ref.shape)

    mi_ref[0] = jnp.concatenate(
        [e1, e2, r1.astype(jnp.int32), r2.astype(jnp.int32),
         jnp.zeros((SUBLANES - 4, tm), jnp.int32)], axis=0)
    wrow = jnp.concatenate([w1, w2, jnp.zeros((ROUTER_COLS - 2, tm), F32)], axis=0)
    wcol_ref[...] = wrow.T


def _out_proj(x2, attn2, conv2, wo_a, wo_c, g_ffn, wr, br, tri):
    t, d = x2.shape
    tm = min(OUT_PROJ_TM, t)
    nt = t // tm
    da, dc = attn2.shape[1], conv2.shape[1]
    slabs = d // (2 * LANES)
    const = lambda i: (0, 0)
    return pl.pallas_call(
        _out_proj_kernel,
        grid=(nt,),
        in_specs=[
            pl.BlockSpec((tm, d), lambda i: (i, 0)),
            pl.BlockSpec((tm, da), lambda i: (i, 0)),
            pl.BlockSpec((tm, dc), lambda i: (i, 0)),
            pl.BlockSpec((da, d), const),
            pl.BlockSpec((dc, d), const),
            pl.BlockSpec((1, d), const),
            pl.BlockSpec((d, 2 * ROUTER_COLS), const),
            pl.BlockSpec((1, ROUTER_COLS), const),
            pl.BlockSpec((tm, tm), const),
        ],
        out_specs=[
            pl.BlockSpec((tm, d), lambda i: (i, 0)),
            pl.BlockSpec((slabs, tm, LANES), lambda i: (0, i, 0)),
            pl.BlockSpec((1, SUBLANES, tm), lambda i: (i, 0, 0)),
            pl.BlockSpec((tm, ROUTER_COLS), lambda i: (i, 0)),
            pl.BlockSpec((N_EXPERTS, LANES), const),
        ],
        out_shape=[
            jax.ShapeDtypeStruct((t, d), F32),
            jax.ShapeDtypeStruct((slabs, t, LANES), jnp.uint32),
            jax.ShapeDtypeStruct((nt, SUBLANES, tm), jnp.int32),
            jax.ShapeDtypeStruct((t, ROUTER_COLS), F32),
            jax.ShapeDtypeStruct((N_EXPERTS, LANES), F32),
        ],
        scratch_shapes=[pltpu.VMEM((N_EXPERTS, LANES), F32)],
        compiler_params=_cparams(("arbitrary",)),
        name="out_proj_router",
    )(x2, attn2, conv2, wo_a, wo_c, g_ffn, wr, br, tri)


def _moe_kernel(te_ref, nv_ref, x_ref, w1_ref, w3_ref, w2_ref, o_ref, w13_ref, w2b_ref):
    i = pl.program_id(0)
    dff = w2_ref.shape[2]

    @pl.when(i < nv_ref[0])
    def _():
        @pl.when(jnp.logical_or(i == 0, te_ref[i] != te_ref[jnp.maximum(i - 1, 0)]))
        def _():
            w13_ref[:, :dff] = w1_ref[0, 0].astype(BF16)
            w13_ref[:, dff:] = w3_ref[0, 0].astype(BF16)
            w2b_ref[...] = w2_ref[0, 0].astype(BF16)

        x = _load_routed_rows(x_ref, BF16)
        ab = jnp.dot(x, w13_ref[...], preferred_element_type=F32)
        a, b = ab[:, :dff], ab[:, dff:]
        hid = (a * jax.nn.sigmoid(a) * b).astype(BF16)
        _store_routed_rows(o_ref, jnp.dot(hid, w2b_ref[...], preferred_element_type=F32))

    @pl.when(i >= nv_ref[0])
    def _():
        o_ref[...] = jnp.zeros_like(o_ref)


def _moe(xs, w1, w3, w2, layer, tile_expert, n_valid):
    d, dff = w2.shape[3], w2.shape[2]
    slabs = xs.shape[0]
    nt = xs.shape[1] // MOE_TM
    grid_spec = pltpu.PrefetchScalarGridSpec(
        num_scalar_prefetch=2,
        grid=(nt,),
        in_specs=[
            pl.BlockSpec((slabs, MOE_TM, LANES),
                         lambda i, te, nv: (0, jnp.minimum(i, nv[0] - 1), 0)),
            pl.BlockSpec((1, 1, d, dff), lambda i, te, nv: (layer, te[i], 0, 0)),
            pl.BlockSpec((1, 1, d, dff), lambda i, te, nv: (layer, te[i], 0, 0)),
            pl.BlockSpec((1, 1, dff, d), lambda i, te, nv: (layer, te[i], 0, 0)),
        ],
        out_specs=pl.BlockSpec((slabs, MOE_TM, LANES), lambda i, te, nv: (0, i, 0)),
        scratch_shapes=[pltpu.VMEM((d, 2 * dff), BF16), pltpu.VMEM((dff, d), BF16)],
    )
    return pl.pallas_call(
        _moe_kernel,
        grid_spec=grid_spec,
        out_shape=jax.ShapeDtypeStruct(xs.shape, jnp.uint32),
        compiler_params=_cparams(("arbitrary",)),
        name="moe_grouped",
    )(tile_expert, n_valid, xs, w1, w3, w2)


def _ple_kernel(h_ref, y1_ref, y2_ref, wcol_ref, p_ref, g_ref, wg_ref, wp_ref, *rest):
    o_ref = rest[-1]
    wcol = wcol_ref[...]
    h = (h_ref[...]
         + wcol[:, 0:1] * _load_routed_rows(y1_ref.at[0], F32)
         + wcol[:, 1:2] * _load_routed_rows(y2_ref.at[0], F32))
    ms = jnp.mean(h * h, axis=-1, keepdims=True)
    hn = (h * lax.rsqrt(ms + EPS) * g_ref[...]).astype(BF16)
    gate = jax.nn.sigmoid(jnp.dot(hn, wg_ref[...], preferred_element_type=F32))
    emb = jnp.dot(p_ref[...].astype(BF16), wp_ref[...], preferred_element_type=F32)
    o_ref[...] = h + gate * emb


def _ple(h1, yg, wcol, p2, g_ple, wg, wp, row0, out_prev):
    t, d = h1.shape
    n = yg.shape[2]
    tm = min(PLE_TM, n)
    dp = p2.shape[1]
    slabs = d // (2 * LANES)
    b0 = row0 // tm
    const = lambda i: (0, 0)
    row = lambda i: (b0 + i, 0)
    in_specs = [
        pl.BlockSpec((tm, d), row),
        pl.BlockSpec((1, slabs, tm, LANES), lambda i: (0, 0, i, 0)),
        pl.BlockSpec((1, slabs, tm, LANES), lambda i: (1, 0, i, 0)),
        pl.BlockSpec((tm, ROUTER_COLS), row),
        pl.BlockSpec((tm, dp), row),
        pl.BlockSpec((1, d), const),
        pl.BlockSpec((d, d), const),
        pl.BlockSpec((dp, d), const),
    ]
    args = [h1, yg, yg, wcol, p2, g_ple, wg, wp]
    aliases = {}
    if out_prev is not None:
        in_specs.append(pl.BlockSpec(memory_space=pl.ANY))
        aliases = {len(args): 0}
        args.append(out_prev)
    return pl.pallas_call(
        _ple_kernel,
        grid=(n // tm,),
        in_specs=in_specs,
        out_specs=pl.BlockSpec((tm, d), row),
        out_shape=jax.ShapeDtypeStruct((t, d), F32),
        input_output_aliases=aliases,
        compiler_params=_cparams(("parallel",)),
        name="combine_ple",
    )(*args)


def kernel(x, p, rel_bias, g_mix, w_in, g_q, g_k, lam_q1, lam_k1, lam_q2, lam_k2, g_subln,
           conv_w, conv_b, g_conv, w_o, g_ffn, w_group, b_group, w_expert, b_expert,
           w1, w3, w2, g_ple, w_ple_gate, w_ple_proj):
    depth = g_mix.shape[0]
    assert depth == 1
    li = 0
    b, s, d = x.shape
    t = b * s
    d_attn = N_DIFF_HEADS * DIFF_V_DIM
    x2 = x.reshape(t, d)

    n_groups_qk = d_attn // DIFF_QK_DIM
    gqk = jnp.concatenate([jnp.tile(g_q[li] * (DIFF_QK_DIM ** -0.5 * LOG2E), n_groups_qk),
                           jnp.tile(g_k[li], n_groups_qk)])[None, :].astype(F32)
    blk = jnp.arange(MXU_WIDTH) // DIFF_QK_DIM
    gsum = (blk[:, None] == blk[None, :]).astype(BF16)
    lam = (jnp.exp(jnp.sum(lam_q1[li] * lam_k1[li])) - jnp.exp(jnp.sum(lam_q2[li] * lam_k2[li]))
           + LAM_INIT).reshape(1).astype(F32)
    gsub = (g_subln[li] * (1.0 - LAM_INIT))[None, :].astype(F32)
    bias_tiles = _bias_tiles(rel_bias, ATTN_TQ, ATTN_TK)

    pad_g = EXPERTS_PER_GROUP - N_GROUPS
    pad_e = ROUTER_COLS - EXPERTS_PER_GROUP - N_EXPERTS
    wr_f32 = jnp.concatenate([w_group[li], jnp.zeros((d, pad_g), F32),
                              w_expert[li], jnp.zeros((d, pad_e), F32)], axis=1)
    wr_hi = wr_f32.astype(BF16)
    wr_lo = (wr_f32 - wr_hi.astype(F32)).astype(BF16)
    wr = jnp.concatenate([wr_hi, wr_lo], axis=1)
    br = jnp.concatenate([b_group[li], jnp.full((pad_g,), MASK_VALUE, F32),
                          b_expert[li], jnp.zeros((pad_e,), F32)])[None, :]
    tm_r = min(OUT_PROJ_TM, t)
    ar = jnp.arange(tm_r)
    tri = (ar[:, None] < ar[None, :]).astype(BF16)

    w_in_bf = w_in[li].astype(BF16)
    wo_bf = w_o[li].astype(BF16)
    wg_bf = w_ple_gate[li].astype(BF16)
    wp_bf = w_ple_proj[li].astype(BF16)

    n_qk = 2 * d_attn
    qk = _in_proj(x2, g_mix[li][None, :], w_in_bf, 0, n_qk, gqk, gsum)
    vbcu = _in_proj(x2, g_mix[li][None, :], w_in_bf, n_qk, w_in_bf.shape[1] - n_qk)
    vbcu3 = vbcu.reshape(b, s, -1)
    attn = _attention(qk.reshape(b, s, -1), vbcu3, bias_tiles, gsub, lam)
    conv = _short_conv(vbcu3, conv_w[li], conv_b[li][None, :], g_conv[li][None, :])
    h1, hn, meta_i, wcol, counts = _out_proj(
        x2, attn.reshape(t, -1), conv.reshape(t, -1), wo_bf[:d_attn], wo_bf[d_attn:],
        g_ffn[li][None, :], wr, br, tri)

    eid = jnp.transpose(meta_i[:, 0:2, :], (1, 0, 2)).reshape(2, t)
    rank = jnp.transpose(meta_i[:, 2:4, :], (1, 0, 2)).reshape(2, t)
    cnt = counts[:, 0].astype(jnp.int32)
    tiles_per = (cnt + MOE_TM - 1) // MOE_TM
    experts = jnp.arange(N_EXPERTS, dtype=jnp.int32)
    tile_end = jnp.sum(jnp.where(experts[:, None] <= experts[None, :], tiles_per[:, None], 0),
                       axis=0)
    row_start = (tile_end - tiles_per) * MOE_TM
    n_tiles = (TOP_K * t) // MOE_TM + N_EXPERTS
    pos = rank + jnp.sum(jnp.where(eid[..., None] == experts, row_start, 0), axis=-1)
    tile_expert = jnp.minimum(
        jnp.sum((jnp.arange(n_tiles, dtype=jnp.int32)[:, None] >= tile_end[None, :])
                .astype(jnp.int32), axis=1),
        N_EXPERTS - 1)
    n_valid = tile_end[-1:]

    slabs = d // (2 * LANES)
    n_slots = n_tiles * MOE_TM
    slab_base = jnp.arange(slabs, dtype=jnp.int32) * n_slots
    row_idx = pos[:, None, :] + slab_base[None, :, None]

    xs = _sc_scatter_rows(hn.reshape(slabs * t, LANES), row_idx[0].reshape(1, -1),
                          row_idx[1].reshape(1, -1), slabs * n_slots)
    ys = _moe(xs.reshape(slabs, n_slots, LANES), w1, w3, w2, li, tile_expert, n_valid)
    ys_rows = ys.reshape(slabs * n_slots, LANES)
    p2 = p[li].reshape(t, -1)
    tc = t // COMBINE_CHUNKS
    out = None
    for c in range(COMBINE_CHUNKS):
        idx_c = row_idx[:, :, c * tc:(c + 1) * tc].reshape(1, -1)
        yg = _sc_gather_rows(ys_rows, idx_c).reshape(TOP_K, slabs, tc, LANES)
        out = _ple(h1, yg, wcol, p2, g_ple[li][None, :], wg_bf, wp_bf, c * tc, out)
    return out.reshape(b, s, d)
```

```python
import functools
import math

import jax
import jax.numpy as jnp
from jax import lax
from jax.experimental import pallas as pl
from jax.experimental.pallas import tpu as pltpu
from jax.experimental.pallas import tpu_sc as plsc

F32 = jnp.float32
BF16 = jnp.bfloat16

N_DIFF_HEADS = 8
DIFF_QK_DIM = 64
DIFF_V_DIM = 128
CHUNK = 64
NUM_BUCKETS = 32
MAX_DISTANCE = 128
CONV_WIDTH = 3
CONV_GROUP_DIM = 128
N_GROUPS = 4
EXPERTS_PER_GROUP = 8
N_EXPERTS = N_GROUPS * EXPERTS_PER_GROUP
TOP_K = 2
EPS = 1e-6
MASK_VALUE = -1e30
LAM_INIT = 0.8 - 0.6 * math.exp(-0.3 * 0)
LOG2E = math.log2(math.e)

LANES = 128
SUBLANES = 8
MXU_WIDTH = 256
VMEM_LIMIT_BYTES = 56 * 1024 * 1024

IN_PROJ_TM = 1024
IN_PROJ_TN = 1024
ATTN_TQ = 512
ATTN_TK = 256
CONV_TS = 512
OUT_PROJ_TM = 512
MOE_TM = 256
PLE_TM = 512
COMBINE_CHUNKS = 4
ROUTER_COLS = 128
SC_WINDOW = 128
SC_NUM_CORES = 2
SC_NUM_SUBCORES = 16


def _cparams(semantics):
    return pltpu.CompilerParams(dimension_semantics=semantics,
                                vmem_limit_bytes=VMEM_LIMIT_BYTES)


def _store_routed_rows(o_ref, x):
    d = x.shape[1]
    half = d // 2
    slabs = half // LANES
    xr = x.astype(BF16).astype(F32)
    lo = lax.bitcast_convert_type(xr[:, :half], jnp.uint32)
    hi = lax.bitcast_convert_type(xr[:, half:], jnp.uint32)
    packed = (lo >> 16) | (hi & jnp.uint32(0xFFFF0000))
    for r in range(slabs):
        o_ref[r] = packed[:, r * LANES:(r + 1) * LANES]


def _load_routed_rows(x_ref, dtype):
    parts = [x_ref[r] for r in range(x_ref.shape[0])]
    lo = [lax.bitcast_convert_type(w << 16, F32).astype(dtype) for w in parts]
    hi = [lax.bitcast_convert_type(w & jnp.uint32(0xFFFF0000), F32).astype(dtype) for w in parts]
    return jnp.concatenate(lo + hi, axis=1)


def _sc_mesh():
    return plsc.VectorSubcoreMesh(core_axis_name="core", subcore_axis_name="subcore",
                                  num_cores=SC_NUM_CORES, num_subcores=SC_NUM_SUBCORES)


def _sc_scatter_rows(rows, idx_a, idx_b, n_out_rows):
    n_src = rows.shape[0]

    @functools.partial(pl.kernel, mesh=_sc_mesh(), scratch_types=[],
                       out_type=jax.ShapeDtypeStruct((n_out_rows, LANES), rows.dtype))
    def scatter(x_hbm, ia_hbm, ib_hbm, o_hbm):
        def body(x_vmem, ia_vmem, ib_vmem):
            pltpu.sync_copy(x_vmem, o_hbm.at[ia_vmem.at[0]])
            pltpu.sync_copy(x_vmem, o_hbm.at[ib_vmem.at[0]])

        pltpu.emit_pipeline(
            body,
            grid=(n_src // SC_WINDOW,),
            in_specs=[pl.BlockSpec((SC_WINDOW, LANES), lambda i: (i, 0)),
                      pl.BlockSpec((1, SC_WINDOW), lambda i: (0, i)),
                      pl.BlockSpec((1, SC_WINDOW), lambda i: (0, i))],
            out_specs=[],
            core_axis_name=("core", "subcore"),
            dimension_semantics=(pltpu.PARALLEL,),
        )(x_hbm, ia_hbm, ib_hbm)

    return scatter(rows, idx_a, idx_b)


def _sc_gather_rows(table, idx):
    n_idx = idx.shape[1]

    @functools.partial(pl.kernel, mesh=_sc_mesh(), scratch_types=[],
                       out_type=jax.ShapeDtypeStruct((n_idx, LANES), table.dtype))
    def gather(t_hbm, i_hbm, o_hbm):
        def body(i_vmem, o_vmem):
            pltpu.sync_copy(t_hbm.at[i_vmem.at[0]], o_vmem)

        pltpu.emit_pipeline(
            body,
            grid=(n_idx // SC_WINDOW,),
            in_specs=[pl.BlockSpec((1, SC_WINDOW), lambda i: (0, i))],
            out_specs=[pl.BlockSpec((SC_WINDOW, LANES), lambda i: (i, 0))],
            core_axis_name=("core", "subcore"),
            dimension_semantics=(pltpu.PARALLEL,),
        )(i_hbm, o_hbm)

    return gather(table, idx)


def _proj_kernel(xn_ref, w_ref, o_ref):
    o_ref[...] = jnp.dot(xn_ref[...], w_ref[...],
                         preferred_element_type=F32).astype(o_ref.dtype)


def _in_proj_qk_kernel(x_ref, g_ref, w_ref, gqk_ref, gsum_ref, o_ref, xn_ref):
    @pl.when(pl.program_id(1) == 0)
    def _():
        x = x_ref[...]
        ms = jnp.mean(x * x, axis=-1, keepdims=True)
        xn_ref[...] = (x * lax.rsqrt(ms + EPS) * g_ref[...]).astype(BF16)

    acc = jnp.dot(xn_ref[...], w_ref[...], preferred_element_type=F32)
    for c in range(acc.shape[1] // MXU_WIDTH):
        sl = slice(c * MXU_WIDTH, (c + 1) * MXU_WIDTH)
        a = acc[:, sl]
        sq = a * a
        hi = sq.astype(BF16)
        lo = (sq - hi.astype(F32)).astype(BF16)
        ss = (jnp.dot(hi, gsum_ref[...], preferred_element_type=F32)
              + jnp.dot(lo, gsum_ref[...], preferred_element_type=F32))
        y = a * lax.rsqrt(ss * (1.0 / DIFF_QK_DIM) + EPS)
        o_ref[:, sl] = (y * gqk_ref[:, sl]).astype(o_ref.dtype)


def _in_proj_qk(x2, g_mix, w_in_bf, n_out, gqk, gsum):
    t, d = x2.shape
    tm, tn = min(IN_PROJ_TM, t), IN_PROJ_TN
    return pl.pallas_call(
        _in_proj_qk_kernel,
        grid=(t // tm, n_out // tn),
        in_specs=[
            pl.BlockSpec((tm, d), lambda i, j: (i, 0)),
            pl.BlockSpec((1, d), lambda i, j: (0, 0)),
            pl.BlockSpec((d, tn), lambda i, j: (0, j)),
            pl.BlockSpec((1, tn), lambda i, j: (0, j)),
            pl.BlockSpec((MXU_WIDTH, MXU_WIDTH), lambda i, j: (0, 0)),
        ],
        out_specs=[pl.BlockSpec((tm, tn), lambda i, j: (i, j)),
                   pl.BlockSpec((tm, d), lambda i, j: (i, 0))],
        out_shape=[jax.ShapeDtypeStruct((t, n_out), BF16),
                   jax.ShapeDtypeStruct((t, d), BF16)],
        compiler_params=_cparams(("parallel", "arbitrary")),
        name="in_proj_qk",
    )(x2, g_mix, w_in_bf, gqk, gsum)


def _proj(xn, w_bf, col0, n_out):
    t, d = xn.shape
    tm, tn = min(IN_PROJ_TM, t), IN_PROJ_TN
    jb0 = col0 // tn
    return pl.pallas_call(
        _proj_kernel,
        grid=(t // tm, n_out // tn),
        in_specs=[pl.BlockSpec((tm, d), lambda i, j: (i, 0)),
                  pl.BlockSpec((d, tn), lambda i, j: (0, jb0 + j))],
        out_specs=pl.BlockSpec((tm, tn), lambda i, j: (i, j)),
        out_shape=jax.ShapeDtypeStruct((t, n_out), BF16),
        compiler_params=_cparams(("parallel", "parallel")),
        name="in_proj_vbcu",
    )(xn, w_bf)


def _attn_kernel(lam_ref, q_ref, k_ref, v_ref, bias_ref, gsub_ref, o_ref,
                 qs_ref, acc_ref, s0_ref, s1_ref, p0_ref, p1_ref, m0_ref, m1_ref, *, tq, tk):
    qi = pl.program_id(2)
    dv = DIFF_V_DIM
    s_bufs, p_bufs, m_bufs = (s0_ref, s1_ref), (p0_ref, p1_ref), (m0_ref, m1_ref)

    q = q_ref[0]
    lane = lax.broadcasted_iota(jnp.int32, q.shape, 1)
    zero = jnp.zeros_like(q)
    qs_ref[0:tq, :] = jnp.where(lane < DIFF_QK_DIM, q, zero)
    qs_ref[tq:, :] = jnp.where(lane < DIFF_QK_DIM, zero, q)
    ones = jnp.ones((tk, dv), BF16)

    def block_start(step):
        blk = jnp.where(step < 2, 2 * qi + jnp.maximum(step, 0), step - 2)
        return pl.multiple_of(blk * tk, tk)

    def logits_stage(step, dst):
        dst[...] = lax.dot_general(qs_ref[...], k_ref[0, pl.ds(block_start(step), tk), :],
                                   (((1,), (1,)), ((), ())), preferred_element_type=F32)

    def pv_stage(step, buf, first=False):
        vx = jnp.concatenate([v_ref[0, pl.ds(block_start(step), tk), :], ones], axis=1)
        pv = jnp.dot(p_bufs[buf][...], vx, preferred_element_type=F32)
        if first:
            acc_ref[...] = pv
        else:
            alpha = jnp.exp2(m_bufs[1 - buf][...] - m_bufs[buf][...])
            acc_ref[...] = jnp.concatenate([alpha, alpha], axis=1) * acc_ref[...] + pv

    def softmax_stage(buf, bias_tile, first=False):
        s = s_bufs[buf][...]
        if bias_tile is not None:
            bias = bias_ref[0, bias_tile]
            s = jnp.concatenate([s[:tq] + bias, s[tq:] + bias], axis=0)
        row_max = jnp.max(s, axis=-1, keepdims=True)
        if first:
            m_new = jnp.broadcast_to(row_max, m_bufs[buf].shape)
        else:
            m_new = jnp.maximum(m_bufs[1 - buf][...], row_max)
        m_bufs[buf][...] = m_new
        p = jnp.exp2(s - jnp.concatenate([m_new] * (tk // LANES), axis=1))
        p_bufs[buf][...] = p.astype(BF16)

    def pair(step0, bias0, bias1, lookahead=True, first=False):
        logits_stage(step0 + 1, s_bufs[1])
        if not first:
            pv_stage(step0 - 1, 1)
        softmax_stage(0, bias0, first)
        if lookahead:
            logits_stage(step0 + 2, s_bufs[0])
        pv_stage(step0, 0, first)
        softmax_stage(1, bias1)

    logits_stage(0, s_bufs[0])
    pair(0, 0, 1, first=True)

    n_far_pairs = jnp.maximum(qi - 1, 0)

    def far_quad(jj, carry):
        pair(4 * jj + 2, None, None)
        pair(4 * jj + 4, None, None)
        return carry

    lax.fori_loop(0, n_far_pairs // 2, far_quad, 0)

    @pl.when(n_far_pairs % 2 == 1)
    def _():
        pair(2 * n_far_pairs, None, None)

    @pl.when(qi > 0)
    def _():
        pair(2 * qi, None, 2, lookahead=False)

    pv_stage(2 * qi + 1, 1)

    acc = acc_ref[...]
    o = (acc[:tq, :dv] / acc[:tq, dv:]) - lam_ref[0] * (acc[tq:, :dv] / acc[tq:, dv:])
    ms = jnp.mean(o * o, axis=-1, keepdims=True)
    o_ref[0] = (o * lax.rsqrt(ms + EPS) * gsub_ref[...]).astype(o_ref.dtype)


def _attention(qk3, vbcu3, bias_tiles, gsub, lam):
    b, s, _ = qk3.shape
    h = N_DIFF_HEADS
    tq, tk = ATTN_TQ, ATTN_TK
    assert tq == 2 * tk and s % tq == 0
    kern = functools.partial(_attn_kernel, tq=tq, tk=tk)
    return pl.pallas_call(
        kern,
        grid=(b, h, s // tq),
        in_specs=[
            pl.BlockSpec(memory_space=pltpu.SMEM),
            pl.BlockSpec((1, tq, LANES), lambda bi, hi, qi: (bi, qi, hi)),
            pl.BlockSpec((1, s, LANES), lambda bi, hi, qi: (bi, 0, h + hi)),
            pl.BlockSpec((1, s, LANES), lambda bi, hi, qi: (bi, 0, hi)),
            pl.BlockSpec((1, 3, tq, tk), lambda bi, hi, qi: (hi, 0, 0, 0)),
            pl.BlockSpec((1, LANES), lambda bi, hi, qi: (0, 0)),
        ],
        out_specs=pl.BlockSpec((1, tq, LANES), lambda bi, hi, qi: (bi, qi, hi)),
        out_shape=jax.ShapeDtypeStruct((b, s, h * DIFF_V_DIM), BF16),
        scratch_shapes=[pltpu.VMEM((2 * tq, LANES), BF16),
                        pltpu.VMEM((2 * tq, 2 * DIFF_V_DIM), F32),
                        pltpu.VMEM((2 * tq, tk), F32),
                        pltpu.VMEM((2 * tq, tk), F32),
                        pltpu.VMEM((2 * tq, tk), BF16),
                        pltpu.VMEM((2 * tq, tk), BF16),
                        pltpu.VMEM((2 * tq, LANES), F32),
                        pltpu.VMEM((2 * tq, LANES), F32)],
        compiler_params=_cparams(("parallel", "parallel", "arbitrary")),
        name="diff_attention",
    )(lam, qk3, qk3, vbcu3, bias_tiles, gsub)


def _rel_bucket(rel):
    nb = NUM_BUCKETS // 2
    max_exact = nb // 2
    n = jnp.abs(rel)
    n2 = n * n
    large = max_exact + sum((n2 >= (max_exact * max_exact) * (2 ** k)).astype(jnp.int32)
                            for k in range(1, nb - max_exact))
    return jnp.where(rel > 0, nb, 0) + jnp.where(n < max_exact, n, large)


def _bias_kernel(rb_ref, o_ref, *, tk):
    h = pl.program_id(0)
    tq = o_ref.shape[2]
    qpos = lax.broadcasted_iota(jnp.int32, (tq, tk), 0)
    kcol = lax.broadcasted_iota(jnp.int32, (tq, tk), 1)
    chunk_shift = CHUNK.bit_length() - 1
    far_bias = rb_ref[NUM_BUCKETS // 2 - 1, h]
    for tile, offset in enumerate((0, tk, -tk)):
        kpos = kcol + offset
        bucket = _rel_bucket(kpos - qpos)
        bias = jnp.zeros((tq, tk), F32)
        for b in range(NUM_BUCKETS):
            bias = jnp.where(bucket == b, rb_ref[b, h], bias)
        mask = (kpos >> chunk_shift) <= (qpos >> chunk_shift)
        o_ref[0, tile] = jnp.where(mask, (bias - far_bias) * LOG2E, MASK_VALUE)


def _bias_tiles(rel_bias, tq, tk):
    assert tk >= MAX_DISTANCE
    assert CHUNK & (CHUNK - 1) == 0
    n_heads = rel_bias.shape[1]
    return pl.pallas_call(
        functools.partial(_bias_kernel, tk=tk),
        grid=(n_heads,),
        in_specs=[pl.BlockSpec(memory_space=pltpu.SMEM)],
        out_specs=pl.BlockSpec((1, 3, tq, tk), lambda h: (h, 0, 0, 0)),
        out_shape=jax.ShapeDtypeStruct((n_heads, 3, tq, tk), F32),
        compiler_params=_cparams(("parallel",)),
        name="bias_tiles",
    )(rel_bias)


def _conv_kernel(b_ref, c_ref, u_ref, cp_ref, up_ref, w_ref, cb_ref, g_ref, o_ref, buf_ref):
    si = pl.program_id(1)
    ts = o_ref.shape[1]
    cu = c_ref[0].astype(F32) * u_ref[0].astype(F32)
    prev = cp_ref[0].astype(F32) * up_ref[0].astype(F32)
    buf_ref[0:SUBLANES, :] = jnp.where(si > 0, prev, 0.0)
    buf_ref[SUBLANES:, :] = cu
    conv = (w_ref[0:1, :] * buf_ref[pl.ds(SUBLANES - 2, ts), :]
            + w_ref[1:2, :] * buf_ref[pl.ds(SUBLANES - 1, ts), :]
            + w_ref[2:3, :] * cu)
    z = b_ref[0].astype(F32) * (conv + cb_ref[...])
    for c in range(z.shape[1] // CONV_GROUP_DIM):
        sl = slice(c * CONV_GROUP_DIM, (c + 1) * CONV_GROUP_DIM)
        zc = z[:, sl]
        ms = jnp.mean(zc * zc, axis=-1, keepdims=True)
        o_ref[0, :, sl] = (zc * lax.rsqrt(ms + EPS) * g_ref[:, sl]).astype(o_ref.dtype)


def _short_conv(proj3, conv_w, conv_b, g_conv):
    b, s, n = proj3.shape
    dc = conv_w.shape[1]
    ts = min(CONV_TS, s)
    col0 = (n - 3 * dc) // dc
    halo = ts // SUBLANES

    def main(col):
        return pl.BlockSpec((1, ts, dc), lambda bi, si: (bi, si, col))

    def prev(col):
        return pl.BlockSpec((1, SUBLANES, dc),
                            lambda bi, si: (bi, jnp.maximum(si * halo - 1, 0), col))

    return pl.pallas_call(
        _conv_kernel,
        grid=(b, s // ts),
        in_specs=[main(col0), main(col0 + 1), main(col0 + 2), prev(col0 + 1), prev(col0 + 2),
                  pl.BlockSpec((CONV_WIDTH, dc), lambda bi, si: (0, 0)),
                  pl.BlockSpec((1, dc), lambda bi, si: (0, 0)),
                  pl.BlockSpec((1, dc), lambda bi, si: (0, 0))],
        out_specs=pl.BlockSpec((1, ts, dc), lambda bi, si: (bi, si, 0)),
        out_shape=jax.ShapeDtypeStruct((b, s, dc), BF16),
        scratch_shapes=[pltpu.VMEM((ts + SUBLANES, dc), F32)],
        compiler_params=_cparams(("parallel", "parallel")),
        name="short_conv",
    )(proj3, proj3, proj3, proj3, proj3, conv_w, conv_b, g_conv)


def _out_proj_kernel(x_ref, a_ref, c_ref, wa_ref, wc_ref, g_ref, wr_ref, br_ref, tri_ref,
                     h_ref, hn_ref, mi_ref, wcol_ref, cnt_ref, carry_ref):
    i = pl.program_id(0)
    tm = x_ref.shape[0]

    @pl.when(i == 0)
    def _():
        carry_ref[...] = jnp.zeros_like(carry_ref)

    h = (x_ref[...]
         + jnp.dot(a_ref[...], wa_ref[...], preferred_element_type=F32)
         + jnp.dot(c_ref[...], wc_ref[...], preferred_element_type=F32))
    h_ref[...] = h
    ms = jnp.mean(h * h, axis=-1, keepdims=True)
    hn = h * lax.rsqrt(ms + EPS) * g_ref[...]
    hn_hi = hn.astype(BF16)
    _store_routed_rows(hn_ref, hn)
    hn_lo = (hn - hn_hi.astype(F32)).astype(BF16)

    r_hi = jnp.dot(hn_hi, wr_ref[...], preferred_element_type=F32)
    r_lo = jnp.dot(hn_lo, wr_ref[...], preferred_element_type=F32)
    logits = (r_hi[:, :ROUTER_COLS] + r_hi[:, ROUTER_COLS:] + r_lo[:, :ROUTER_COLS]
              + br_ref[...])
    lt = logits.T

    e = EXPERTS_PER_GROUP
    row = lax.broadcasted_iota(jnp.int32, (e, tm), 0)

    def first_argmax(v):
        vmax = jnp.max(v, axis=0, keepdims=True)
        idx = jnp.min(jnp.where(v == vmax, row, e), axis=0, keepdims=True)
        return vmax, idx

    gl = lt[0:e]
    gmax, g = first_argmax(gl)
    p_g = 1.0 / jnp.sum(jnp.exp(gl - gmax), axis=0, keepdims=True)
    ing = lt[e:2 * e]
    for gi in range(1, N_GROUPS):
        ing = jnp.where(g == gi, lt[(gi + 1) * e:(gi + 2) * e], ing)
    v1, i1 = first_argmax(ing)
    v2, i2 = first_argmax(jnp.where(row == i1, -jnp.inf, ing))
    ex = jnp.exp(v2 - v1)
    w1 = p_g / (1.0 + ex)
    w2 = w1 * ex
    e1 = g * e + i1
    e2 = g * e + i2

    erow = lax.broadcasted_iota(jnp.int32, (N_EXPERTS, tm), 0)
    oh1 = (erow == e1).astype(F32)
    oh2 = (erow == e2).astype(F32)
    oh = jnp.concatenate([oh1, oh2], axis=0).astype(BF16)
    pre = jnp.dot(oh, tri_ref[...], preferred_element_type=F32)
    cnt1 = jnp.sum(oh1, axis=1, keepdims=True)
    cnt2 = jnp.sum(oh2, axis=1, keepdims=True)
    carry = carry_ref[:, 0:1]
    r1 = jnp.sum(oh1 * (pre[:N_EXPERTS] + carry), axis=0, keepdims=True)
    r2 = jnp.sum(oh2 * (pre[N_EXPERTS:] + carry + cnt1), axis=0, keepdims=True)
    new_carry = carry + cnt1 + cnt2
    carry_ref[...] = jnp.broadcast_to(new_carry, carry_ref.shape)
    cnt_ref[...] = jnp.broadcast_to(new_carry, cnt_ref.shape)

    mi_ref[0] = jnp.concatenate(
        [e1, e2, r1.astype(jnp.int32), r2.astype(jnp.int32),
         jnp.zeros((SUBLANES - 4, tm), jnp.int32)], axis=0)
    wrow = jnp.concatenate([w1, w2, jnp.zeros((ROUTER_COLS - 2, tm), F32)], axis=0)
    wcol_ref[...] = wrow.T


def _out_proj(x2, attn2, conv2, wo_a, wo_c, g_ffn, wr, br, tri):
    t, d = x2.shape
    tm = min(OUT_PROJ_TM, t)
    nt = t // tm
    da, dc = attn2.shape[1], conv2.shape[1]
    slabs = d // (2 * LANES)
    const = lambda i: (0, 0)
    return pl.pallas_call(
        _out_proj_kernel,
        grid=(nt,),
        in_specs=[
            pl.BlockSpec((tm, d), lambda i: (i, 0)),
            pl.BlockSpec((tm, da), lambda i: (i, 0)),
            pl.BlockSpec((tm, dc), lambda i: (i, 0)),
            pl.BlockSpec((da, d), const),
            pl.BlockSpec((dc, d), const),
            pl.BlockSpec((1, d), const),
            pl.BlockSpec((d, 2 * ROUTER_COLS), const),
            pl.BlockSpec((1, ROUTER_COLS), const),
            pl.BlockSpec((tm, tm), const),
        ],
        out_specs=[
            pl.BlockSpec((tm, d), lambda i: (i, 0)),
            pl.BlockSpec((slabs, tm, LANES), lambda i: (0, i, 0)),
            pl.BlockSpec((1, SUBLANES, tm), lambda i: (i, 0, 0)),
            pl.BlockSpec((tm, ROUTER_COLS), lambda i: (i, 0)),
            pl.BlockSpec((N_EXPERTS, LANES), const),
        ],
        out_shape=[
            jax.ShapeDtypeStruct((t, d), F32),
            jax.ShapeDtypeStruct((slabs, t, LANES), jnp.uint32),
            jax.ShapeDtypeStruct((nt, SUBLANES, tm), jnp.int32),
            jax.ShapeDtypeStruct((t, ROUTER_COLS), F32),
            jax.ShapeDtypeStruct((N_EXPERTS, LANES), F32),
        ],
        scratch_shapes=[pltpu.VMEM((N_EXPERTS, LANES), F32)],
        compiler_params=_cparams(("arbitrary",)),
        name="out_proj_router",
    )(x2, attn2, conv2, wo_a, wo_c, g_ffn, wr, br, tri)


def _moe_kernel(te_ref, nv_ref, x_ref, w1_ref, w3_ref, w2_ref, o_ref, w13_ref, w2b_ref):
    i = pl.program_id(0)
    dff = w2_ref.shape[2]

    @pl.when(i < nv_ref[0])
    def _():
        @pl.when(jnp.logical_or(i == 0, te_ref[i] != te_ref[jnp.maximum(i - 1, 0)]))
        def _():
            w13_ref[:, :dff] = w1_ref[0, 0].astype(BF16)
            w13_ref[:, dff:] = w3_ref[0, 0].astype(BF16)
            w2b_ref[...] = w2_ref[0, 0].astype(BF16)

        x = _load_routed_rows(x_ref, BF16)
        ab = jnp.dot(x, w13_ref[...], preferred_element_type=F32)
        a, b = ab[:, :dff], ab[:, dff:]
        hid = (a * jax.nn.sigmoid(a) * b).astype(BF16)
        _store_routed_rows(o_ref, jnp.dot(hid, w2b_ref[...], preferred_element_type=F32))

    @pl.when(i >= nv_ref[0])
    def _():
        o_ref[...] = jnp.zeros_like(o_ref)


def _moe(xs, w1, w3, w2, layer, tile_expert, n_valid):
    d, dff = w2.shape[3], w2.shape[2]
    slabs = xs.shape[0]
    nt = xs.shape[1] // MOE_TM
    grid_spec = pltpu.PrefetchScalarGridSpec(
        num_scalar_prefetch=2,
        grid=(nt,),
        in_specs=[
            pl.BlockSpec((slabs, MOE_TM, LANES),
                         lambda i, te, nv: (0, jnp.minimum(i, nv[0] - 1), 0)),
            pl.BlockSpec((1, 1, d, dff), lambda i, te, nv: (layer, te[i], 0, 0)),
            pl.BlockSpec((1, 1, d, dff), lambda i, te, nv: (layer, te[i], 0, 0)),
            pl.BlockSpec((1, 1, dff, d), lambda i, te, nv: (layer, te[i], 0, 0)),
        ],
        out_specs=pl.BlockSpec((slabs, MOE_TM, LANES), lambda i, te, nv: (0, i, 0)),
        scratch_shapes=[pltpu.VMEM((d, 2 * dff), BF16), pltpu.VMEM((dff, d), BF16)],
    )
    return pl.pallas_call(
        _moe_kernel,
        grid_spec=grid_spec,
        out_shape=jax.ShapeDtypeStruct(xs.shape, jnp.uint32),
        compiler_params=_cparams(("arbitrary",)),
        name="moe_grouped",
    )(tile_expert, n_valid, xs, w1, w3, w2)


def _ple_kernel(h_ref, y1_ref, y2_ref, wcol_ref, p_ref, g_ref, wg_ref, wp_ref, *rest):
    o_ref = rest[-1]
    wcol = wcol_ref[...]
    h = (h_ref[...]
         + wcol[:, 0:1] * _load_routed_rows(y1_ref.at[0], F32)
         + wcol[:, 1:2] * _load_routed_rows(y2_ref.at[0], F32))
    ms = jnp.mean(h * h, axis=-1, keepdims=True)
    hn = (h * lax.rsqrt(ms + EPS) * g_ref[...]).astype(BF16)
    gate = jax.nn.sigmoid(jnp.dot(hn, wg_ref[...], preferred_element_type=F32))
    emb = jnp.dot(p_ref[...].astype(BF16), wp_ref[...], preferred_element_type=F32)
    o_ref[...] = h + gate * emb


def _ple(h1, yg, wcol, p2, g_ple, wg, wp, row0, out_prev):
    t, d = h1.shape
    n = yg.shape[2]
    tm = min(PLE_TM, n)
    dp = p2.shape[1]
    slabs = d // (2 * LANES)
    b0 = row0 // tm
    const = lambda i: (0, 0)
    row = lambda i: (b0 + i, 0)
    in_specs = [
        pl.BlockSpec((tm, d), row),
        pl.BlockSpec((1, slabs, tm, LANES), lambda i: (0, 0, i, 0)),
        pl.BlockSpec((1, slabs, tm, LANES), lambda i: (1, 0, i, 0)),
        pl.BlockSpec((tm, ROUTER_COLS), row),
        pl.BlockSpec((tm, dp), row),
        pl.BlockSpec((1, d), const),
        pl.BlockSpec((d, d), const),
        pl.BlockSpec((dp, d), const),
    ]
    args = [h1, yg, yg, wcol, p2, g_ple, wg, wp]
    aliases = {}
    if out_prev is not None:
        in_specs.append(pl.BlockSpec(memory_space=pl.ANY))
        aliases = {len(args): 0}
        args.append(out_prev)
    return pl.pallas_call(
        _ple_kernel,
        grid=(n // tm,),
        in_specs=in_specs,
        out_specs=pl.BlockSpec((tm, d), row),
        out_shape=jax.ShapeDtypeStruct((t, d), F32),
        input_output_aliases=aliases,
        compiler_params=_cparams(("parallel",)),
        name="combine_ple",
    )(*args)


def kernel(x, p, rel_bias, g_mix, w_in, g_q, g_k, lam_q1, lam_k1, lam_q2, lam_k2, g_subln,
           conv_w, conv_b, g_conv, w_o, g_ffn, w_group, b_group, w_expert, b_expert,
           w1, w3, w2, g_ple, w_ple_gate, w_ple_proj):
    depth = g_mix.shape[0]
    assert depth == 1
    li = 0
    b, s, d = x.shape
    t = b * s
    d_attn = N_DIFF_HEADS * DIFF_V_DIM
    x2 = x.reshape(t, d)

    n_groups_qk = d_attn // DIFF_QK_DIM
    gqk = jnp.concatenate([jnp.tile(g_q[li] * (DIFF_QK_DIM ** -0.5 * LOG2E), n_groups_qk),
                           jnp.tile(g_k[li], n_groups_qk)])[None, :].astype(F32)
    blk = jnp.arange(MXU_WIDTH) // DIFF_QK_DIM
    gsum = (blk[:, None] == blk[None, :]).astype(BF16)
    lam = (jnp.exp(jnp.sum(lam_q1[li] * lam_k1[li])) - jnp.exp(jnp.sum(lam_q2[li] * lam_k2[li]))
           + LAM_INIT).reshape(1).astype(F32)
    gsub = (g_subln[li] * (1.0 - LAM_INIT))[None, :].astype(F32)
    bias_tiles = _bias_tiles(rel_bias, ATTN_TQ, ATTN_TK)

    pad_g = EXPERTS_PER_GROUP - N_GROUPS
    pad_e = ROUTER_COLS - EXPERTS_PER_GROUP - N_EXPERTS
    wr_f32 = jnp.concatenate([w_group[li], jnp.zeros((d, pad_g), F32),
                              w_expert[li], jnp.zeros((d, pad_e), F32)], axis=1)
    wr_hi = wr_f32.astype(BF16)
    wr_lo = (wr_f32 - wr_hi.astype(F32)).astype(BF16)
    wr = jnp.concatenate([wr_hi, wr_lo], axis=1)
    br = jnp.concatenate([b_group[li], jnp.full((pad_g,), MASK_VALUE, F32),
                          b_expert[li], jnp.zeros((pad_e,), F32)])[None, :]
    tm_r = min(OUT_PROJ_TM, t)
    ar = jnp.arange(tm_r)
    tri = (ar[:, None] < ar[None, :]).astype(BF16)

    w_in_bf = w_in[li].astype(BF16)
    wo_bf = w_o[li].astype(BF16)
    wg_bf = w_ple_gate[li].astype(BF16)
    wp_bf = w_ple_proj[li].astype(BF16)

    n_qk = 2 * d_attn
    qk, xn = _in_proj_qk(x2, g_mix[li][None, :], w_in_bf, n_qk, gqk, gsum)
    vbcu = _proj(xn, w_in_bf, n_qk, w_in_bf.shape[1] - n_qk)
    vbcu3 = vbcu.reshape(b, s, -1)
    attn = _attention(qk.reshape(b, s, -1), vbcu3, bias_tiles, gsub, lam)
    conv = _short_conv(vbcu3, conv_w[li], conv_b[li][None, :], g_conv[li][None, :])
    h1, hn, meta_i, wcol, counts = _out_proj(
        x2, attn.reshape(t, -1), conv.reshape(t, -1), wo_bf[:d_attn], wo_bf[d_attn:],
        g_ffn[li][None, :], wr, br, tri)

    eid = jnp.transpose(meta_i[:, 0:2, :], (1, 0, 2)).reshape(2, t)
    rank = jnp.transpose(meta_i[:, 2:4, :], (1, 0, 2)).reshape(2, t)
    cnt = counts[:, 0].astype(jnp.int32)
    tiles_per = (cnt + MOE_TM - 1) // MOE_TM
    experts = jnp.arange(N_EXPERTS, dtype=jnp.int32)
    tile_end = jnp.sum(jnp.where(experts[:, None] <= experts[None, :], tiles_per[:, None], 0),
                       axis=0)
    row_start = (tile_end - tiles_per) * MOE_TM
    n_tiles = (TOP_K * t) // MOE_TM + N_EXPERTS
    pos = rank + jnp.sum(jnp.where(eid[..., None] == experts, row_start, 0), axis=-1)
    tile_expert = jnp.minimum(
        jnp.sum((jnp.arange(n_tiles, dtype=jnp.int32)[:, None] >= tile_end[None, :])
                .astype(jnp.int32), axis=1),
        N_EXPERTS - 1)
    n_valid = tile_end[-1:]

    slabs = d // (2 * LANES)
    n_slots = n_tiles * MOE_TM
    slab_base = jnp.arange(slabs, dtype=jnp.int32) * n_slots
    row_idx = pos[:, None, :] + slab_base[None, :, None]

    xs = _sc_scatter_rows(hn.reshape(slabs * t, LANES), row_idx[0].reshape(1, -1),
                          row_idx[1].reshape(1, -1), slabs * n_slots)
    ys = _moe(xs.reshape(slabs, n_slots, LANES), w1, w3, w2, li, tile_expert, n_valid)
    ys_rows = ys.reshape(slabs * n_slots, LANES)
    p2 = p[li].reshape(t, -1)
    tc = t // COMBINE_CHUNKS
    out = None
    for c in range(COMBINE_CHUNKS):
        idx_c = row_idx[:, :, c * tc:(c + 1) * tc].reshape(1, -1)
        yg = _sc_gather_rows(ys_rows, idx_c).reshape(TOP_K, slabs, tc, LANES)
        out = _ple(h1, yg, wcol, p2, g_ple[li][None, :], wg_bf, wp_bf, c * tc, out)
    return out.reshape(b, s, d)
```

```python
import functools
import math

import jax
import jax.numpy as jnp
from jax import lax
from jax.experimental import pallas as pl
from jax.experimental.pallas import tpu as pltpu
from jax.experimental.pallas import tpu_sc as plsc

F32 = jnp.float32
BF16 = jnp.bfloat16

N_DIFF_HEADS = 8
DIFF_QK_DIM = 64
DIFF_V_DIM = 128
CHUNK = 64
NUM_BUCKETS = 32
MAX_DISTANCE = 128
CONV_WIDTH = 3
CONV_GROUP_DIM = 128
N_GROUPS = 4
EXPERTS_PER_GROUP = 8
N_EXPERTS = N_GROUPS * EXPERTS_PER_GROUP
TOP_K = 2
EPS = 1e-6
MASK_VALUE = -1e30
LAM_INIT = 0.8 - 0.6 * math.exp(-0.3 * 0)
LOG2E = math.log2(math.e)

LANES = 128
SUBLANES = 8
MXU_WIDTH = 256
VMEM_LIMIT_BYTES = 56 * 1024 * 1024

IN_PROJ_TM = 1024
IN_PROJ_TN = 1024
PROJ_TN = 2048
ATTN_TQ = 512
ATTN_TK = 256
CONV_TS = 512
OUT_PROJ_TM = 512
MOE_TM = 256
PLE_TM = 512
COMBINE_CHUNKS = 4
ROUTER_COLS = 128
SC_WINDOW = 128
SC_NUM_CORES = 2
SC_NUM_SUBCORES = 16


def _cparams(semantics):
    return pltpu.CompilerParams(dimension_semantics=semantics,
                                vmem_limit_bytes=VMEM_LIMIT_BYTES)


def _store_routed_rows(o_ref, x):
    d = x.shape[1]
    half = d // 2
    slabs = half // LANES
    xr = x.astype(BF16).astype(F32)
    lo = lax.bitcast_convert_type(xr[:, :half], jnp.uint32)
    hi = lax.bitcast_convert_type(xr[:, half:], jnp.uint32)
    packed = (lo >> 16) | (hi & jnp.uint32(0xFFFF0000))
    for r in range(slabs):
        o_ref[r] = packed[:, r * LANES:(r + 1) * LANES]


def _load_routed_rows(x_ref, dtype):
    parts = [x_ref[r] for r in range(x_ref.shape[0])]
    lo = [lax.bitcast_convert_type(w << 16, F32).astype(dtype) for w in parts]
    hi = [lax.bitcast_convert_type(w & jnp.uint32(0xFFFF0000), F32).astype(dtype) for w in parts]
    return jnp.concatenate(lo + hi, axis=1)


def _sc_mesh():
    return plsc.VectorSubcoreMesh(core_axis_name="core", subcore_axis_name="subcore",
                                  num_cores=SC_NUM_CORES, num_subcores=SC_NUM_SUBCORES)


def _sc_scatter_rows(rows, idx_a, idx_b, n_out_rows):
    n_src = rows.shape[0]

    @functools.partial(pl.kernel, mesh=_sc_mesh(), scratch_types=[],
                       out_type=jax.ShapeDtypeStruct((n_out_rows, LANES), rows.dtype))
    def scatter(x_hbm, ia_hbm, ib_hbm, o_hbm):
        def body(x_vmem, ia_vmem, ib_vmem):
            pltpu.sync_copy(x_vmem, o_hbm.at[ia_vmem.at[0]])
            pltpu.sync_copy(x_vmem, o_hbm.at[ib_vmem.at[0]])

        pltpu.emit_pipeline(
            body,
            grid=(n_src // SC_WINDOW,),
            in_specs=[pl.BlockSpec((SC_WINDOW, LANES), lambda i: (i, 0)),
                      pl.BlockSpec((1, SC_WINDOW), lambda i: (0, i)),
                      pl.BlockSpec((1, SC_WINDOW), lambda i: (0, i))],
            out_specs=[],
            core_axis_name=("core", "subcore"),
            dimension_semantics=(pltpu.PARALLEL,),
        )(x_hbm, ia_hbm, ib_hbm)

    return scatter(rows, idx_a, idx_b)


def _sc_gather_rows(table, idx):
    n_idx = idx.shape[1]

    @functools.partial(pl.kernel, mesh=_sc_mesh(), scratch_types=[],
                       out_type=jax.ShapeDtypeStruct((n_idx, LANES), table.dtype))
    def gather(t_hbm, i_hbm, o_hbm):
        def body(i_vmem, o_vmem):
            pltpu.sync_copy(t_hbm.at[i_vmem.at[0]], o_vmem)

        pltpu.emit_pipeline(
            body,
            grid=(n_idx // SC_WINDOW,),
            in_specs=[pl.BlockSpec((1, SC_WINDOW), lambda i: (0, i))],
            out_specs=[pl.BlockSpec((SC_WINDOW, LANES), lambda i: (i, 0))],
            core_axis_name=("core", "subcore"),
            dimension_semantics=(pltpu.PARALLEL,),
        )(i_hbm, o_hbm)

    return gather(table, idx)


def _proj_kernel(xn_ref, w_ref, o_ref):
    o_ref[...] = jnp.dot(xn_ref[...], w_ref[...],
                         preferred_element_type=F32).astype(o_ref.dtype)


def _in_proj_qk_kernel(x_ref, g_ref, w_ref, gqk_ref, gsum_ref, o_ref, xn_ref):
    @pl.when(pl.program_id(1) == 0)
    def _():
        x = x_ref[...]
        ms = jnp.mean(x * x, axis=-1, keepdims=True)
        xn_ref[...] = (x * lax.rsqrt(ms + EPS) * g_ref[...]).astype(BF16)

    acc = jnp.dot(xn_ref[...], w_ref[...], preferred_element_type=F32)
    for c in range(acc.shape[1] // MXU_WIDTH):
        sl = slice(c * MXU_WIDTH, (c + 1) * MXU_WIDTH)
        a = acc[:, sl]
        sq = a * a
        hi = sq.astype(BF16)
        lo = (sq - hi.astype(F32)).astype(BF16)
        ss = (jnp.dot(hi, gsum_ref[...], preferred_element_type=F32)
              + jnp.dot(lo, gsum_ref[...], preferred_element_type=F32))
        y = a * lax.rsqrt(ss * (1.0 / DIFF_QK_DIM) + EPS)
        o_ref[:, sl] = (y * gqk_ref[:, sl]).astype(o_ref.dtype)


def _in_proj_qk(x2, g_mix, w_in_bf, n_out, gqk, gsum):
    t, d = x2.shape
    tm, tn = min(IN_PROJ_TM, t), IN_PROJ_TN
    return pl.pallas_call(
        _in_proj_qk_kernel,
        grid=(t // tm, n_out // tn),
        in_specs=[
            pl.BlockSpec((tm, d), lambda i, j: (i, 0)),
            pl.BlockSpec((1, d), lambda i, j: (0, 0)),
            pl.BlockSpec((d, tn), lambda i, j: (0, j)),
            pl.BlockSpec((1, tn), lambda i, j: (0, j)),
            pl.BlockSpec((MXU_WIDTH, MXU_WIDTH), lambda i, j: (0, 0)),
        ],
        out_specs=[pl.BlockSpec((tm, tn), lambda i, j: (i, j)),
                   pl.BlockSpec((tm, d), lambda i, j: (i, 0))],
        out_shape=[jax.ShapeDtypeStruct((t, n_out), BF16),
                   jax.ShapeDtypeStruct((t, d), BF16)],
        compiler_params=_cparams(("parallel", "arbitrary")),
        name="in_proj_qk",
    )(x2, g_mix, w_in_bf, gqk, gsum)


def _proj(xn, w_bf, col0, n_out):
    t, d = xn.shape
    tm, tn = min(IN_PROJ_TM, t), PROJ_TN
    jb0 = col0 // tn
    return pl.pallas_call(
        _proj_kernel,
        grid=(t // tm, n_out // tn),
        in_specs=[pl.BlockSpec((tm, d), lambda i, j: (i, 0)),
                  pl.BlockSpec((d, tn), lambda i, j: (0, jb0 + j))],
        out_specs=pl.BlockSpec((tm, tn), lambda i, j: (i, j)),
        out_shape=jax.ShapeDtypeStruct((t, n_out), BF16),
        compiler_params=_cparams(("parallel", "parallel")),
        name="in_proj_vbcu",
    )(xn, w_bf)


def _attn_kernel(lam_ref, q_ref, k_ref, v_ref, bias_ref, gsub_ref, o_ref,
                 qs_ref, acc_ref, s0_ref, s1_ref, p0_ref, p1_ref, m0_ref, m1_ref, *, tq, tk):
    qi = pl.program_id(2)
    dv = DIFF_V_DIM
    s_bufs, p_bufs, m_bufs = (s0_ref, s1_ref), (p0_ref, p1_ref), (m0_ref, m1_ref)

    q = q_ref[0]
    lane = lax.broadcasted_iota(jnp.int32, q.shape, 1)
    zero = jnp.zeros_like(q)
    qs_ref[0:tq, :] = jnp.where(lane < DIFF_QK_DIM, q, zero)
    qs_ref[tq:, :] = jnp.where(lane < DIFF_QK_DIM, zero, q)
    ones = jnp.ones((tk, dv), BF16)

    def block_start(step):
        blk = jnp.where(step < 2, 2 * qi + jnp.maximum(step, 0), step - 2)
        return pl.multiple_of(blk * tk, tk)

    def logits_stage(step, dst):
        dst[...] = lax.dot_general(qs_ref[...], k_ref[0, pl.ds(block_start(step), tk), :],
                                   (((1,), (1,)), ((), ())), preferred_element_type=F32)

    def pv_stage(step, buf, first=False):
        vx = jnp.concatenate([v_ref[0, pl.ds(block_start(step), tk), :], ones], axis=1)
        pv = jnp.dot(p_bufs[buf][...], vx, preferred_element_type=F32)
        if first:
            acc_ref[...] = pv
        else:
            alpha = jnp.exp2(m_bufs[1 - buf][...] - m_bufs[buf][...])
            acc_ref[...] = jnp.concatenate([alpha, alpha], axis=1) * acc_ref[...] + pv

    def softmax_stage(buf, bias_tile, first=False):
        s = s_bufs[buf][...]
        if bias_tile is not None:
            bias = bias_ref[0, bias_tile]
            s = jnp.concatenate([s[:tq] + bias, s[tq:] + bias], axis=0)
        row_max = jnp.max(s, axis=-1, keepdims=True)
        if first:
            m_new = jnp.broadcast_to(row_max, m_bufs[buf].shape)
        else:
            m_new = jnp.maximum(m_bufs[1 - buf][...], row_max)
        m_bufs[buf][...] = m_new
        p = jnp.exp2(s - jnp.concatenate([m_new] * (tk // LANES), axis=1))
        p_bufs[buf][...] = p.astype(BF16)

    def pair(step0, bias0, bias1, lookahead=True, first=False):
        logits_stage(step0 + 1, s_bufs[1])
        if not first:
            pv_stage(step0 - 1, 1)
        softmax_stage(0, bias0, first)
        if lookahead:
            logits_stage(step0 + 2, s_bufs[0])
        pv_stage(step0, 0, first)
        softmax_stage(1, bias1)

    logits_stage(0, s_bufs[0])
    pair(0, 0, 1, first=True)

    n_far_pairs = jnp.maximum(qi - 1, 0)

    def far_quad(jj, carry):
        pair(4 * jj + 2, None, None)
        pair(4 * jj + 4, None, None)
        return carry

    lax.fori_loop(0, n_far_pairs // 2, far_quad, 0)

    @pl.when(n_far_pairs % 2 == 1)
    def _():
        pair(2 * n_far_pairs, None, None)

    @pl.when(qi > 0)
    def _():
        pair(2 * qi, None, 2, lookahead=False)

    pv_stage(2 * qi + 1, 1)

    acc = acc_ref[...]
    o = (acc[:tq, :dv] / acc[:tq, dv:]) - lam_ref[0] * (acc[tq:, :dv] / acc[tq:, dv:])
    ms = jnp.mean(o * o, axis=-1, keepdims=True)
    o_ref[0] = (o * lax.rsqrt(ms + EPS) * gsub_ref[...]).astype(o_ref.dtype)


def _attention(qk3, vbcu3, bias_tiles, gsub, lam):
    b, s, _ = qk3.shape
    h = N_DIFF_HEADS
    tq, tk = ATTN_TQ, ATTN_TK
    assert tq == 2 * tk and s % tq == 0
    kern = functools.partial(_attn_kernel, tq=tq, tk=tk)
    return pl.pallas_call(
        kern,
        grid=(b, h, s // tq),
        in_specs=[
            pl.BlockSpec(memory_space=pltpu.SMEM),
            pl.BlockSpec((1, tq, LANES), lambda bi, hi, qi: (bi, qi, hi)),
            pl.BlockSpec((1, s, LANES), lambda bi, hi, qi: (bi, 0, h + hi)),
            pl.BlockSpec((1, s, LANES), lambda bi, hi, qi: (bi, 0, hi)),
            pl.BlockSpec((1, 3, tq, tk), lambda bi, hi, qi: (hi, 0, 0, 0)),
            pl.BlockSpec((1, LANES), lambda bi, hi, qi: (0, 0)),
        ],
        out_specs=pl.BlockSpec((1, tq, LANES), lambda bi, hi, qi: (bi, qi, hi)),
        out_shape=jax.ShapeDtypeStruct((b, s, h * DIFF_V_DIM), BF16),
        scratch_shapes=[pltpu.VMEM((2 * tq, LANES), BF16),
                        pltpu.VMEM((2 * tq, 2 * DIFF_V_DIM), F32),
                        pltpu.VMEM((2 * tq, tk), F32),
                        pltpu.VMEM((2 * tq, tk), F32),
                        pltpu.VMEM((2 * tq, tk), BF16),
                        pltpu.VMEM((2 * tq, tk), BF16),
                        pltpu.VMEM((2 * tq, LANES), F32),
                        pltpu.VMEM((2 * tq, LANES), F32)],
        compiler_params=_cparams(("parallel", "parallel", "arbitrary")),
        name="diff_attention",
    )(lam, qk3, qk3, vbcu3, bias_tiles, gsub)


def _rel_bucket(rel):
    nb = NUM_BUCKETS // 2
    max_exact = nb // 2
    n = jnp.abs(rel)
    n2 = n * n
    large = max_exact + sum((n2 >= (max_exact * max_exact) * (2 ** k)).astype(jnp.int32)
                            for k in range(1, nb - max_exact))
    return jnp.where(rel > 0, nb, 0) + jnp.where(n < max_exact, n, large)


def _bias_kernel(rb_ref, o_ref, *, tk):
    h = pl.program_id(0)
    tq = o_ref.shape[2]
    qpos = lax.broadcasted_iota(jnp.int32, (tq, tk), 0)
    kcol = lax.broadcasted_iota(jnp.int32, (tq, tk), 1)
    chunk_shift = CHUNK.bit_length() - 1
    far_bias = rb_ref[NUM_BUCKETS // 2 - 1, h]
    for tile, offset in enumerate((0, tk, -tk)):
        kpos = kcol + offset
        bucket = _rel_bucket(kpos - qpos)
        bias = jnp.zeros((tq, tk), F32)
        for b in range(NUM_BUCKETS):
            bias = jnp.where(bucket == b, rb_ref[b, h], bias)
        mask = (kpos >> chunk_shift) <= (qpos >> chunk_shift)
        o_ref[0, tile] = jnp.where(mask, (bias - far_bias) * LOG2E, MASK_VALUE)


def _bias_tiles(rel_bias, tq, tk):
    assert tk >= MAX_DISTANCE
    assert CHUNK & (CHUNK - 1) == 0
    n_heads = rel_bias.shape[1]
    return pl.pallas_call(
        functools.partial(_bias_kernel, tk=tk),
        grid=(n_heads,),
        in_specs=[pl.BlockSpec(memory_space=pltpu.SMEM)],
        out_specs=pl.BlockSpec((1, 3, tq, tk), lambda h: (h, 0, 0, 0)),
        out_shape=jax.ShapeDtypeStruct((n_heads, 3, tq, tk), F32),
        compiler_params=_cparams(("parallel",)),
        name="bias_tiles",
    )(rel_bias)


def _conv_kernel(b_ref, c_ref, u_ref, cp_ref, up_ref, w_ref, cb_ref, g_ref, o_ref, buf_ref):
    si = pl.program_id(1)
    ts = o_ref.shape[1]
    cu = c_ref[0].astype(F32) * u_ref[0].astype(F32)
    prev = cp_ref[0].astype(F32) * up_ref[0].astype(F32)
    buf_ref[0:SUBLANES, :] = jnp.where(si > 0, prev, 0.0)
    buf_ref[SUBLANES:, :] = cu
    conv = (w_ref[0:1, :] * buf_ref[pl.ds(SUBLANES - 2, ts), :]
            + w_ref[1:2, :] * buf_ref[pl.ds(SUBLANES - 1, ts), :]
            + w_ref[2:3, :] * cu)
    z = b_ref[0].astype(F32) * (conv + cb_ref[...])
    for c in range(z.shape[1] // CONV_GROUP_DIM):
        sl = slice(c * CONV_GROUP_DIM, (c + 1) * CONV_GROUP_DIM)
        zc = z[:, sl]
        ms = jnp.mean(zc * zc, axis=-1, keepdims=True)
        o_ref[0, :, sl] = (zc * lax.rsqrt(ms + EPS) * g_ref[:, sl]).astype(o_ref.dtype)


def _short_conv(proj3, conv_w, conv_b, g_conv):
    b, s, n = proj3.shape
    dc = conv_w.shape[1]
    ts = min(CONV_TS, s)
    col0 = (n - 3 * dc) // dc
    halo = ts // SUBLANES

    def main(col):
        return pl.BlockSpec((1, ts, dc), lambda bi, si: (bi, si, col))

    def prev(col):
        return pl.BlockSpec((1, SUBLANES, dc),
                            lambda bi, si: (bi, jnp.maximum(si * halo - 1, 0), col))

    return pl.pallas_call(
        _conv_kernel,
        grid=(b, s // ts),
        in_specs=[main(col0), main(col0 + 1), main(col0 + 2), prev(col0 + 1), prev(col0 + 2),
                  pl.BlockSpec((CONV_WIDTH, dc), lambda bi, si: (0, 0)),
                  pl.BlockSpec((1, dc), lambda bi, si: (0, 0)),
                  pl.BlockSpec((1, dc), lambda bi, si: (0, 0))],
        out_specs=pl.BlockSpec((1, ts, dc), lambda bi, si: (bi, si, 0)),
        out_shape=jax.ShapeDtypeStruct((b, s, dc), BF16),
        scratch_shapes=[pltpu.VMEM((ts + SUBLANES, dc), F32)],
        compiler_params=_cparams(("parallel", "parallel")),
        name="short_conv",
    )(proj3, proj3, proj3, proj3, proj3, conv_w, conv_b, g_conv)


def _out_proj_kernel(x_ref, a_ref, c_ref, wa_ref, wc_ref, g_ref, wr_ref, br_ref, tri_ref,
                     h_ref, hn_ref, mi_ref, wcol_ref, cnt_ref, carry_ref):
    i = pl.program_id(0)
    tm = x_ref.shape[0]

    @pl.when(i == 0)
    def _():
        carry_ref[...] = jnp.zeros_like(carry_ref)

    h = (x_ref[...]
         + jnp.dot(a_ref[...], wa_ref[...], preferred_element_type=F32)
         + jnp.dot(c_ref[...], wc_ref[...], preferred_element_type=F32))
    h_ref[...] = h
    ms = jnp.mean(h * h, axis=-1, keepdims=True)
    hn = h * lax.rsqrt(ms + EPS) * g_ref[...]
    hn_hi = hn.astype(BF16)
    _store_routed_rows(hn_ref, hn)
    hn_lo = (hn - hn_hi.astype(F32)).astype(BF16)

    r_hi = jnp.dot(hn_hi, wr_ref[...], preferred_element_type=F32)
    r_lo = jnp.dot(hn_lo, wr_ref[...], preferred_element_type=F32)
    logits = (r_hi[:, :ROUTER_COLS] + r_hi[:, ROUTER_COLS:] + r_lo[:, :ROUTER_COLS]
              + br_ref[...])
    lt = logits.T

    e = EXPERTS_PER_GROUP
    row = lax.broadcasted_iota(jnp.int32, (e, tm), 0)

    def first_argmax(v):
        vmax = jnp.max(v, axis=0, keepdims=True)
        idx = jnp.min(jnp.where(v == vmax, row, e), axis=0, keepdims=True)
        return vmax, idx

    gl = lt[0:e]
    gmax, g = first_argmax(gl)
    p_g = 1.0 / jnp.sum(jnp.exp(gl - gmax), axis=0, keepdims=True)
    ing = lt[e:2 * e]
    for gi in range(1, N_GROUPS):
        ing = jnp.where(g == gi, lt[(gi + 1) * e:(gi + 2) * e], ing)
    v1, i1 = first_argmax(ing)
    v2, i2 = first_argmax(jnp.where(row == i1, -jnp.inf, ing))
    ex = jnp.exp(v2 - v1)
    w1 = p_g / (1.0 + ex)
    w2 = w1 * ex
    e1 = g * e + i1
    e2 = g * e + i2

    erow = lax.broadcasted_iota(jnp.int32, (N_EXPERTS, tm), 0)
    oh1 = (erow == e1).astype(F32)
    oh2 = (erow == e2).astype(F32)
    oh = jnp.concatenate([oh1, oh2], axis=0).astype(BF16)
    pre = jnp.dot(oh, tri_ref[...], preferred_element_type=F32)
    cnt1 = jnp.sum(oh1, axis=1, keepdims=True)
    cnt2 = jnp.sum(oh2, axis=1, keepdims=True)
    carry = carry_ref[:, 0:1]
    r1 = jnp.sum(oh1 * (pre[:N_EXPERTS] + carry), axis=0, keepdims=True)
    r2 = jnp.sum(oh2 * (pre[N_EXPERTS:] + carry + cnt1), axis=0, keepdims=True)
    new_carry = carry + cnt1 + cnt2
    carry_ref[...] = jnp.broadcast_to(new_carry, carry_ref.shape)
    cnt_ref[...] = jnp.broadcast_to(new_carry, cnt_ref.shape)

    mi_ref[0] = jnp.concatenate(
        [e1, e2, r1.astype(jnp.int32), r2.astype(jnp.int32),
         jnp.zeros((SUBLANES - 4, tm), jnp.int32)], axis=0)
    wrow = jnp.concatenate([w1, w2, jnp.zeros((ROUTER_COLS - 2, tm), F32)], axis=0)
    wcol_ref[...] = wrow.T


def _out_proj(x2, attn2, conv2, wo_a, wo_c, g_ffn, wr, br, tri):
    t, d = x2.shape
    tm = min(OUT_PROJ_TM, t)
    nt = t // tm
    da, dc = attn2.shape[1], conv2.shape[1]
    slabs = d // (2 * LANES)
    const = lambda i: (0, 0)
    return pl.pallas_call(
        _out_proj_kernel,
        grid=(nt,),
        in_specs=[
            pl.BlockSpec((tm, d), lambda i: (i, 0)),
            pl.BlockSpec((tm, da), lambda i: (i, 0)),
            pl.BlockSpec((tm, dc), lambda i: (i, 0)),
            pl.BlockSpec((da, d), const),
            pl.BlockSpec((dc, d), const),
            pl.BlockSpec((1, d), const),
            pl.BlockSpec((d, 2 * ROUTER_COLS), const),
            pl.BlockSpec((1, ROUTER_COLS), const),
            pl.BlockSpec((tm, tm), const),
        ],
        out_specs=[
            pl.BlockSpec((tm, d), lambda i: (i, 0)),
            pl.BlockSpec((slabs, tm, LANES), lambda i: (0, i, 0)),
            pl.BlockSpec((1, SUBLANES, tm), lambda i: (i, 0, 0)),
            pl.BlockSpec((tm, ROUTER_COLS), lambda i: (i, 0)),
            pl.BlockSpec((N_EXPERTS, LANES), const),
        ],
        out_shape=[
            jax.ShapeDtypeStruct((t, d), F32),
            jax.ShapeDtypeStruct((slabs, t, LANES), jnp.uint32),
            jax.ShapeDtypeStruct((nt, SUBLANES, tm), jnp.int32),
            jax.ShapeDtypeStruct((t, ROUTER_COLS), F32),
            jax.ShapeDtypeStruct((N_EXPERTS, LANES), F32),
        ],
        scratch_shapes=[pltpu.VMEM((N_EXPERTS, LANES), F32)],
        compiler_params=_cparams(("arbitrary",)),
        name="out_proj_router",
    )(x2, attn2, conv2, wo_a, wo_c, g_ffn, wr, br, tri)


def _moe_kernel(te_ref, nv_ref, x_ref, w1_ref, w3_ref, w2_ref, o_ref, w13_ref, w2b_ref):
    i = pl.program_id(0)
    dff = w2_ref.shape[2]

    @pl.when(i < nv_ref[0])
    def _():
        @pl.when(jnp.logical_or(i == 0, te_ref[i] != te_ref[jnp.maximum(i - 1, 0)]))
        def _():
            w13_ref[:, :dff] = w1_ref[0, 0].astype(BF16)
            w13_ref[:, dff:] = w3_ref[0, 0].astype(BF16)
            w2b_ref[...] = w2_ref[0, 0].astype(BF16)

        x = _load_routed_rows(x_ref, BF16)
        ab = jnp.dot(x, w13_ref[...], preferred_element_type=F32)
        a, b = ab[:, :dff], ab[:, dff:]
        hid = (a * jax.nn.sigmoid(a) * b).astype(BF16)
        _store_routed_rows(o_ref, jnp.dot(hid, w2b_ref[...], preferred_element_type=F32))

    @pl.when(i >= nv_ref[0])
    def _():
        o_ref[...] = jnp.zeros_like(o_ref)


def _moe(xs, w1, w3, w2, layer, tile_expert, n_valid):
    d, dff = w2.shape[3], w2.shape[2]
    slabs = xs.shape[0]
    nt = xs.shape[1] // MOE_TM
    grid_spec = pltpu.PrefetchScalarGridSpec(
        num_scalar_prefetch=2,
        grid=(nt,),
        in_specs=[
            pl.BlockSpec((slabs, MOE_TM, LANES),
                         lambda i, te, nv: (0, jnp.minimum(i, nv[0] - 1), 0)),
            pl.BlockSpec((1, 1, d, dff), lambda i, te, nv: (layer, te[i], 0, 0)),
            pl.BlockSpec((1, 1, d, dff), lambda i, te, nv: (layer, te[i], 0, 0)),
            pl.BlockSpec((1, 1, dff, d), lambda i, te, nv: (layer, te[i], 0, 0)),
        ],
        out_specs=pl.BlockSpec((slabs, MOE_TM, LANES), lambda i, te, nv: (0, i, 0)),
        scratch_shapes=[pltpu.VMEM((d, 2 * dff), BF16), pltpu.VMEM((dff, d), BF16)],
    )
    return pl.pallas_call(
        _moe_kernel,
        grid_spec=grid_spec,
        out_shape=jax.ShapeDtypeStruct(xs.shape, jnp.uint32),
        compiler_params=_cparams(("arbitrary",)),
        name="moe_grouped",
    )(tile_expert, n_valid, xs, w1, w3, w2)


def _ple_kernel(h_ref, y1_ref, y2_ref, wcol_ref, p_ref, g_ref, wg_ref, wp_ref, *rest):
    o_ref = rest[-1]
    wcol = wcol_ref[...]
    h = (h_ref[...]
         + wcol[:, 0:1] * _load_routed_rows(y1_ref.at[0], F32)
         + wcol[:, 1:2] * _load_routed_rows(y2_ref.at[0], F32))
    ms = jnp.mean(h * h, axis=-1, keepdims=True)
    hn = (h * lax.rsqrt(ms + EPS) * g_ref[...]).astype(BF16)
    gate = jax.nn.sigmoid(jnp.dot(hn, wg_ref[...], preferred_element_type=F32))
    emb = jnp.dot(p_ref[...].astype(BF16), wp_ref[...], preferred_element_type=F32)
    o_ref[...] = h + gate * emb


def _ple(h1, yg, wcol, p2, g_ple, wg, wp, row0, out_prev):
    t, d = h1.shape
    n = yg.shape[2]
    tm = min(PLE_TM, n)
    dp = p2.shape[1]
    slabs = d // (2 * LANES)
    b0 = row0 // tm
    const = lambda i: (0, 0)
    row = lambda i: (b0 + i, 0)
    in_specs = [
        pl.BlockSpec((tm, d), row),
        pl.BlockSpec((1, slabs, tm, LANES), lambda i: (0, 0, i, 0)),
        pl.BlockSpec((1, slabs, tm, LANES), lambda i: (1, 0, i, 0)),
        pl.BlockSpec((tm, ROUTER_COLS), row),
        pl.BlockSpec((tm, dp), row),
        pl.BlockSpec((1, d), const),
        pl.BlockSpec((d, d), const),
        pl.BlockSpec((dp, d), const),
    ]
    args = [h1, yg, yg, wcol, p2, g_ple, wg, wp]
    aliases = {}
    if out_prev is not None:
        in_specs.append(pl.BlockSpec(memory_space=pl.ANY))
        aliases = {len(args): 0}
        args.append(out_prev)
    return pl.pallas_call(
        _ple_kernel,
        grid=(n // tm,),
        in_specs=in_specs,
        out_specs=pl.BlockSpec((tm, d), row),
        out_shape=jax.ShapeDtypeStruct((t, d), F32),
        input_output_aliases=aliases,
        compiler_params=_cparams(("parallel",)),
        name="combine_ple",
    )(*args)


def kernel(x, p, rel_bias, g_mix, w_in, g_q, g_k, lam_q1, lam_k1, lam_q2, lam_k2, g_subln,
           conv_w, conv_b, g_conv, w_o, g_ffn, w_group, b_group, w_expert, b_expert,
           w1, w3, w2, g_ple, w_ple_gate, w_ple_proj):
    depth = g_mix.shape[0]
    assert depth == 1
    li = 0
    b, s, d = x.shape
    t = b * s
    d_attn = N_DIFF_HEADS * DIFF_V_DIM
    x2 = x.reshape(t, d)

    n_groups_qk = d_attn // DIFF_QK_DIM
    gqk = jnp.concatenate([jnp.tile(g_q[li] * (DIFF_QK_DIM ** -0.5 * LOG2E), n_groups_qk),
                           jnp.tile(g_k[li], n_groups_qk)])[None, :].astype(F32)
    blk = jnp.arange(MXU_WIDTH) // DIFF_QK_DIM
    gsum = (blk[:, None] == blk[None, :]).astype(BF16)
    lam = (jnp.exp(jnp.sum(lam_q1[li] * lam_k1[li])) - jnp.exp(jnp.sum(lam_q2[li] * lam_k2[li]))
           + LAM_INIT).reshape(1).astype(F32)
    gsub = (g_subln[li] * (1.0 - LAM_INIT))[None, :].astype(F32)
    bias_tiles = _bias_tiles(rel_bias, ATTN_TQ, ATTN_TK)

    pad_g = EXPERTS_PER_GROUP - N_GROUPS
    pad_e = ROUTER_COLS - EXPERTS_PER_GROUP - N_EXPERTS
    wr_f32 = jnp.concatenate([w_group[li], jnp.zeros((d, pad_g), F32),
                              w_expert[li], jnp.zeros((d, pad_e), F32)], axis=1)
    wr_hi = wr_f32.astype(BF16)
    wr_lo = (wr_f32 - wr_hi.astype(F32)).astype(BF16)
    wr = jnp.concatenate([wr_hi, wr_lo], axis=1)
    br = jnp.concatenate([b_group[li], jnp.full((pad_g,), MASK_VALUE, F32),
                          b_expert[li], jnp.zeros((pad_e,), F32)])[None, :]
    tm_r = min(OUT_PROJ_TM, t)
    ar = jnp.arange(tm_r)
    tri = (ar[:, None] < ar[None, :]).astype(BF16)

    w_in_bf = w_in[li].astype(BF16)
    wo_bf = w_o[li].astype(BF16)
    wg_bf = w_ple_gate[li].astype(BF16)
    wp_bf = w_ple_proj[li].astype(BF16)

    n_qk = 2 * d_attn
    qk, xn = _in_proj_qk(x2, g_mix[li][None, :], w_in_bf, n_qk, gqk, gsum)
    vbcu = _proj(xn, w_in_bf, n_qk, w_in_bf.shape[1] - n_qk)
    vbcu3 = vbcu.reshape(b, s, -1)
    attn = _attention(qk.reshape(b, s, -1), vbcu3, bias_tiles, gsub, lam)
    conv = _short_conv(vbcu3, conv_w[li], conv_b[li][None, :], g_conv[li][None, :])
    h1, hn, meta_i, wcol, counts = _out_proj(
        x2, attn.reshape(t, -1), conv.reshape(t, -1), wo_bf[:d_attn], wo_bf[d_attn:],
        g_ffn[li][None, :], wr, br, tri)

    eid = jnp.transpose(meta_i[:, 0:2, :], (1, 0, 2)).reshape(2, t)
    rank = jnp.transpose(meta_i[:, 2:4, :], (1, 0, 2)).reshape(2, t)
    cnt = counts[:, 0].astype(jnp.int32)
    tiles_per = (cnt + MOE_TM - 1) // MOE_TM
    experts = jnp.arange(N_EXPERTS, dtype=jnp.int32)
    tile_end = jnp.sum(jnp.where(experts[:, None] <= experts[None, :], tiles_per[:, None], 0),
                       axis=0)
    row_start = (tile_end - tiles_per) * MOE_TM
    n_tiles = (TOP_K * t) // MOE_TM + N_EXPERTS
    pos = rank + jnp.sum(jnp.where(eid[..., None] == experts, row_start, 0), axis=-1)
    tile_expert = jnp.minimum(
        jnp.sum((jnp.arange(n_tiles, dtype=jnp.int32)[:, None] >= tile_end[None, :])
                .astype(jnp.int32), axis=1),
        N_EXPERTS - 1)
    n_valid = tile_end[-1:]

    slabs = d // (2 * LANES)
    n_slots = n_tiles * MOE_TM
    slab_base = jnp.arange(slabs, dtype=jnp.int32) * n_slots
    row_idx = pos[:, None, :] + slab_base[None, :, None]

    xs = _sc_scatter_rows(hn.reshape(slabs * t, LANES), row_idx[0].reshape(1, -1),
                          row_idx[1].reshape(1, -1), slabs * n_slots)
    ys = _moe(xs.reshape(slabs, n_slots, LANES), w1, w3, w2, li, tile_expert, n_valid)
    ys_rows = ys.reshape(slabs * n_slots, LANES)
    p2 = p[li].reshape(t, -1)
    tc = t // COMBINE_CHUNKS
    out = None
    for c in range(COMBINE_CHUNKS):
        idx_c = row_idx[:, :, c * tc:(c + 1) * tc].reshape(1, -1)
        yg = _sc_gather_rows(ys_rows, idx_c).reshape(TOP_K, slabs, tc, LANES)
        out = _ple(h1, yg, wcol, p2, g_ple[li][None, :], wg_bf, wp_bf, c * tc, out)
    return out.reshape(b, s, d)
```

```python
import functools
import math

import jax
import jax.numpy as jnp
from jax import lax
from jax.experimental import pallas as pl
from jax.experimental.pallas import tpu as pltpu
from jax.experimental.pallas import tpu_sc as plsc

F32 = jnp.float32
BF16 = jnp.bfloat16

N_DIFF_HEADS = 8
DIFF_QK_DIM = 64
DIFF_V_DIM = 128
CHUNK = 64
NUM_BUCKETS = 32
MAX_DISTANCE = 128
CONV_WIDTH = 3
CONV_GROUP_DIM = 128
N_GROUPS = 4
EXPERTS_PER_GROUP = 8
N_EXPERTS = N_GROUPS * EXPERTS_PER_GROUP
TOP_K = 2
EPS = 1e-6
MASK_VALUE = -1e30
LAM_INIT = 0.8 - 0.6 * math.exp(-0.3 * 0)
LOG2E = math.log2(math.e)

LANES = 128
SUBLANES = 8
MXU_WIDTH = 256
VMEM_LIMIT_BYTES = 56 * 1024 * 1024

IN_PROJ_TM = 1024
IN_PROJ_TN = 1024
ATTN_TQ = 512
ATTN_TK = 256
CONV_TS = 512
OUT_PROJ_TM = 512
MOE_TM = 256
PLE_TM = 512
COMBINE_CHUNKS = 4
ROUTER_COLS = 128
SC_WINDOW = 128
SC_NUM_CORES = 2
SC_NUM_SUBCORES = 16


def _cparams(semantics):
    return pltpu.CompilerParams(dimension_semantics=semantics,
                                vmem_limit_bytes=VMEM_LIMIT_BYTES)


def _store_routed_rows(o_ref, x):
    d = x.shape[1]
    half = d // 2
    slabs = half // LANES
    xr = x.astype(BF16).astype(F32)
    lo = lax.bitcast_convert_type(xr[:, :half], jnp.uint32)
    hi = lax.bitcast_convert_type(xr[:, half:], jnp.uint32)
    packed = (lo >> 16) | (hi & jnp.uint32(0xFFFF0000))
    for r in range(slabs):
        o_ref[r] = packed[:, r * LANES:(r + 1) * LANES]


def _load_routed_rows(x_ref, dtype):
    parts = [x_ref[r] for r in range(x_ref.shape[0])]
    lo = [lax.bitcast_convert_type(w << 16, F32).astype(dtype) for w in parts]
    hi = [lax.bitcast_convert_type(w & jnp.uint32(0xFFFF0000), F32).astype(dtype) for w in parts]
    return jnp.concatenate(lo + hi, axis=1)


def _sc_mesh():
    return plsc.VectorSubcoreMesh(core_axis_name="core", subcore_axis_name="subcore",
                                  num_cores=SC_NUM_CORES, num_subcores=SC_NUM_SUBCORES)


def _sc_scatter_rows(rows, idx_a, idx_b, n_out_rows):
    n_src = rows.shape[0]

    @functools.partial(pl.kernel, mesh=_sc_mesh(), scratch_types=[],
                       out_type=jax.ShapeDtypeStruct((n_out_rows, LANES), rows.dtype))
    def scatter(x_hbm, ia_hbm, ib_hbm, o_hbm):
        def body(x_vmem, ia_vmem, ib_vmem):
            pltpu.sync_copy(x_vmem, o_hbm.at[ia_vmem.at[0]])
            pltpu.sync_copy(x_vmem, o_hbm.at[ib_vmem.at[0]])

        pltpu.emit_pipeline(
            body,
            grid=(n_src // SC_WINDOW,),
            in_specs=[pl.BlockSpec((SC_WINDOW, LANES), lambda i: (i, 0)),
                      pl.BlockSpec((1, SC_WINDOW), lambda i: (0, i)),
                      pl.BlockSpec((1, SC_WINDOW), lambda i: (0, i))],
            out_specs=[],
            core_axis_name=("core", "subcore"),
            dimension_semantics=(pltpu.PARALLEL,),
        )(x_hbm, ia_hbm, ib_hbm)

    return scatter(rows, idx_a, idx_b)


def _sc_gather_rows(table, idx):
    n_idx = idx.shape[1]

    @functools.partial(pl.kernel, mesh=_sc_mesh(), scratch_types=[],
                       out_type=jax.ShapeDtypeStruct((n_idx, LANES), table.dtype))
    def gather(t_hbm, i_hbm, o_hbm):
        def body(i_vmem, o_vmem):
            pltpu.sync_copy(t_hbm.at[i_vmem.at[0]], o_vmem)

        pltpu.emit_pipeline(
            body,
            grid=(n_idx // SC_WINDOW,),
            in_specs=[pl.BlockSpec((1, SC_WINDOW), lambda i: (0, i))],
            out_specs=[pl.BlockSpec((SC_WINDOW, LANES), lambda i: (i, 0))],
            core_axis_name=("core", "subcore"),
            dimension_semantics=(pltpu.PARALLEL,),
        )(i_hbm, o_hbm)

    return gather(table, idx)


def _proj_kernel(xn_ref, w_ref, o_ref):
    o_ref[...] = jnp.dot(xn_ref[...], w_ref[...],
                         preferred_element_type=F32).astype(o_ref.dtype)


def _in_proj_qk_kernel(x_ref, g_ref, w_ref, gqk_ref, gsum_ref, o_ref, xn_ref):
    @pl.when(pl.program_id(1) == 0)
    def _():
        x = x_ref[...]
        ms = jnp.mean(x * x, axis=-1, keepdims=True)
        xn_ref[...] = (x * lax.rsqrt(ms + EPS) * g_ref[...]).astype(BF16)

    acc = jnp.dot(xn_ref[...], w_ref[...], preferred_element_type=F32)
    for c in range(acc.shape[1] // MXU_WIDTH):
        sl = slice(c * MXU_WIDTH, (c + 1) * MXU_WIDTH)
        a = acc[:, sl]
        sq = a * a
        hi = sq.astype(BF16)
        lo = (sq - hi.astype(F32)).astype(BF16)
        ss = (jnp.dot(hi, gsum_ref[...], preferred_element_type=F32)
              + jnp.dot(lo, gsum_ref[...], preferred_element_type=F32))
        y = a * lax.rsqrt(ss * (1.0 / DIFF_QK_DIM) + EPS)
        o_ref[:, sl] = (y * gqk_ref[:, sl]).astype(o_ref.dtype)


def _in_proj_qk(x2, g_mix, w_in_bf, n_out, gqk, gsum):
    t, d = x2.shape
    tm, tn = min(IN_PROJ_TM, t), IN_PROJ_TN
    return pl.pallas_call(
        _in_proj_qk_kernel,
        grid=(t // tm, n_out // tn),
        in_specs=[
            pl.BlockSpec((tm, d), lambda i, j: (i, 0)),
            pl.BlockSpec((1, d), lambda i, j: (0, 0)),
            pl.BlockSpec((d, tn), lambda i, j: (0, j)),
            pl.BlockSpec((1, tn), lambda i, j: (0, j)),
            pl.BlockSpec((MXU_WIDTH, MXU_WIDTH), lambda i, j: (0, 0)),
        ],
        out_specs=[pl.BlockSpec((tm, tn), lambda i, j: (i, j)),
                   pl.BlockSpec((tm, d), lambda i, j: (i, 0))],
        out_shape=[jax.ShapeDtypeStruct((t, n_out), BF16),
                   jax.ShapeDtypeStruct((t, d), BF16)],
        compiler_params=_cparams(("parallel", "arbitrary")),
        name="in_proj_qk",
    )(x2, g_mix, w_in_bf, gqk, gsum)


def _proj(xn, w_bf, col0, n_out):
    t, d = xn.shape
    tm, tn = min(IN_PROJ_TM, t), IN_PROJ_TN
    jb0 = col0 // tn
    return pl.pallas_call(
        _proj_kernel,
        grid=(t // tm, n_out // tn),
        in_specs=[pl.BlockSpec((tm, d), lambda i, j: (i, 0)),
                  pl.BlockSpec((d, tn), lambda i, j: (0, jb0 + j))],
        out_specs=pl.BlockSpec((tm, tn), lambda i, j: (i, j)),
        out_shape=jax.ShapeDtypeStruct((t, n_out), BF16),
        compiler_params=_cparams(("parallel", "parallel")),
        name="in_proj_vbcu",
    )(xn, w_bf)


def _attn_kernel(lam_ref, q_ref, k_ref, v_ref, bias_ref, gsub_ref, o_ref,
                 qs_ref, acc_ref, s0_ref, s1_ref, p0_ref, p1_ref, m0_ref, m1_ref, *, tq, tk):
    qi = pl.program_id(2)
    dv = DIFF_V_DIM
    s_bufs, p_bufs, m_bufs = (s0_ref, s1_ref), (p0_ref, p1_ref), (m0_ref, m1_ref)

    q = q_ref[0]
    lane = lax.broadcasted_iota(jnp.int32, q.shape, 1)
    zero = jnp.zeros_like(q)
    qs_ref[0:tq, :] = jnp.where(lane < DIFF_QK_DIM, q, zero)
    qs_ref[tq:, :] = jnp.where(lane < DIFF_QK_DIM, zero, q)
    ones = jnp.ones((tk, dv), BF16)

    def block_start(step):
        blk = jnp.where(step < 2, 2 * qi + jnp.maximum(step, 0), step - 2)
        return pl.multiple_of(blk * tk, tk)

    def logits_stage(step, dst):
        dst[...] = lax.dot_general(qs_ref[...], k_ref[0, pl.ds(block_start(step), tk), :],
                                   (((1,), (1,)), ((), ())), preferred_element_type=F32)

    def pv_stage(step, buf, first=False):
        vx = jnp.concatenate([v_ref[0, pl.ds(block_start(step), tk), :], ones], axis=1)
        pv = jnp.dot(p_bufs[buf][...], vx, preferred_element_type=F32)
        if first:
            acc_ref[...] = pv
        else:
            alpha = jnp.exp2(m_bufs[1 - buf][...] - m_bufs[buf][...])
            acc_ref[...] = jnp.concatenate([alpha, alpha], axis=1) * acc_ref[...] + pv

    def softmax_stage(buf, bias_tile, first=False):
        s = s_bufs[buf][...]
        if bias_tile is not None:
            bias = bias_ref[0, bias_tile]
            s = jnp.concatenate([s[:tq] + bias, s[tq:] + bias], axis=0)
        row_max = jnp.max(s, axis=-1, keepdims=True)
        if first:
            m_new = jnp.broadcast_to(row_max, m_bufs[buf].shape)
        else:
            m_new = jnp.maximum(m_bufs[1 - buf][...], row_max)
        m_bufs[buf][...] = m_new
        p = jnp.exp2(s - jnp.concatenate([m_new] * (tk // LANES), axis=1))
        p_bufs[buf][...] = p.astype(BF16)

    def pair(step0, bias0, bias1, lookahead=True, first=False):
        logits_stage(step0 + 1, s_bufs[1])
        if not first:
            pv_stage(step0 - 1, 1)
        softmax_stage(0, bias0, first)
        if lookahead:
            logits_stage(step0 + 2, s_bufs[0])
        pv_stage(step0, 0, first)
        softmax_stage(1, bias1)

    logits_stage(0, s_bufs[0])
    pair(0, 0, 1, first=True)

    n_far_pairs = jnp.maximum(qi - 1, 0)

    def far_hexa(jj, carry):
        pair(6 * jj + 2, None, None)
        pair(6 * jj + 4, None, None)
        pair(6 * jj + 6, None, None)
        return carry

    lax.fori_loop(0, n_far_pairs // 3, far_hexa, 0)
    rem_base = 2 + 6 * (n_far_pairs // 3)

    @pl.when(n_far_pairs % 3 == 2)
    def _():
        pair(rem_base, None, None)
        pair(rem_base + 2, None, None)

    @pl.when(n_far_pairs % 3 == 1)
    def _():
        pair(rem_base, None, None)

    @pl.when(qi > 0)
    def _():
        pair(2 * qi, None, 2, lookahead=False)

    pv_stage(2 * qi + 1, 1)

    acc = acc_ref[...]
    o = (acc[:tq, :dv] / acc[:tq, dv:]) - lam_ref[0] * (acc[tq:, :dv] / acc[tq:, dv:])
    ms = jnp.mean(o * o, axis=-1, keepdims=True)
    o_ref[0] = (o * lax.rsqrt(ms + EPS) * gsub_ref[...]).astype(o_ref.dtype)


def _attention(qk3, vbcu3, bias_tiles, gsub, lam):
    b, s, _ = qk3.shape
    h = N_DIFF_HEADS
    tq, tk = ATTN_TQ, ATTN_TK
    assert tq == 2 * tk and s % tq == 0
    kern = functools.partial(_attn_kernel, tq=tq, tk=tk)
    return pl.pallas_call(
        kern,
        grid=(b, h, s // tq),
        in_specs=[
            pl.BlockSpec(memory_space=pltpu.SMEM),
            pl.BlockSpec((1, tq, LANES), lambda bi, hi, qi: (bi, qi, hi)),
            pl.BlockSpec((1, s, LANES), lambda bi, hi, qi: (bi, 0, h + hi)),
            pl.BlockSpec((1, s, LANES), lambda bi, hi, qi: (bi, 0, hi)),
            pl.BlockSpec((1, 3, tq, tk), lambda bi, hi, qi: (hi, 0, 0, 0)),
            pl.BlockSpec((1, LANES), lambda bi, hi, qi: (0, 0)),
        ],
        out_specs=pl.BlockSpec((1, tq, LANES), lambda bi, hi, qi: (bi, qi, hi)),
        out_shape=jax.ShapeDtypeStruct((b, s, h * DIFF_V_DIM), BF16),
        scratch_shapes=[pltpu.VMEM((2 * tq, LANES), BF16),
                        pltpu.VMEM((2 * tq, 2 * DIFF_V_DIM), F32),
                        pltpu.VMEM((2 * tq, tk), F32),
                        pltpu.VMEM((2 * tq, tk), F32),
                        pltpu.VMEM((2 * tq, tk), BF16),
                        pltpu.VMEM((2 * tq, tk), BF16),
                        pltpu.VMEM((2 * tq, LANES), F32),
                        pltpu.VMEM((2 * tq, LANES), F32)],
        compiler_params=_cparams(("parallel", "parallel", "arbitrary")),
        name="diff_attention",
    )(lam, qk3, qk3, vbcu3, bias_tiles, gsub)


def _rel_bucket(rel):
    nb = NUM_BUCKETS // 2
    max_exact = nb // 2
    n = jnp.abs(rel)
    n2 = n * n
    large = max_exact + sum((n2 >= (max_exact * max_exact) * (2 ** k)).astype(jnp.int32)
                            for k in range(1, nb - max_exact))
    return jnp.where(rel > 0, nb, 0) + jnp.where(n < max_exact, n, large)


def _bias_kernel(rb_ref, o_ref, *, tk):
    h = pl.program_id(0)
    tq = o_ref.shape[2]
    qpos = lax.broadcasted_iota(jnp.int32, (tq, tk), 0)
    kcol = lax.broadcasted_iota(jnp.int32, (tq, tk), 1)
    chunk_shift = CHUNK.bit_length() - 1
    far_bias = rb_ref[NUM_BUCKETS // 2 - 1, h]
    for tile, offset in enumerate((0, tk, -tk)):
        kpos = kcol + offset
        bucket = _rel_bucket(kpos - qpos)
        bias = jnp.zeros((tq, tk), F32)
        for b in range(NUM_BUCKETS):
            bias = jnp.where(bucket == b, rb_ref[b, h], bias)
        mask = (kpos >> chunk_shift) <= (qpos >> chunk_shift)
        o_ref[0, tile] = jnp.where(mask, (bias - far_bias) * LOG2E, MASK_VALUE)


def _bias_tiles(rel_bias, tq, tk):
    assert tk >= MAX_DISTANCE
    assert CHUNK & (CHUNK - 1) == 0
    n_heads = rel_bias.shape[1]
    return pl.pallas_call(
        functools.partial(_bias_kernel, tk=tk),
        grid=(n_heads,),
        in_specs=[pl.BlockSpec(memory_space=pltpu.SMEM)],
        out_specs=pl.BlockSpec((1, 3, tq, tk), lambda h: (h, 0, 0, 0)),
        out_shape=jax.ShapeDtypeStruct((n_heads, 3, tq, tk), F32),
        compiler_params=_cparams(("parallel",)),
        name="bias_tiles",
    )(rel_bias)


def _conv_kernel(b_ref, c_ref, u_ref, cp_ref, up_ref, w_ref, cb_ref, g_ref, o_ref, buf_ref):
    si = pl.program_id(1)
    ts = o_ref.shape[1]
    cu = c_ref[0].astype(F32) * u_ref[0].astype(F32)
    prev = cp_ref[0].astype(F32) * up_ref[0].astype(F32)
    buf_ref[0:SUBLANES, :] = jnp.where(si > 0, prev, 0.0)
    buf_ref[SUBLANES:, :] = cu
    conv = (w_ref[0:1, :] * buf_ref[pl.ds(SUBLANES - 2, ts), :]
            + w_ref[1:2, :] * buf_ref[pl.ds(SUBLANES - 1, ts), :]
            + w_ref[2:3, :] * cu)
    z = b_ref[0].astype(F32) * (conv + cb_ref[...])
    for c in range(z.shape[1] // CONV_GROUP_DIM):
        sl = slice(c * CONV_GROUP_DIM, (c + 1) * CONV_GROUP_DIM)
        zc = z[:, sl]
        ms = jnp.mean(zc * zc, axis=-1, keepdims=True)
        o_ref[0, :, sl] = (zc * lax.rsqrt(ms + EPS) * g_ref[:, sl]).astype(o_ref.dtype)


def _short_conv(proj3, conv_w, conv_b, g_conv):
    b, s, n = proj3.shape
    dc = conv_w.shape[1]
    ts = min(CONV_TS, s)
    col0 = (n - 3 * dc) // dc
    halo = ts // SUBLANES

    def main(col):
        return pl.BlockSpec((1, ts, dc), lambda bi, si: (bi, si, col))

    def prev(col):
        return pl.BlockSpec((1, SUBLANES, dc),
                            lambda bi, si: (bi, jnp.maximum(si * halo - 1, 0), col))

    return pl.pallas_call(
        _conv_kernel,
        grid=(b, s // ts),
        in_specs=[main(col0), main(col0 + 1), main(col0 + 2), prev(col0 + 1), prev(col0 + 2),
                  pl.BlockSpec((CONV_WIDTH, dc), lambda bi, si: (0, 0)),
                  pl.BlockSpec((1, dc), lambda bi, si: (0, 0)),
                  pl.BlockSpec((1, dc), lambda bi, si: (0, 0))],
        out_specs=pl.BlockSpec((1, ts, dc), lambda bi, si: (bi, si, 0)),
        out_shape=jax.ShapeDtypeStruct((b, s, dc), BF16),
        scratch_shapes=[pltpu.VMEM((ts + SUBLANES, dc), F32)],
        compiler_params=_cparams(("parallel", "parallel")),
        name="short_conv",
    )(proj3, proj3, proj3, proj3, proj3, conv_w, conv_b, g_conv)


def _out_proj_kernel(x_ref, a_ref, c_ref, wa_ref, wc_ref, g_ref, wr_ref, br_ref, tri_ref,
                     h_ref, hn_ref, mi_ref, wcol_ref, cnt_ref, carry_ref):
    i = pl.program_id(0)
    tm = x_ref.shape[0]

    @pl.when(i == 0)
    def _():
        carry_ref[...] = jnp.zeros_like(carry_ref)

    h = (x_ref[...]
         + jnp.dot(a_ref[...], wa_ref[...], preferred_element_type=F32)
         + jnp.dot(c_ref[...], wc_ref[...], preferred_element_type=F32))
    h_ref[...] = h
    ms = jnp.mean(h * h, axis=-1, keepdims=True)
    hn = h * lax.rsqrt(ms + EPS) * g_ref[...]
    hn_hi = hn.astype(BF16)
    _store_routed_rows(hn_ref, hn)
    hn_lo = (hn - hn_hi.astype(F32)).astype(BF16)

    r_hi = jnp.dot(hn_hi, wr_ref[...], preferred_element_type=F32)
    r_lo = jnp.dot(hn_lo, wr_ref[...], preferred_element_type=F32)
    logits = (r_hi[:, :ROUTER_COLS] + r_hi[:, ROUTER_COLS:] + r_lo[:, :ROUTER_COLS]
              + br_ref[...])
    lt = logits.T

    e = EXPERTS_PER_GROUP
    row = lax.broadcasted_iota(jnp.int32, (e, tm), 0)

    def first_argmax(v):
        vmax = jnp.max(v, axis=0, keepdims=True)
        idx = jnp.min(jnp.where(v == vmax, row, e), axis=0, keepdims=True)
        return vmax, idx

    gl = lt[0:e]
    gmax, g = first_argmax(gl)
    p_g = 1.0 / jnp.sum(jnp.exp(gl - gmax), axis=0, keepdims=True)
    ing = lt[e:2 * e]
    for gi in range(1, N_GROUPS):
        ing = jnp.where(g == gi, lt[(gi + 1) * e:(gi + 2) * e], ing)
    v1, i1 = first_argmax(ing)
    v2, i2 = first_argmax(jnp.where(row == i1, -jnp.inf, ing))
    ex = jnp.exp(v2 - v1)
    w1 = p_g / (1.0 + ex)
    w2 = w1 * ex
    e1 = g * e + i1
    e2 = g * e + i2

    erow = lax.broadcasted_iota(jnp.int32, (N_EXPERTS, tm), 0)
    oh1 = (erow == e1).astype(F32)
    oh2 = (erow == e2).astype(F32)
    oh = jnp.concatenate([oh1, oh2], axis=0).astype(BF16)
    pre = jnp.dot(oh, tri_ref[...], preferred_element_type=F32)
    cnt1 = jnp.sum(oh1, axis=1, keepdims=True)
    cnt2 = jnp.sum(oh2, axis=1, keepdims=True)
    carry = carry_ref[:, 0:1]
    r1 = jnp.sum(oh1 * (pre[:N_EXPERTS] + carry), axis=0, keepdims=True)
    r2 = jnp.sum(oh2 * (pre[N_EXPERTS:] + carry + cnt1), axis=0, keepdims=True)
    new_carry = carry + cnt1 + cnt2
    carry_ref[...] = jnp.broadcast_to(new_carry, carry_ref.shape)
    cnt_ref[...] = jnp.broadcast_to(new_carry, cnt_ref.shape)

    mi_ref[0] = jnp.concatenate(
        [e1, e2, r1.astype(jnp.int32), r2.astype(jnp.int32),
         jnp.zeros((SUBLANES - 4, tm), jnp.int32)], axis=0)
    wrow = jnp.concatenate([w1, w2, jnp.zeros((ROUTER_COLS - 2, tm), F32)], axis=0)
    wcol_ref[...] = wrow.T


def _out_proj(x2, attn2, conv2, wo_a, wo_c, g_ffn, wr, br, tri):
    t, d = x2.shape
    tm = min(OUT_PROJ_TM, t)
    nt = t // tm
    da, dc = attn2.shape[1], conv2.shape[1]
    slabs = d // (2 * LANES)
    const = lambda i: (0, 0)
    return pl.pallas_call(
        _out_proj_kernel,
        grid=(nt,),
        in_specs=[
            pl.BlockSpec((tm, d), lambda i: (i, 0)),
            pl.BlockSpec((tm, da), lambda i: (i, 0)),
            pl.BlockSpec((tm, dc), lambda i: (i, 0)),
            pl.BlockSpec((da, d), const),
            pl.BlockSpec((dc, d), const),
            pl.BlockSpec((1, d), const),
            pl.BlockSpec((d, 2 * ROUTER_COLS), const),
            pl.BlockSpec((1, ROUTER_COLS), const),
            pl.BlockSpec((tm, tm), const),
        ],
        out_specs=[
            pl.BlockSpec((tm, d), lambda i: (i, 0)),
            pl.BlockSpec((slabs, tm, LANES), lambda i: (0, i, 0)),
            pl.BlockSpec((1, SUBLANES, tm), lambda i: (i, 0, 0)),
            pl.BlockSpec((tm, ROUTER_COLS), lambda i: (i, 0)),
            pl.BlockSpec((N_EXPERTS, LANES), const),
        ],
        out_shape=[
            jax.ShapeDtypeStruct((t, d), F32),
            jax.ShapeDtypeStruct((slabs, t, LANES), jnp.uint32),
            jax.ShapeDtypeStruct((nt, SUBLANES, tm), jnp.int32),
            jax.ShapeDtypeStruct((t, ROUTER_COLS), F32),
            jax.ShapeDtypeStruct((N_EXPERTS, LANES), F32),
        ],
        scratch_shapes=[pltpu.VMEM((N_EXPERTS, LANES), F32)],
        compiler_params=_cparams(("arbitrary",)),
        name="out_proj_router",
    )(x2, attn2, conv2, wo_a, wo_c, g_ffn, wr, br, tri)


def _moe_kernel(te_ref, nv_ref, x_ref, w1_ref, w3_ref, w2_ref, o_ref, w13_ref, w2b_ref):
    i = pl.program_id(0)
    dff = w2_ref.shape[2]

    @pl.when(i < nv_ref[0])
    def _():
        @pl.when(jnp.logical_or(i == 0, te_ref[i] != te_ref[jnp.maximum(i - 1, 0)]))
        def _():
            w13_ref[:, :dff] = w1_ref[0, 0].astype(BF16)
            w13_ref[:, dff:] = w3_ref[0, 0].astype(BF16)
            w2b_ref[...] = w2_ref[0, 0].astype(BF16)

        x = _load_routed_rows(x_ref, BF16)
        ab = jnp.dot(x, w13_ref[...], preferred_element_type=F32)
        a, b = ab[:, :dff], ab[:, dff:]
        hid = (a * jax.nn.sigmoid(a) * b).astype(BF16)
        _store_routed_rows(o_ref, jnp.dot(hid, w2b_ref[...], preferred_element_type=F32))

    @pl.when(i >= nv_ref[0])
    def _():
        o_ref[...] = jnp.zeros_like(o_ref)


def _moe(xs, w1, w3, w2, layer, tile_expert, n_valid):
    d, dff = w2.shape[3], w2.shape[2]
    slabs = xs.shape[0]
    nt = xs.shape[1] // MOE_TM
    grid_spec = pltpu.PrefetchScalarGridSpec(
        num_scalar_prefetch=2,
        grid=(nt,),
        in_specs=[
            pl.BlockSpec((slabs, MOE_TM, LANES),
                         lambda i, te, nv: (0, jnp.minimum(i, nv[0] - 1), 0)),
            pl.BlockSpec((1, 1, d, dff), lambda i, te, nv: (layer, te[i], 0, 0)),
            pl.BlockSpec((1, 1, d, dff), lambda i, te, nv: (layer, te[i], 0, 0)),
            pl.BlockSpec((1, 1, dff, d), lambda i, te, nv: (layer, te[i], 0, 0)),
        ],
        out_specs=pl.BlockSpec((slabs, MOE_TM, LANES), lambda i, te, nv: (0, i, 0)),
        scratch_shapes=[pltpu.VMEM((d, 2 * dff), BF16), pltpu.VMEM((dff, d), BF16)],
    )
    return pl.pallas_call(
        _moe_kernel,
        grid_spec=grid_spec,
        out_shape=jax.ShapeDtypeStruct(xs.shape, jnp.uint32),
        compiler_params=_cparams(("arbitrary",)),
        name="moe_grouped",
    )(tile_expert, n_valid, xs, w1, w3, w2)


def _ple_kernel(h_ref, y1_ref, y2_ref, wcol_ref, p_ref, g_ref, wg_ref, wp_ref, *rest):
    o_ref = rest[-1]
    wcol = wcol_ref[...]
    h = (h_ref[...]
         + wcol[:, 0:1] * _load_routed_rows(y1_ref.at[0], F32)
         + wcol[:, 1:2] * _load_routed_rows(y2_ref.at[0], F32))
    ms = jnp.mean(h * h, axis=-1, keepdims=True)
    hn = (h * lax.rsqrt(ms + EPS) * g_ref[...]).astype(BF16)
    gate = jax.nn.sigmoid(jnp.dot(hn, wg_ref[...], preferred_element_type=F32))
    emb = jnp.dot(p_ref[...].astype(BF16), wp_ref[...], preferred_element_type=F32)
    o_ref[...] = h + gate * emb


def _ple(h1, yg, wcol, p2, g_ple, wg, wp, row0, out_prev):
    t, d = h1.shape
    n = yg.shape[2]
    tm = min(PLE_TM, n)
    dp = p2.shape[1]
    slabs = d // (2 * LANES)
    b0 = row0 // tm
    const = lambda i: (0, 0)
    row = lambda i: (b0 + i, 0)
    in_specs = [
        pl.BlockSpec((tm, d), row),
        pl.BlockSpec((1, slabs, tm, LANES), lambda i: (0, 0, i, 0)),
        pl.BlockSpec((1, slabs, tm, LANES), lambda i: (1, 0, i, 0)),
        pl.BlockSpec((tm, ROUTER_COLS), row),
        pl.BlockSpec((tm, dp), row),
        pl.BlockSpec((1, d), const),
        pl.BlockSpec((d, d), const),
        pl.BlockSpec((dp, d), const),
    ]
    args = [h1, yg, yg, wcol, p2, g_ple, wg, wp]
    aliases = {}
    if out_prev is not None:
        in_specs.append(pl.BlockSpec(memory_space=pl.ANY))
        aliases = {len(args): 0}
        args.append(out_prev)
    return pl.pallas_call(
        _ple_kernel,
        grid=(n // tm,),
        in_specs=in_specs,
        out_specs=pl.BlockSpec((tm, d), row),
        out_shape=jax.ShapeDtypeStruct((t, d), F32),
        input_output_aliases=aliases,
        compiler_params=_cparams(("parallel",)),
        name="combine_ple",
    )(*args)


def kernel(x, p, rel_bias, g_mix, w_in, g_q, g_k, lam_q1, lam_k1, lam_q2, lam_k2, g_subln,
           conv_w, conv_b, g_conv, w_o, g_ffn, w_group, b_group, w_expert, b_expert,
           w1, w3, w2, g_ple, w_ple_gate, w_ple_proj):
    depth = g_mix.shape[0]
    assert depth == 1
    li = 0
    b, s, d = x.shape
    t = b * s
    d_attn = N_DIFF_HEADS * DIFF_V_DIM
    x2 = x.reshape(t, d)

    n_groups_qk = d_attn // DIFF_QK_DIM
    gqk = jnp.concatenate([jnp.tile(g_q[li] * (DIFF_QK_DIM ** -0.5 * LOG2E), n_groups_qk),
                           jnp.tile(g_k[li], n_groups_qk)])[None, :].astype(F32)
    blk = jnp.arange(MXU_WIDTH) // DIFF_QK_DIM
    gsum = (blk[:, None] == blk[None, :]).astype(BF16)
    lam = (jnp.exp(jnp.sum(lam_q1[li] * lam_k1[li])) - jnp.exp(jnp.sum(lam_q2[li] * lam_k2[li]))
           + LAM_INIT).reshape(1).astype(F32)
    gsub = (g_subln[li] * (1.0 - LAM_INIT))[None, :].astype(F32)
    bias_tiles = _bias_tiles(rel_bias, ATTN_TQ, ATTN_TK)

    pad_g = EXPERTS_PER_GROUP - N_GROUPS
    pad_e = ROUTER_COLS - EXPERTS_PER_GROUP - N_EXPERTS
    wr_f32 = jnp.concatenate([w_group[li], jnp.zeros((d, pad_g), F32),
                              w_expert[li], jnp.zeros((d, pad_e), F32)], axis=1)
    wr_hi = wr_f32.astype(BF16)
    wr_lo = (wr_f32 - wr_hi.astype(F32)).astype(BF16)
    wr = jnp.concatenate([wr_hi, wr_lo], axis=1)
    br = jnp.concatenate([b_group[li], jnp.full((pad_g,), MASK_VALUE, F32),
                          b_expert[li], jnp.zeros((pad_e,), F32)])[None, :]
    tm_r = min(OUT_PROJ_TM, t)
    ar = jnp.arange(tm_r)
    tri = (ar[:, None] < ar[None, :]).astype(BF16)

    w_in_bf = w_in[li].astype(BF16)
    wo_bf = w_o[li].astype(BF16)
    wg_bf = w_ple_gate[li].astype(BF16)
    wp_bf = w_ple_proj[li].astype(BF16)

    n_qk = 2 * d_attn
    qk, xn = _in_proj_qk(x2, g_mix[li][None, :], w_in_bf, n_qk, gqk, gsum)
    vbcu = _proj(xn, w_in_bf, n_qk, w_in_bf.shape[1] - n_qk)
    vbcu3 = vbcu.reshape(b, s, -1)
    attn = _attention(qk.reshape(b, s, -1), vbcu3, bias_tiles, gsub, lam)
    conv = _short_conv(vbcu3, conv_w[li], conv_b[li][None, :], g_conv[li][None, :])
    h1, hn, meta_i, wcol, counts = _out_proj(
        x2, attn.reshape(t, -1), conv.reshape(t, -1), wo_bf[:d_attn], wo_bf[d_attn:],
        g_ffn[li][None, :], wr, br, tri)

    eid = jnp.transpose(meta_i[:, 0:2, :], (1, 0, 2)).reshape(2, t)
    rank = jnp.transpose(meta_i[:, 2:4, :], (1, 0, 2)).reshape(2, t)
    cnt = counts[:, 0].astype(jnp.int32)
    tiles_per = (cnt + MOE_TM - 1) // MOE_TM
    experts = jnp.arange(N_EXPERTS, dtype=jnp.int32)
    tile_end = jnp.sum(jnp.where(experts[:, None] <= experts[None, :], tiles_per[:, None], 0),
                       axis=0)
    row_start = (tile_end - tiles_per) * MOE_TM
    n_tiles = (TOP_K * t) // MOE_TM + N_EXPERTS
    pos = rank + jnp.sum(jnp.where(eid[..., None] == experts, row_start, 0), axis=-1)
    tile_expert = jnp.minimum(
        jnp.sum((jnp.arange(n_tiles, dtype=jnp.int32)[:, None] >= tile_end[None, :])
                .astype(jnp.int32), axis=1),
        N_EXPERTS - 1)
    n_valid = tile_end[-1:]

    slabs = d // (2 * LANES)
    n_slots = n_tiles * MOE_TM
    slab_base = jnp.arange(slabs, dtype=jnp.int32) * n_slots
    row_idx = pos[:, None, :] + slab_base[None, :, None]

    xs = _sc_scatter_rows(hn.reshape(slabs * t, LANES), row_idx[0].reshape(1, -1),
                          row_idx[1].reshape(1, -1), slabs * n_slots)
    ys = _moe(xs.reshape(slabs, n_slots, LANES), w1, w3, w2, li, tile_expert, n_valid)
    ys_rows = ys.reshape(slabs * n_slots, LANES)
    p2 = p[li].reshape(t, -1)
    tc = t // COMBINE_CHUNKS
    out = None
    for c in range(COMBINE_CHUNKS):
        idx_c = row_idx[:, :, c * tc:(c + 1) * tc].reshape(1, -1)
        yg = _sc_gather_rows(ys_rows, idx_c).reshape(TOP_K, slabs, tc, LANES)
        out = _ple(h1, yg, wcol, p2, g_ple[li][None, :], wg_bf, wp_bf, c * tc, out)
    return out.reshape(b, s, d)
```
